```python
import jax, jax.numpy as jnp
from jax import lax
import numpy as np

D_MODEL = 2048
BATCH = 2
SEQ = 4096
DEPTH = 1
DEC_BATCH = 8
DEC_SEQ = 1
PAST_LEN = 16384
PAGE_SIZE = 128

HEAD_DIM = 128
N_Q_HEADS = D_MODEL // HEAD_DIM
N_KV_HEADS = N_Q_HEADS // 4
GROUP = N_Q_HEADS // N_KV_HEADS
W_Q = N_Q_HEADS * HEAD_DIM
W_KV = N_KV_HEADS * HEAD_DIM
D_FF = 4 * D_MODEL
N_META = 16
BLOCK_Q = 128
EPS = 1e-6
FORGET_W_SCALE = 0.1
FORGET_OFFSET = 2 * W_Q + 4 * W_KV
SPLITS = (W_Q, W_KV, W_KV, W_Q, W_KV, W_KV, N_Q_HEADS, D_MODEL, D_MODEL)
PROJ_OUT = 2 * W_Q + 4 * W_KV + N_Q_HEADS + 2 * D_MODEL

kernel_name = 'stickbreak_fox_hybrid_step'


def _rms_norm(x, g):
    xf = x.astype(jnp.float32)
    y = xf * lax.rsqrt(jnp.mean(xf * xf, axis=-1, keepdims=True) + EPS)
    return (y * g.astype(jnp.float32)).astype(x.dtype)


def _project(h, w_in, b_f, g_q, g_k):
    B, L = h.shape[0], h.shape[1]
    proj = jnp.einsum('bld,de->ble', h, w_in)
    idx = []
    acc = 0
    for s in SPLITS[:-1]:
        acc += s
        idx.append(acc)
    sq, sk, sv, fq, fk, fv, fl, gs, gf = jnp.split(proj, idx, axis=-1)
    heads = lambda a, n: a.reshape(B, L, n, HEAD_DIM)
    sq, sk, sv = heads(sq, N_Q_HEADS), heads(sk, N_KV_HEADS), heads(sv, N_KV_HEADS)
    fq = _rms_norm(heads(fq, N_Q_HEADS), g_q)
    fk = _rms_norm(heads(fk, N_KV_HEADS), g_k)
    fv = heads(fv, N_KV_HEADS)
    log_f = jax.nn.log_sigmoid(fl.astype(jnp.float32) + b_f.astype(jnp.float32))
    return sq, sk, sv, fq, fk, fv, log_f, jax.nn.sigmoid(gs), jax.nn.sigmoid(gf)


def _stick_breaking(q, k, v, q_pos, k_pos):
    B, Q = q.shape[0], q.shape[1]
    qg = q.reshape(B, Q, N_KV_HEADS, GROUP, HEAD_DIM)
    z = jnp.einsum('bqngd,bknd->bngqk', qg, k, preferred_element_type=jnp.float32) * (HEAD_DIM ** -0.5)
    before = k_pos[None, :] < q_pos[:, None]
    log_keep = jnp.where(before, -jax.nn.softplus(z), 0.0)
    log_between = lax.cumsum(log_keep, axis=4, reverse=True) - log_keep
    w = jnp.where(before, jnp.exp(jax.nn.log_sigmoid(z) + log_between), 0.0)
    o = jnp.einsum('bngqk,bknd->bqngd', w.astype(v.dtype), v)
    return o.reshape(B, Q, N_Q_HEADS, HEAD_DIM)


def _forgetting(q, k, v, cum_q, cum_k, q_pos, k_pos):
    B, Q = q.shape[0], q.shape[1]
    K = k.shape[1]
    qg = q.reshape(B, Q, N_KV_HEADS, GROUP, HEAD_DIM)
    z = jnp.einsum('bqngd,bknd->bngqk', qg, k, preferred_element_type=jnp.float32) * (HEAD_DIM ** -0.5)
    cq = cum_q.astype(jnp.float32).reshape(B, Q, N_KV_HEADS, GROUP).transpose(0, 2, 3, 1)[..., None]
    ck = cum_k.astype(jnp.float32).reshape(B, K, N_KV_HEADS, GROUP).transpose(0, 2, 3, 1)[..., None, :]
    z = jnp.where(k_pos[None, :] <= q_pos[:, None], z + cq - ck, -jnp.inf)
    p = jax.nn.softmax(z, axis=-1)
    o = jnp.einsum('bngqk,bknd->bqngd', p.astype(v.dtype), v)
    return o.reshape(B, Q, N_Q_HEADS, HEAD_DIM)


def _merge_and_ffn(x, o_sb, o_fx, g_sb, g_fx, w_o, g_ffn, w_up, w_down):
    B, L = x.shape[0], x.shape[1]
    merged = g_sb * o_sb.reshape(B, L, W_Q) + g_fx * o_fx.reshape(B, L, W_Q)
    x = x + jnp.einsum('blm,md->bld', merged, w_o)
    h = _rms_norm(x, g_ffn)
    u = jnp.square(jax.nn.relu(jnp.einsum('bld,df->blf', h, w_up)))
    return x + jnp.einsum('blf,fd->bld', u, w_down)


def _prompt_layer(x, g_mix, w_in, b_f, g_q, g_k, w_o, g_ffn, w_up, w_down):
    B, L = x.shape[0], x.shape[1]
    h = _rms_norm(x, g_mix)
    sq, sk, sv, fq, fk, fv, log_f, g_sb, g_fx = _project(h, w_in, b_f, g_q, g_k)
    cum_f = jnp.cumsum(log_f, axis=1)
    pos = jnp.arange(L)
    mpos = pos[:N_META]
    o_sb_meta = _stick_breaking(sq[:, :N_META], sk[:, :N_META], sv[:, :N_META], mpos, mpos)
    o_fx_meta = _forgetting(fq[:, :N_META], fk[:, :N_META], fv[:, :N_META],
                            cum_f[:, :N_META], cum_f[:, :N_META], mpos, mpos)

    def block(i):
        start = N_META + i * BLOCK_Q
        q_pos = start + jnp.arange(BLOCK_Q)
        take = lambda a: lax.dynamic_slice_in_dim(a, start, BLOCK_Q, axis=1)
        return (_stick_breaking(take(sq), sk, sv, q_pos, pos),
                _forgetting(take(fq), fk, fv, take(cum_f), cum_f, q_pos, pos))

    n_blocks = (L - N_META) // BLOCK_Q
    o_sb_blk, o_fx_blk = lax.map(block, jnp.arange(n_blocks))
    unblock = lambda o: jnp.swapaxes(o, 0, 1).reshape(B, L - N_META, N_Q_HEADS, HEAD_DIM)
    o_sb = jnp.concatenate([o_sb_meta, unblock(o_sb_blk)], axis=1)
    o_fx = jnp.concatenate([o_fx_meta, unblock(o_fx_blk)], axis=1)
    y = _merge_and_ffn(x, o_sb, o_fx, g_sb, g_fx, w_o, g_ffn, w_up, w_down)
    return y, (sk, sv, fk, fv, log_f)


def _sample_layer(x, c_sk, c_sv, c_fk, c_fv, c_lf, page_table,
                  g_mix, w_in, b_f, g_q, g_k, w_o, g_ffn, w_up, w_down):
    B, n = x.shape[0], x.shape[1]
    past = page_table.shape[1] * c_sk.shape[1]
    h = _rms_norm(x, g_mix)
    sq, sk, sv, fq, fk, fv, log_f, g_sb, g_fx = _project(h, w_in, b_f, g_q, g_k)

    def with_past(cache, new):
        rows = cache[page_table].reshape((B, past) + cache.shape[2:])
        return jnp.concatenate([rows.astype(new.dtype), new], axis=1)

    all_sk, all_sv = with_past(c_sk, sk), with_past(c_sv, sv)
    all_fk, all_fv = with_past(c_fk, fk), with_past(c_fv, fv)
    cum_f = jnp.cumsum(with_past(c_lf, log_f), axis=1)
    k_pos = jnp.arange(past + n)
    q_pos = past + jnp.arange(n)
    o_sb = _stick_breaking(sq, all_sk, all_sv, q_pos, k_pos)
    o_fx = _forgetting(fq, all_fk, all_fv, cum_f[:, past:], cum_f, q_pos, k_pos)
    y = _merge_and_ffn(x, o_sb, o_fx, g_sb, g_fx, w_o, g_ffn, w_up, w_down)
    return y, (sk, sv, fk, fv, log_f)


def _stack(states, j):
    return jnp.stack([s[j] for s in states])


def setup_inputs(seed: int = 0) -> dict:
    key = jax.random.key(seed)
    ks = jax.random.split(key, 20)
    f32 = jnp.float32
    nrm = lambda k, shape, scale=1.0: scale * jax.random.normal(k, shape, f32)
    n_pages = PAST_LEN // PAGE_SIZE
    used = DEC_BATCH * n_pages
    pool = used + (used + 3) // 4
    kv_shape = (DEPTH, pool, PAGE_SIZE, N_KV_HEADS, HEAD_DIM)
    w_in = nrm(ks[10], (DEPTH, D_MODEL, PROJ_OUT), D_MODEL ** -0.5)
    w_in = w_in.at[:, :, FORGET_OFFSET:FORGET_OFFSET + N_Q_HEADS].multiply(FORGET_W_SCALE)
    return {
        'x_prompt': nrm(ks[0], (BATCH, SEQ, D_MODEL)),
        'x_sample': nrm(ks[1], (DEC_BATCH, DEC_SEQ, D_MODEL)),
        'cache_sb_k': nrm(ks[2], kv_shape),
        'cache_sb_v': nrm(ks[3], kv_shape),
        'cache_fox_k': nrm(ks[4], kv_shape),
        'cache_fox_v': nrm(ks[5], kv_shape),
        'cache_fox_logf': jax.nn.log_sigmoid(nrm(ks[6], (DEPTH, pool, PAGE_SIZE, N_Q_HEADS), 0.5) + 3.5),
        'page_table': jax.random.permutation(ks[7], pool)[:used].reshape(DEC_BATCH, n_pages).astype(jnp.int32),
        'meta_tokens': nrm(ks[8], (N_META, D_MODEL)),
        'g_mix': 1.0 + nrm(ks[9], (DEPTH, D_MODEL), 0.02),
        'w_in': w_in,
        'b_forget': jax.random.uniform(ks[11], (DEPTH, N_Q_HEADS), f32, 1.0, 6.0),
        'g_q': 1.0 + nrm(ks[12], (DEPTH, HEAD_DIM), 0.02),
        'g_k': 1.0 + nrm(ks[13], (DEPTH, HEAD_DIM), 0.02),
        'w_out': nrm(ks[14], (DEPTH, W_Q, D_MODEL), W_Q ** -0.5),
        'g_ffn': 1.0 + nrm(ks[15], (DEPTH, D_MODEL), 0.02),
        'w_up': nrm(ks[16], (DEPTH, D_MODEL, D_FF), D_MODEL ** -0.5),
        'w_down': nrm(ks[17], (DEPTH, D_FF, D_MODEL), D_FF ** -0.5),
    }


def reference(x_prompt, x_sample, cache_sb_k, cache_sb_v, cache_fox_k, cache_fox_v, cache_fox_logf,
              page_table, meta_tokens, g_mix, w_in, b_forget, g_q, g_k, w_out, g_ffn, w_up, w_down):
    B = x_prompt.shape[0]
    meta = jnp.broadcast_to(meta_tokens[None].astype(x_prompt.dtype), (B, N_META, D_MODEL))
    xp = jnp.concatenate([meta, x_prompt], axis=1)
    xs = x_sample
    p_new = []
    s_new = []
    for layer in range(DEPTH):
        wts = (g_mix[layer], w_in[layer], b_forget[layer], g_q[layer], g_k[layer],
               w_out[layer], g_ffn[layer], w_up[layer], w_down[layer])
        xp, p_state = _prompt_layer(xp, *wts)
        xs, s_state = _sample_layer(xs, cache_sb_k[layer], cache_sb_v[layer], cache_fox_k[layer],
                                    cache_fox_v[layer], cache_fox_logf[layer], page_table, *wts)
        p_new.append(p_state)
        s_new.append(s_state)
    y_prompt = xp[:, N_META:]
    return (y_prompt, xs,
            _stack(p_new, 0), _stack(p_new, 1), _stack(p_new, 2), _stack(p_new, 3), _stack(p_new, 4),
            _stack(s_new, 0), _stack(s_new, 1), _stack(s_new, 2), _stack(s_new, 3), _stack(s_new, 4))
```

```python
import functools

import jax
import jax.numpy as jnp
from jax import lax
from jax.experimental import pallas as pl
from jax.experimental.pallas import tpu as pltpu

D_MODEL = 2048
HEAD_DIM = 128
N_Q_HEADS = 16
N_KV_HEADS = 4
GROUP = 4
W_Q = N_Q_HEADS * HEAD_DIM
W_KV = N_KV_HEADS * HEAD_DIM
D_FF = 4 * D_MODEL
N_META = 16
PAGE_SIZE = 128
EPS = 1e-6
SCALE = HEAD_DIM ** -0.5
FORGET_OFFSET = 2 * W_Q + 4 * W_KV
AUX_ROWS = 32

BF16 = jnp.bfloat16
F32 = jnp.float32

LANES = 128
NEG_BIG = -1e30
SB_DEAD = 104.0
VMEM_LIMIT = 48 * 1024 * 1024

_NT = (((1,), (1,)), ((), ()))


def _params(*sem):
    return pltpu.CompilerParams(dimension_semantics=sem, vmem_limit_bytes=VMEM_LIMIT)


def _rms(x, g):
    ms = jnp.mean(x * x, axis=-1, keepdims=True)
    return x * lax.rsqrt(ms + EPS) * g


def _softplus(z):
    return jnp.maximum(z, 0.0) + jnp.log1p(jnp.exp(-jnp.abs(z)))


def _split2(x):
    hi = x.astype(BF16)
    lo = (x - hi.astype(F32)).astype(BF16)
    return hi, lo


def _split3(x):
    hi = x.astype(BF16)
    r = x - hi.astype(F32)
    mid = r.astype(BF16)
    lo = (r - mid.astype(F32)).astype(BF16)
    return hi, mid, lo


def _dot(a, b):
    return jnp.dot(a, b, preferred_element_type=F32)


def _rms_kernel(x_ref, g_ref, h_ref):
    h_ref[...] = _rms(x_ref[...], g_ref[...]).astype(h_ref.dtype)


def _rms_bf16(x, g, tm):
    rows = x.shape[0]
    return pl.pallas_call(
        _rms_kernel,
        out_shape=jax.ShapeDtypeStruct((rows, D_MODEL), BF16),
        grid=(rows // tm,),
        in_specs=[pl.BlockSpec((tm, D_MODEL), lambda i: (i, 0)),
                  pl.BlockSpec((1, D_MODEL), lambda i: (0, 0))],
        out_specs=pl.BlockSpec((tm, D_MODEL), lambda i: (i, 0)),
        compiler_params=_params("parallel"),
        name="rms_mix",
    )(x, g)


def _head_norm(acc, g):
    outs = []
    for hh in range(acc.shape[1] // HEAD_DIM):
        xh = acc[:, hh * HEAD_DIM:(hh + 1) * HEAD_DIM]
        outs.append(_rms(xh, g))
    return jnp.concatenate(outs, axis=1)


def _proj_kernel(h_ref, w_ref, g_ref, o_ref, *, mode):
    acc = _dot(h_ref[...], w_ref[...])
    if mode == "scale":
        acc = acc * SCALE
    elif mode == "norm_scale":
        acc = _head_norm(acc, g_ref[...]) * SCALE
    elif mode == "sigmoid":
        acc = jax.nn.sigmoid(acc)
    o_ref[...] = acc.astype(o_ref.dtype)


def _proj(h, w, g, mode, tm, tn, name):
    rows, n = h.shape[0], w.shape[1]
    return pl.pallas_call(
        functools.partial(_proj_kernel, mode=mode),
        out_shape=jax.ShapeDtypeStruct((rows, n), BF16),
        grid=(rows // tm, n // tn),
        in_specs=[pl.BlockSpec((tm, D_MODEL), lambda i, j: (i, 0)),
                  pl.BlockSpec((D_MODEL, tn), lambda i, j: (0, j)),
                  pl.BlockSpec((1, HEAD_DIM), lambda i, j: (0, 0))],
        out_specs=pl.BlockSpec((tm, tn), lambda i, j: (i, j)),
        compiler_params=_params("parallel", "arbitrary"),
        name=name,
    )(h, w, g)


def _kv_kernel(h_ref, w_ref, gk_ref, *out_refs):
    j = pl.program_id(1)
    acc = _dot(h_ref[...], w_ref[...])
    for jj in range(4):
        @pl.when(j == jj)
        def _(jj=jj):
            val = _head_norm(acc, gk_ref[...]) if jj == 2 else acc
            out_refs[jj][...] = val
            out_refs[4 + jj][...] = val.astype(BF16)


def _kv_proj(h, w_kv, g_k, tm):
    rows = h.shape[0]
    blk = pl.BlockSpec((tm, W_KV), lambda i, j: (i, 0))
    return pl.pallas_call(
        _kv_kernel,
        out_shape=[jax.ShapeDtypeStruct((rows, W_KV), F32)] * 4
        + [jax.ShapeDtypeStruct((rows, W_KV), BF16)] * 4,
        grid=(rows // tm, 4),
        in_specs=[pl.BlockSpec((tm, D_MODEL), lambda i, j: (i, 0)),
                  pl.BlockSpec((D_MODEL, W_KV), lambda i, j: (0, j)),
                  pl.BlockSpec((1, HEAD_DIM), lambda i, j: (0, 0))],
        out_specs=[blk] * 8,
        compiler_params=_params("parallel", "arbitrary"),
        name="kv_proj",
    )(h, w_kv, g_k)


def _forget_kernel(x_ref, g_ref, whi_ref, wlo_ref, b_ref, lf_ref):
    h = _rms(x_ref[...], g_ref[...])
    hh, hl = _split2(h)
    z = _dot(hh, whi_ref[...]) + _dot(hl, whi_ref[...]) + _dot(hh, wlo_ref[...])
    z = z[:, :N_Q_HEADS] + b_ref[...]
    lf_ref[...] = -_softplus(-z)


def _forget(x, g, w_hi, w_lo, b, tm):
    rows = x.shape[0]
    return pl.pallas_call(
        _forget_kernel,
        out_shape=jax.ShapeDtypeStruct((rows, N_Q_HEADS), F32),
        grid=(rows // tm,),
        in_specs=[pl.BlockSpec((tm, D_MODEL), lambda i: (i, 0)),
                  pl.BlockSpec((1, D_MODEL), lambda i: (0, 0)),
                  pl.BlockSpec((D_MODEL, LANES), lambda i: (0, 0)),
                  pl.BlockSpec((D_MODEL, LANES), lambda i: (0, 0)),
                  pl.BlockSpec((1, N_Q_HEADS), lambda i: (0, 0))],
        out_specs=pl.BlockSpec((tm, N_Q_HEADS), lambda i: (i, 0)),
        compiler_params=_params("parallel"),
        name="forget_gate",
    )(x, g, w_hi, w_lo, b)


def _lane_cumsum(x):
    n = x.shape[-1]
    lane = lax.broadcasted_iota(jnp.int32, x.shape, x.ndim - 1)
    k = 1
    while k < n:
        x = x + jnp.where(lane >= k, pltpu.roll(x, k, axis=x.ndim - 1), 0.0)
        k *= 2
    return x


def _cum_kernel(meta_ref, main_ref, cmeta_ref, cmain_ref):
    cm = _lane_cumsum(meta_ref[...])
    cmeta_ref[...] = cm
    cmain_ref[...] = _lane_cumsum(main_ref[...]) + cm[:, LANES - 1:LANES]


def _cum_forget(lf_meta_t, lf_main_t):
    b, _, s = lf_main_t.shape
    return pl.pallas_call(
        _cum_kernel,
        out_shape=[jax.ShapeDtypeStruct((b, N_Q_HEADS, LANES), F32),
                   jax.ShapeDtypeStruct((b, N_Q_HEADS, s), F32)],
        grid=(b,),
        in_specs=[pl.BlockSpec((N_Q_HEADS, LANES), lambda i: (0, 0)),
                  pl.BlockSpec((None, N_Q_HEADS, s), lambda i: (i, 0, 0))],
        out_specs=[pl.BlockSpec((None, N_Q_HEADS, LANES), lambda i: (i, 0, 0)),
                   pl.BlockSpec((None, N_Q_HEADS, s), lambda i: (i, 0, 0))],
        compiler_params=_params("parallel"),
        name="cum_forget",
    )(lf_meta_t, lf_main_t)


def _stack_heads(q):
    return jnp.concatenate([q[:, g * HEAD_DIM:(g + 1) * HEAD_DIM] for g in range(GROUP)], axis=0)


def _sb_block(qs, kb, vb, uo, r_prev, mask):
    tk = kb.shape[0]
    z = lax.dot_general(qs, kb, _NT, preferred_element_type=F32)
    sp = _softplus(z)
    if mask is not None:
        sp = jnp.where(mask, sp, 0.0)
    hi, lo = _split2(sp)
    ct = _dot(hi, uo) + _dot(lo, uo)
    a = jnp.exp(z - ct[:, :tk] - r_prev)
    if mask is not None:
        a = jnp.where(mask, a, 0.0)
    return _dot(a.astype(BF16), vb), ct[:, tk:]


def _fox_block(qs, kb, vb, ckb, mask, m_ref, acc_ref):
    r, tk = qs.shape[0], kb.shape[0]
    t = r // GROUP
    z = lax.dot_general(qs, kb, _NT, preferred_element_type=F32)
    z = (z.reshape(GROUP, t, tk) - ckb[:, None, :]).reshape(r, tk)
    if mask is not None:
        z = jnp.where(mask, z, NEG_BIG)
    m_prev = m_ref[...]
    m_new = jnp.maximum(m_prev, jnp.max(z, axis=1, keepdims=True))
    alpha = jnp.exp(m_prev - m_new)
    p = jnp.exp(z - jnp.tile(m_new, (1, tk // LANES)))
    v1 = jnp.concatenate([vb, jnp.ones((tk, LANES), BF16)], axis=1)
    acc_ref[...] = acc_ref[...] * jnp.tile(alpha, (1, 2)) + _dot(p.astype(BF16), v1)
    m_ref[...] = m_new


def _tile_mask(t, tk, rel):
    row = lax.broadcasted_iota(jnp.int32, (GROUP, t, tk), 1).reshape(GROUP * t, tk)
    col = lax.broadcasted_iota(jnp.int32, (GROUP * t, tk), 1)
    return rel(col, row)


SB_T = 128


def _sb_prompt_kernel(q_ref, k_ref, v_ref, mk_ref, mv_ref, uo_ref, o_ref, r_ref, acc_ref):
    i = pl.program_id(2)
    t = SB_T
    qs = _stack_heads(q_ref[...])
    uo = uo_ref[...]

    def visit(kb, vb, mask):
        o, tot = _sb_block(qs, kb, vb, uo, r_ref[...], mask)
        acc_ref[...] += o
        r_ref[...] += tot
        return jnp.min(r_ref[...])

    r_ref[...] = jnp.zeros_like(r_ref)
    acc_ref[...] = jnp.zeros_like(acc_ref)
    start = pl.multiple_of(i * t, t)
    mn = visit(k_ref[pl.ds(start, t), :], v_ref[pl.ds(start, t), :],
               _tile_mask(t, t, lambda c, r: c < r))

    def cond(c):
        return jnp.logical_and(c[0] >= 0, c[1] < SB_DEAD)

    def body(c):
        s = pl.multiple_of(c[0] * t, t)
        return c[0] - 1, visit(k_ref[pl.ds(s, t), :], v_ref[pl.ds(s, t), :], None)

    _, mn = lax.while_loop(cond, body, (i - 1, mn))

    @pl.when(mn < SB_DEAD)
    def _():
        visit(mk_ref[...], mv_ref[...], _tile_mask(t, LANES, lambda c, r: c < N_META))

    acc = acc_ref[...]
    for g in range(GROUP):
        o_ref[:, g * HEAD_DIM:(g + 1) * HEAD_DIM] = acc[g * t:(g + 1) * t].astype(o_ref.dtype)


def _sb_prompt(q, k, v, mk, mv, uo, batch, seq):
    t = SB_T
    nb = seq // t
    return pl.pallas_call(
        _sb_prompt_kernel,
        out_shape=jax.ShapeDtypeStruct((batch * seq, W_Q), BF16),
        grid=(batch, N_KV_HEADS, nb),
        in_specs=[pl.BlockSpec((t, GROUP * HEAD_DIM), lambda b, n, i: (b * nb + i, n)),
                  pl.BlockSpec((seq, HEAD_DIM), lambda b, n, i: (b, n)),
                  pl.BlockSpec((seq, HEAD_DIM), lambda b, n, i: (b, n)),
                  pl.BlockSpec((LANES, HEAD_DIM), lambda b, n, i: (0, n)),
                  pl.BlockSpec((LANES, HEAD_DIM), lambda b, n, i: (0, n)),
                  pl.BlockSpec((t, 2 * t), lambda b, n, i: (0, 0))],
        out_specs=pl.BlockSpec((t, GROUP * HEAD_DIM), lambda b, n, i: (b * nb + i, n)),
        scratch_shapes=[pltpu.VMEM((GROUP * t, t), F32), pltpu.VMEM((GROUP * t, HEAD_DIM), F32)],
        compiler_params=_params("parallel", "parallel", "arbitrary"),
        name="sb_prompt",
    )(q, k, v, mk, mv, uo)


FOX_T = 256


def _fox_prompt_kernel(q_ref, k_ref, v_ref, ck_ref, mk_ref, mv_ref, mck_ref, o_ref, m_ref, acc_ref):
    i = pl.program_id(2)
    t = FOX_T
    qs = _stack_heads(q_ref[...])
    m_ref[...] = jnp.full_like(m_ref, NEG_BIG)
    acc_ref[...] = jnp.zeros_like(acc_ref)

    def visit(s, mask):
        _fox_block(qs, k_ref[pl.ds(s, t), :], v_ref[pl.ds(s, t), :], ck_ref[:, pl.ds(s, t)],
                   mask, m_ref, acc_ref)

    visit(pl.multiple_of(i * t, t), _tile_mask(t, t, lambda c, r: c <= r))

    def body(j, carry):
        visit(pl.multiple_of(j * t, t), None)
        return carry

    lax.fori_loop(0, i, body, 0)
    _fox_block(qs, mk_ref[...], mv_ref[...], mck_ref[...],
               _tile_mask(t, LANES, lambda c, r: c < N_META), m_ref, acc_ref)

    acc = acc_ref[...]
    o = acc[:, :HEAD_DIM] / acc[:, HEAD_DIM:]
    for g in range(GROUP):
        o_ref[:, g * HEAD_DIM:(g + 1) * HEAD_DIM] = o[g * t:(g + 1) * t].astype(o_ref.dtype)


def _fox_prompt(q, k, v, ck, mk, mv, mck, batch, seq):
    t = FOX_T
    nb = seq // t
    return pl.pallas_call(
        _fox_prompt_kernel,
        out_shape=jax.ShapeDtypeStruct((batch * seq, W_Q), BF16),
        grid=(batch, N_KV_HEADS, nb),
        in_specs=[pl.BlockSpec((t, GROUP * HEAD_DIM), lambda b, n, i: (b * nb + i, n)),
                  pl.BlockSpec((seq, HEAD_DIM), lambda b, n, i: (b, n)),
                  pl.BlockSpec((seq, HEAD_DIM), lambda b, n, i: (b, n)),
                  pl.BlockSpec((None, None, GROUP, seq), lambda b, n, i: (b, n, 0, 0)),
                  pl.BlockSpec((LANES, HEAD_DIM), lambda b, n, i: (0, n)),
                  pl.BlockSpec((LANES, HEAD_DIM), lambda b, n, i: (0, n)),
                  pl.BlockSpec((None, None, GROUP, LANES), lambda b, n, i: (b, n, 0, 0))],
        out_specs=pl.BlockSpec((t, GROUP * HEAD_DIM), lambda b, n, i: (b * nb + i, n)),
        scratch_shapes=[pltpu.VMEM((GROUP * t, LANES), F32), pltpu.VMEM((GROUP * t, 2 * LANES), F32)],
        compiler_params=_params("parallel", "parallel", "arbitrary"),
        name="fox_prompt",
    )(q, k, v, ck, mk, mv, mck)


def _meta_attn_kernel(sq_ref, sk_ref, sv_ref, fq_ref, fk_ref, fv_ref, ck_ref, uo_ref,
                      osb_ref, ofx_ref, m_ref, acc_ref):
    t = N_META
    valid = lambda c: c < N_META
    o, _ = _sb_block(_stack_heads(sq_ref[...]), sk_ref[...], sv_ref[...], uo_ref[...], 0.0,
                     _tile_mask(t, LANES, lambda c, r: jnp.logical_and(c < r, valid(c))))
    m_ref[...] = jnp.full_like(m_ref, NEG_BIG)
    acc_ref[...] = jnp.zeros_like(acc_ref)
    _fox_block(_stack_heads(fq_ref[...]), fk_ref[...], fv_ref[...], ck_ref[...],
               _tile_mask(t, LANES, lambda c, r: jnp.logical_and(c <= r, valid(c))), m_ref, acc_ref)
    acc = acc_ref[...]
    f = acc[:, :HEAD_DIM] / acc[:, HEAD_DIM:]
    for g in range(GROUP):
        osb_ref[:, g * HEAD_DIM:(g + 1) * HEAD_DIM] = o[g * t:(g + 1) * t].astype(osb_ref.dtype)
        ofx_ref[:, g * HEAD_DIM:(g + 1) * HEAD_DIM] = f[g * t:(g + 1) * t].astype(ofx_ref.dtype)


def _meta_attn(sq, sk, sv, fq, fk, fv, ck, uo):
    qspec = pl.BlockSpec((N_META, GROUP * HEAD_DIM), lambda n: (0, n))
    kspec = pl.BlockSpec((LANES, HEAD_DIM), lambda n: (0, n))
    return pl.pallas_call(
        _meta_attn_kernel,
        out_shape=[jax.ShapeDtypeStruct((N_META, W_Q), BF16)] * 2,
        grid=(N_KV_HEADS,),
        in_specs=[qspec, kspec, kspec, qspec, kspec, kspec,
                  pl.BlockSpec((None, GROUP, LANES), lambda n: (n, 0, 0)),
                  pl.BlockSpec((LANES, 2 * LANES), lambda n: (0, 0))],
        out_specs=[qspec, qspec],
        scratch_shapes=[pltpu.VMEM((GROUP * N_META, LANES), F32),
                        pltpu.VMEM((GROUP * N_META, 2 * LANES), F32)],
        compiler_params=_params("parallel"),
        name="meta_attn",
    )(sq, sk, sv, fq, fk, fv, ck, uo)


def _group_rows(vals):
    grp = lax.broadcasted_iota(jnp.int32, vals[0].shape, 0) // GROUP
    out = vals[0]
    for n in range(1, N_KV_HEADS):
        out = jnp.where(grp == n, vals[n], out)
    return out


def _page_copies(pt_ref, b, p, slot, srcs, bufs, sem):
    pid = pt_ref[b, p]
    return [pltpu.make_async_copy(src.at[pid], buf.at[slot], sem.at[a, slot])
            for a, (src, buf) in enumerate(zip(srcs, bufs))]


def _sb_decode_kernel(pt_ref, q_ref, ck_hbm, cv_hbm, uo_ref, o_ref, kbuf, vbuf, sem, r_ref, acc_ref):
    b = pl.program_id(0)
    n_pages = pt_ref.shape[1]
    q = q_ref[...]
    uo = uo_ref[...]
    copies = functools.partial(_page_copies, pt_ref, b, srcs=(ck_hbm, cv_hbm), bufs=(kbuf, vbuf), sem=sem)

    r_ref[...] = jnp.zeros_like(r_ref)
    acc_ref[...] = jnp.zeros_like(acc_ref)
    for c in copies(n_pages - 1, (n_pages - 1) % 2):
        c.start()

    def cond(c):
        return jnp.logical_and(c[0] >= 0, c[1] < SB_DEAD)

    def body(c):
        p = c[0]
        slot = p % 2
        for cp in copies(p, slot):
            cp.wait()

        @pl.when(p > 0)
        def _():
            for cp in copies(p - 1, 1 - slot):
                cp.start()

        kb = kbuf[slot].astype(BF16)
        vb = vbuf[slot].astype(BF16)
        z = _group_rows([lax.dot_general(q, kb[:, n * HEAD_DIM:(n + 1) * HEAD_DIM], _NT,
                                         preferred_element_type=F32) for n in range(N_KV_HEADS)])
        sp = _softplus(z)
        hi, lo = _split2(sp)
        ct = _dot(hi, uo) + _dot(lo, uo)
        a = jnp.exp(z - ct[:, :PAGE_SIZE] - r_ref[...]).astype(BF16)
        acc_ref[...] += _group_rows([_dot(a, vb[:, n * HEAD_DIM:(n + 1) * HEAD_DIM])
                                     for n in range(N_KV_HEADS)])
        r_new = r_ref[...] + ct[:, PAGE_SIZE:]
        r_ref[...] = r_new
        return p - 1, jnp.min(r_new)

    p_end, _ = lax.while_loop(cond, body, (jnp.int32(n_pages - 1), jnp.float32(0.0)))

    @pl.when(p_end >= 0)
    def _():
        for cp in copies(p_end, p_end % 2):
            cp.wait()

    o_ref[...] = acc_ref[...].astype(o_ref.dtype)


def _sb_decode(page_table, q, cache_k, cache_v, uo):
    nb = q.shape[0]
    any_spec = pl.BlockSpec(memory_space=pl.ANY)
    return pl.pallas_call(
        _sb_decode_kernel,
        out_shape=jax.ShapeDtypeStruct((nb, N_Q_HEADS, HEAD_DIM), BF16),
        grid_spec=pltpu.PrefetchScalarGridSpec(
            num_scalar_prefetch=1,
            grid=(nb,),
            in_specs=[pl.BlockSpec((None, N_Q_HEADS, HEAD_DIM), lambda b, pt: (b, 0, 0)),
                      any_spec, any_spec,
                      pl.BlockSpec((PAGE_SIZE, 2 * PAGE_SIZE), lambda b, pt: (0, 0))],
            out_specs=pl.BlockSpec((None, N_Q_HEADS, HEAD_DIM), lambda b, pt: (b, 0, 0)),
            scratch_shapes=[pltpu.VMEM((2, PAGE_SIZE, W_KV), F32),
                            pltpu.VMEM((2, PAGE_SIZE, W_KV), F32),
                            pltpu.SemaphoreType.DMA((2, 2)),
                            pltpu.VMEM((N_Q_HEADS, PAGE_SIZE), F32),
                            pltpu.VMEM((N_Q_HEADS, HEAD_DIM), F32)]),
        compiler_params=_params("arbitrary"),
        name="sb_decode",
    )(page_table, q, cache_k, cache_v, uo)


def _fox_decode_kernel(pt_ref, q_ref, kn_ref, vn_ref, lfn_ref, ck_hbm, cv_hbm, lf_hbm, us_ref, o_ref,
                       kbuf, vbuf, lbuf, sem, m_ref, acc_ref, d_ref):
    b = pl.program_id(0)
    n_pages = pt_ref.shape[1]
    q = q_ref[...]
    us = us_ref[...]
    copies = functools.partial(_page_copies, pt_ref, b, srcs=(ck_hbm, cv_hbm, lf_hbm),
                               bufs=(kbuf, vbuf, lbuf), sem=sem)

    z_self = jnp.sum(q.astype(F32) * kn_ref[...].astype(F32), axis=1, keepdims=True)
    m_ref[...] = jnp.broadcast_to(z_self, m_ref.shape)
    acc_ref[...] = jnp.concatenate([vn_ref[...].astype(F32), jnp.ones((N_Q_HEADS, LANES), F32)], axis=1)
    d_ref[...] = jnp.broadcast_to(lfn_ref[...], d_ref.shape)

    for c in copies(n_pages - 1, (n_pages - 1) % 2):
        c.start()

    def body(it, carry):
        p = n_pages - 1 - it
        slot = p % 2
        for cp in copies(p, slot):
            cp.wait()

        @pl.when(p > 0)
        def _():
            for cp in copies(p - 1, 1 - slot):
                cp.start()

        kb = kbuf[slot].astype(BF16)
        vb = vbuf[slot].astype(BF16)
        l1, l2, l3 = _split3(lbuf[slot])
        dt = _dot(l1, us) + _dot(l2, us) + _dot(l3, us)
        z = _group_rows([lax.dot_general(q, kb[:, n * HEAD_DIM:(n + 1) * HEAD_DIM], _NT,
                                         preferred_element_type=F32) for n in range(N_KV_HEADS)])
        z = z + dt[:, :PAGE_SIZE] + d_ref[...]
        m_prev = m_ref[...]
        m_new = jnp.maximum(m_prev, jnp.max(z, axis=1, keepdims=True))
        alpha = jnp.exp(m_prev - m_new)
        pr = jnp.exp(z - m_new).astype(BF16)
        pv = _group_rows([_dot(pr, vb[:, n * HEAD_DIM:(n + 1) * HEAD_DIM]) for n in range(N_KV_HEADS)])
        lsum = jnp.sum(pr.astype(F32), axis=1, keepdims=True)
        acc_ref[...] = acc_ref[...] * jnp.tile(alpha, (1, 2)) + jnp.concatenate(
            [pv, jnp.broadcast_to(lsum, pv.shape)], axis=1)
        m_ref[...] = m_new
        d_ref[...] += dt[:, PAGE_SIZE:]
        return carry

    lax.fori_loop(0, n_pages, body, 0)
    acc = acc_ref[...]
    o_ref[...] = (acc[:, :HEAD_DIM] / acc[:, HEAD_DIM:]).astype(o_ref.dtype)


def _fox_decode(page_table, q, k_new, v_new, lf_new, cache_k, cache_v, cache_lf_t, us):
    nb = q.shape[0]
    any_spec = pl.BlockSpec(memory_space=pl.ANY)
    head_spec = pl.BlockSpec((None, N_Q_HEADS, HEAD_DIM), lambda b, pt: (b, 0, 0))
    return pl.pallas_call(
        _fox_decode_kernel,
        out_shape=jax.ShapeDtypeStruct((nb, N_Q_HEADS, HEAD_DIM), BF16),
        grid_spec=pltpu.PrefetchScalarGridSpec(
            num_scalar_prefetch=1,
            grid=(nb,),
            in_specs=[head_spec, head_spec, head_spec,
                      pl.BlockSpec((None, N_Q_HEADS, 1), lambda b, pt: (b, 0, 0)),
                      any_spec, any_spec, any_spec,
                      pl.BlockSpec((PAGE_SIZE, 2 * PAGE_SIZE), lambda b, pt: (0, 0))],
            out_specs=head_spec,
            scratch_shapes=[pltpu.VMEM((2, PAGE_SIZE, W_KV), F32),
                            pltpu.VMEM((2, PAGE_SIZE, W_KV), F32),
                            pltpu.VMEM((2, N_Q_HEADS, PAGE_SIZE), F32),
                            pltpu.SemaphoreType.DMA((3, 2)),
                            pltpu.VMEM((N_Q_HEADS, LANES), F32),
                            pltpu.VMEM((N_Q_HEADS, 2 * LANES), F32),
                            pltpu.VMEM((N_Q_HEADS, PAGE_SIZE), F32)]),
        compiler_params=_params("arbitrary"),
        name="fox_decode",
    )(page_table, q, k_new, v_new, lf_new, cache_k, cache_v, cache_lf_t, us)


def _out_proj_kernel(osb_ref, ofx_ref, gsb_ref, gfx_ref, w_ref, x_ref, y_ref, merged_ref):
    @pl.when(pl.program_id(1) == 0)
    def _():
        merged = (gsb_ref[...].astype(F32) * osb_ref[...].astype(F32)
                  + gfx_ref[...].astype(F32) * ofx_ref[...].astype(F32))
        merged_ref[...] = merged.astype(BF16)

    y_ref[...] = x_ref[...] + _dot(merged_ref[...], w_ref[...])


def _out_proj(o_sb, o_fx, gates, w_out, x, tm, tn):
    rows = x.shape[0]
    wide = lambda c: pl.BlockSpec((tm, W_Q), lambda i, j: (i, c))
    return pl.pallas_call(
        _out_proj_kernel,
        out_shape=jax.ShapeDtypeStruct((rows, D_MODEL), F32),
        grid=(rows // tm, D_MODEL // tn),
        in_specs=[wide(0), wide(0), wide(0), wide(1),
                  pl.BlockSpec((W_Q, tn), lambda i, j: (0, j)),
                  pl.BlockSpec((tm, tn), lambda i, j: (i, j))],
        out_specs=pl.BlockSpec((tm, tn), lambda i, j: (i, j)),
        scratch_shapes=[pltpu.VMEM((tm, W_Q), BF16)],
        compiler_params=_params("parallel", "arbitrary"),
        name="out_proj",
    )(o_sb, o_fx, gates, gates, w_out, x)


def _ffn_kernel(y_ref, g_ref, wu_ref, wd_ref, o_ref, h_ref):
    @pl.when(pl.program_id(1) == 0)
    def _():
        y = y_ref[...]
        h_ref[...] = _rms(y, g_ref[...]).astype(BF16)
        o_ref[...] = y

    u = jnp.square(jnp.maximum(_dot(h_ref[...], wu_ref[...]), 0.0))
    o_ref[...] += _dot(u.astype(BF16), wd_ref[...])


def _ffn(y, g, w_up, w_down, tm, tf):
    rows = y.shape[0]
    return pl.pallas_call(
        _ffn_kernel,
        out_shape=jax.ShapeDtypeStruct((rows, D_MODEL), F32),
        grid=(rows // tm, D_FF // tf),
        in_specs=[pl.BlockSpec((tm, D_MODEL), lambda i, k: (i, 0)),
                  pl.BlockSpec((1, D_MODEL), lambda i, k: (0, 0)),
                  pl.BlockSpec((D_MODEL, tf), lambda i, k: (0, k)),
                  pl.BlockSpec((tf, D_MODEL), lambda i, k: (k, 0))],
        out_specs=pl.BlockSpec((tm, D_MODEL), lambda i, k: (i, 0)),
        scratch_shapes=[pltpu.VMEM((tm, D_MODEL), BF16)],
        compiler_params=_params("parallel", "arbitrary"),
        name="ffn",
    )(y, g, w_up, w_down)


def _project_rows(x, wts, tm, tn):
    h = _rms_bf16(x, wts["g_mix"], tm)
    sq = _proj(h, wts["w_sq"], wts["g_q"], "scale", tm, tn, "proj_sq")
    fq = _proj(h, wts["w_fq"], wts["g_q"], "norm_scale", tm, tn, "proj_fq")
    gates = _proj(h, wts["w_gate"], wts["g_q"], "sigmoid", tm, tn, "proj_gates")
    kv = _kv_proj(h, wts["w_kv"], wts["g_k"], tm)
    lf = _forget(x, wts["g_mix"], wts["w_fl_hi"], wts["w_fl_lo"], wts["b_f"], tm)
    return sq, fq, gates, kv, lf


def _finish_rows(x, o_sb, o_fx, gates, wts, tm, tn):
    y1 = _out_proj(o_sb, o_fx, gates, wts["w_out"], x, tm, tn)
    return _ffn(y1, wts["g_ffn"], wts["w_up"], wts["w_down"], tm, tn)


def kernel(x_prompt, x_sample, cache_sb_k, cache_sb_v, cache_fox_k, cache_fox_v, cache_fox_logf,
           page_table, meta_tokens, g_mix, w_in, b_forget, g_q, g_k, w_out, g_ffn, w_up, w_down):
    batch, seq, _ = x_prompt.shape
    dec = x_sample.shape[0]
    assert w_in.shape[0] == 1 and x_sample.shape[1] == 1 and N_META + dec <= AUX_ROWS
    pool = cache_sb_k.shape[1]

    w = w_in[0]
    w_fl = jnp.pad(w[:, FORGET_OFFSET:FORGET_OFFSET + N_Q_HEADS], ((0, 0), (0, LANES - N_Q_HEADS)))
    w_fl_hi = w_fl.astype(BF16)
    wts = {
        "g_mix": g_mix, "g_q": g_q, "g_k": g_k, "g_ffn": g_ffn, "b_f": b_forget,
        "w_sq": w[:, :W_Q].astype(BF16),
        "w_fq": w[:, W_Q + 2 * W_KV:2 * W_Q + 2 * W_KV].astype(BF16),
        "w_kv": jnp.concatenate([w[:, W_Q:W_Q + 2 * W_KV], w[:, 2 * W_Q + 2 * W_KV:FORGET_OFFSET]],
                                axis=1).astype(BF16),
        "w_gate": w[:, FORGET_OFFSET + N_Q_HEADS:].astype(BF16),
        "w_fl_hi": w_fl_hi, "w_fl_lo": (w_fl - w_fl_hi.astype(F32)).astype(BF16),
        "w_out": w_out[0].astype(BF16), "w_up": w_up[0].astype(BF16), "w_down": w_down[0].astype(BF16),
    }

    x_main = x_prompt.reshape(batch * seq, D_MODEL)
    x_aux = jnp.concatenate([meta_tokens, x_sample.reshape(dec, D_MODEL),
                             jnp.zeros((AUX_ROWS - N_META - dec, D_MODEL), F32)], axis=0)

    sq_m, fq_m, gates_m, kv_m, lf_m = _project_rows(x_main, wts, 512, 512)
    sq_a, fq_a, gates_a, kv_a, lf_a = _project_rows(x_aux, wts, AUX_ROWS, 512)
    sk_m, sv_m, fk_m, fv_m, skb_m, svb_m, fkb_m, fvb_m = kv_m
    sk_a, sv_a, fk_a, fv_a, skb_a, svb_a, fkb_a, fvb_a = kv_a

    lf_meta_t = jnp.pad(lf_a[:N_META].T, ((0, 0), (0, LANES - N_META)))
    lf_main_t = jnp.swapaxes(lf_m.reshape(batch, seq, N_Q_HEADS), 1, 2)
    ck_meta, ck_main = _cum_forget(lf_meta_t, lf_main_t)
    ck_meta = ck_meta.reshape(batch, N_KV_HEADS, GROUP, LANES)
    ck_main = ck_main.reshape(batch, N_KV_HEADS, GROUP, seq)

    pad_keys = lambda a: jnp.pad(a[:N_META], ((0, LANES - N_META), (0, 0)))
    mk_sb, mv_sb, mk_fx, mv_fx = pad_keys(skb_a), pad_keys(svb_a), pad_keys(fkb_a), pad_keys(fvb_a)

    j_idx = lax.broadcasted_iota(jnp.int32, (LANES, LANES), 0)
    s_idx = lax.broadcasted_iota(jnp.int32, (LANES, LANES), 1)
    ones = jnp.ones((LANES, LANES), BF16)
    uo = jnp.concatenate([(j_idx >= s_idx).astype(BF16), ones], axis=1)
    us = jnp.concatenate([(j_idx > s_idx).astype(BF16), ones], axis=1)

    o_sb_m = _sb_prompt(sq_m, skb_m, svb_m, mk_sb, mv_sb, uo, batch, seq)
    o_fx_m = _fox_prompt(fq_m, fkb_m, fvb_m, ck_main, mk_fx, mv_fx, ck_meta, batch, seq)
    o_sb_meta, o_fx_meta = _meta_attn(sq_a[:N_META], mk_sb, mv_sb, fq_a[:N_META], mk_fx, mv_fx,
                                      ck_meta[0], uo)

    heads = lambda a: a[N_META:N_META + dec].reshape(dec, N_Q_HEADS, HEAD_DIM)
    kv_heads = lambda a: jnp.repeat(a[N_META:N_META + dec].reshape(dec, N_KV_HEADS, HEAD_DIM), GROUP, axis=1)
    o_sb_dec = _sb_decode(page_table, heads(sq_a), cache_sb_k[0].reshape(pool, PAGE_SIZE, W_KV),
                          cache_sb_v[0].reshape(pool, PAGE_SIZE, W_KV), uo)
    o_fx_dec = _fox_decode(page_table, heads(fq_a), kv_heads(fkb_a), kv_heads(fvb_a),
                           lf_a[N_META:N_META + dec].reshape(dec, N_Q_HEADS, 1),
                           cache_fox_k[0].reshape(pool, PAGE_SIZE, W_KV),
                           cache_fox_v[0].reshape(pool, PAGE_SIZE, W_KV),
                           jnp.swapaxes(cache_fox_logf[0], 1, 2), us)

    tail = jnp.zeros((AUX_ROWS - N_META - dec, W_Q), BF16)
    o_sb_a = jnp.concatenate([o_sb_meta, o_sb_dec.reshape(dec, W_Q), tail], axis=0)
    o_fx_a = jnp.concatenate([o_fx_meta, o_fx_dec.reshape(dec, W_Q), tail], axis=0)

    y_main = _finish_rows(x_main, o_sb_m, o_fx_m, gates_m, wts, 512, 512)
    y_aux = _finish_rows(x_aux, o_sb_a, o_fx_a, gates_a, wts, AUX_ROWS, 512)

    def prompt_state(a_aux, a_main, tail_shape):
        meta = jnp.broadcast_to(a_aux[None, :N_META], (batch, N_META, a_aux.shape[1]))
        full = jnp.concatenate([meta, a_main.reshape(batch, seq, a_aux.shape[1])], axis=1)
        return full.reshape((1, batch, seq + N_META) + tail_shape)

    kv_shape = (N_KV_HEADS, HEAD_DIM)
    sample_state = lambda a, tail_shape: a[N_META:N_META + dec].reshape((1, dec, 1) + tail_shape)
    return (y_main.reshape(batch, seq, D_MODEL), y_aux[N_META:N_META + dec].reshape(dec, 1, D_MODEL),
            prompt_state(sk_a, sk_m, kv_shape), prompt_state(sv_a, sv_m, kv_shape),
            prompt_state(fk_a, fk_m, kv_shape), prompt_state(fv_a, fv_m, kv_shape),
            prompt_state(lf_a, lf_m, (N_Q_HEADS,)),
            sample_state(sk_a, kv_shape), sample_state(sv_a, kv_shape),
            sample_state(fk_a, kv_shape), sample_state(fv_a, kv_shape),
            sample_state(lf_a, (N_Q_HEADS,)))
```

```python
import functools

import jax
import jax.numpy as jnp
from jax import lax
from jax.experimental import pallas as pl
from jax.experimental.pallas import tpu as pltpu

D_MODEL = 2048
HEAD_DIM = 128
N_Q_HEADS = 16
N_KV_HEADS = 4
GROUP = 4
W_Q = N_Q_HEADS * HEAD_DIM
W_KV = N_KV_HEADS * HEAD_DIM
D_FF = 4 * D_MODEL
N_META = 16
PAGE_SIZE = 128
EPS = 1e-6
LOG2E = 1.4426950408889634
Q_SCALE = HEAD_DIM ** -0.5 * LOG2E
FORGET_OFFSET = 2 * W_Q + 4 * W_KV
AUX_ROWS = 32

BF16 = jnp.bfloat16
F32 = jnp.float32

LANES = 128
NEG_BIG = -1e30
SB_DEAD = 104.0 * LOG2E
VMEM_LIMIT = 48 * 1024 * 1024

_NT = (((1,), (1,)), ((), ()))


def _params(*sem):
    return pltpu.CompilerParams(dimension_semantics=sem, vmem_limit_bytes=VMEM_LIMIT)


def _rms(x, g):
    ms = jnp.mean(x * x, axis=-1, keepdims=True)
    return x * lax.rsqrt(ms + EPS) * g


def _softplus(z):
    return jnp.maximum(z, 0.0) + jnp.log1p(jnp.exp(-jnp.abs(z)))


def _softplus2(z2):
    return jnp.maximum(z2, 0.0) + jnp.log2(1.0 + jnp.exp2(-jnp.abs(z2)))


def _split2(x):
    hi = x.astype(BF16)
    lo = (x - hi.astype(F32)).astype(BF16)
    return hi, lo


def _split3(x):
    hi = x.astype(BF16)
    r = x - hi.astype(F32)
    mid = r.astype(BF16)
    lo = (r - mid.astype(F32)).astype(BF16)
    return hi, mid, lo


def _dot(a, b):
    return jnp.dot(a, b, preferred_element_type=F32)


def _split_dot(parts, m_stacked):
    return _dot(jnp.concatenate(parts, axis=1), m_stacked)


def _rms_kernel(x_ref, g_ref, h_ref):
    h_ref[...] = _rms(x_ref[...], g_ref[...]).astype(h_ref.dtype)


def _rms_bf16(x, g, tm):
    rows = x.shape[0]
    return pl.pallas_call(
        _rms_kernel,
        out_shape=jax.ShapeDtypeStruct((rows, D_MODEL), BF16),
        grid=(rows // tm,),
        in_specs=[pl.BlockSpec((tm, D_MODEL), lambda i: (i, 0)),
                  pl.BlockSpec((1, D_MODEL), lambda i: (0, 0))],
        out_specs=pl.BlockSpec((tm, D_MODEL), lambda i: (i, 0)),
        compiler_params=_params("parallel"),
        name="rms_mix",
    )(x, g)


def _head_norm(acc, g):
    outs = []
    for hh in range(acc.shape[1] // HEAD_DIM):
        xh = acc[:, hh * HEAD_DIM:(hh + 1) * HEAD_DIM]
        outs.append(_rms(xh, g))
    return jnp.concatenate(outs, axis=1)


def _proj_kernel(h_ref, w_ref, g_ref, o_ref, *, mode):
    acc = _dot(h_ref[...], w_ref[...])
    if mode == "scale":
        acc = acc * Q_SCALE
    elif mode == "norm_scale":
        acc = _head_norm(acc, g_ref[...]) * Q_SCALE
    elif mode == "sigmoid":
        acc = jax.nn.sigmoid(acc)
    o_ref[...] = acc.astype(o_ref.dtype)


def _proj(h, w, g, mode, tm, tn, name):
    rows, n = h.shape[0], w.shape[1]
    return pl.pallas_call(
        functools.partial(_proj_kernel, mode=mode),
        out_shape=jax.ShapeDtypeStruct((rows, n), BF16),
        grid=(rows // tm, n // tn),
        in_specs=[pl.BlockSpec((tm, D_MODEL), lambda i, j: (i, 0)),
                  pl.BlockSpec((D_MODEL, tn), lambda i, j: (0, j)),
                  pl.BlockSpec((1, HEAD_DIM), lambda i, j: (0, 0))],
        out_specs=pl.BlockSpec((tm, tn), lambda i, j: (i, j)),
        compiler_params=_params("parallel", "arbitrary"),
        name=name,
    )(h, w, g)


def _kv_kernel(h_ref, w_ref, gk_ref, *out_refs):
    j = pl.program_id(1)
    acc = _dot(h_ref[...], w_ref[...])
    for jj in range(4):
        @pl.when(j == jj)
        def _(jj=jj):
            val = _head_norm(acc, gk_ref[...]) if jj == 2 else acc
            out_refs[jj][...] = val
            out_refs[4 + jj][...] = val.astype(BF16)


def _kv_proj(h, w_kv, g_k, tm):
    rows = h.shape[0]
    blk = pl.BlockSpec((tm, W_KV), lambda i, j: (i, 0))
    return pl.pallas_call(
        _kv_kernel,
        out_shape=[jax.ShapeDtypeStruct((rows, W_KV), F32)] * 4
        + [jax.ShapeDtypeStruct((rows, W_KV), BF16)] * 4,
        grid=(rows // tm, 4),
        in_specs=[pl.BlockSpec((tm, D_MODEL), lambda i, j: (i, 0)),
                  pl.BlockSpec((D_MODEL, W_KV), lambda i, j: (0, j)),
                  pl.BlockSpec((1, HEAD_DIM), lambda i, j: (0, 0))],
        out_specs=[blk] * 8,
        compiler_params=_params("parallel", "arbitrary"),
        name="kv_proj",
    )(h, w_kv, g_k)


def _forget_kernel(x_ref, g_ref, whi_ref, wlo_ref, b_ref, lf_ref):
    h = _rms(x_ref[...], g_ref[...])
    hh, hl = _split2(h)
    z = _dot(hh, whi_ref[...]) + _dot(hl, whi_ref[...]) + _dot(hh, wlo_ref[...])
    z = z[:, :N_Q_HEADS] + b_ref[...]
    lf_ref[...] = -_softplus(-z)


def _forget(x, g, w_hi, w_lo, b, tm):
    rows = x.shape[0]
    return pl.pallas_call(
        _forget_kernel,
        out_shape=jax.ShapeDtypeStruct((rows, N_Q_HEADS), F32),
        grid=(rows // tm,),
        in_specs=[pl.BlockSpec((tm, D_MODEL), lambda i: (i, 0)),
                  pl.BlockSpec((1, D_MODEL), lambda i: (0, 0)),
                  pl.BlockSpec((D_MODEL, LANES), lambda i: (0, 0)),
                  pl.BlockSpec((D_MODEL, LANES), lambda i: (0, 0)),
                  pl.BlockSpec((1, N_Q_HEADS), lambda i: (0, 0))],
        out_specs=pl.BlockSpec((tm, N_Q_HEADS), lambda i: (i, 0)),
        compiler_params=_params("parallel"),
        name="forget_gate",
    )(x, g, w_hi, w_lo, b)


def _lane_cumsum(x):
    n = x.shape[-1]
    lane = lax.broadcasted_iota(jnp.int32, x.shape, x.ndim - 1)
    k = 1
    while k < n:
        x = x + jnp.where(lane >= k, pltpu.roll(x, k, axis=x.ndim - 1), 0.0)
        k *= 2
    return x


def _cum_kernel(meta_ref, main_ref, cmeta_ref, cmain_ref):
    cm = _lane_cumsum(meta_ref[...])
    cmeta_ref[...] = cm * LOG2E
    cmain_ref[...] = (_lane_cumsum(main_ref[...]) + cm[:, LANES - 1:LANES]) * LOG2E


def _cum_forget(lf_meta_t, lf_main_t):
    b, _, s = lf_main_t.shape
    return pl.pallas_call(
        _cum_kernel,
        out_shape=[jax.ShapeDtypeStruct((b, N_Q_HEADS, LANES), F32),
                   jax.ShapeDtypeStruct((b, N_Q_HEADS, s), F32)],
        grid=(b,),
        in_specs=[pl.BlockSpec((N_Q_HEADS, LANES), lambda i: (0, 0)),
                  pl.BlockSpec((None, N_Q_HEADS, s), lambda i: (i, 0, 0))],
        out_specs=[pl.BlockSpec((None, N_Q_HEADS, LANES), lambda i: (i, 0, 0)),
                   pl.BlockSpec((None, N_Q_HEADS, s), lambda i: (i, 0, 0))],
        compiler_params=_params("parallel"),
        name="cum_forget",
    )(lf_meta_t, lf_main_t)


def _stack_heads(q):
    return jnp.concatenate([q[:, g * HEAD_DIM:(g + 1) * HEAD_DIM] for g in range(GROUP)], axis=0)


SB_CHUNK = 512
FOX_CHUNK = 256


def _chunk_visible(rel, c, cr, t, tk):
    col = lax.broadcasted_iota(jnp.int32, (cr, tk), 1)
    row = (lax.broadcasted_iota(jnp.int32, (cr, tk), 0) + c * cr) % t
    return rel(col, row)


def _sb_block(qs, kb, vb, uo, rel, r_ref, acc_ref):
    r, tk = qs.shape[0], kb.shape[0]
    t = r // GROUP
    cr = min(r, SB_CHUNK)
    score = lambda c: lax.dot_general(qs[c * cr:(c + 1) * cr], kb, _NT, preferred_element_type=F32)
    z_next = score(0)
    for c in range(r // cr):
        rows = slice(c * cr, (c + 1) * cr)
        z = z_next
        if c + 1 < r // cr:
            z_next = score(c + 1)
        sp = _softplus2(z)
        if rel is not None:
            vis = _chunk_visible(rel, c, cr, t, tk)
            sp = jnp.where(vis, sp, 0.0)
        ct = _split_dot(_split2(sp), uo)
        r_prev = r_ref[rows]
        a = jnp.exp2(z - ct[:, :tk] - r_prev)
        if rel is not None:
            a = jnp.where(vis, a, 0.0)
        acc_ref[rows] += _dot(a.astype(BF16), vb)
        r_ref[rows] = r_prev + ct[:, tk:]


def _fox_block(qs, kb, vb, ckb, rel, m_ref, acc_ref):
    r, tk = qs.shape[0], kb.shape[0]
    t = r // GROUP
    cr = min(t, FOX_CHUNK)
    v1 = jnp.concatenate([vb, jnp.ones((tk, LANES), BF16)], axis=1)
    score = lambda c: lax.dot_general(qs[c * cr:(c + 1) * cr], kb, _NT, preferred_element_type=F32)
    z_next = score(0)
    for c in range(r // cr):
        rows = slice(c * cr, (c + 1) * cr)
        g = c * cr // t
        z = z_next - ckb[g:g + 1, :]
        if c + 1 < r // cr:
            z_next = score(c + 1)
        if rel is not None:
            z = jnp.where(_chunk_visible(rel, c, cr, t, tk), z, NEG_BIG)
        m_prev = m_ref[rows]
        m_new = jnp.maximum(m_prev, jnp.max(z, axis=1, keepdims=True))
        alpha = jnp.exp2(m_prev - m_new)
        p = jnp.exp2(z - jnp.tile(m_new, (1, tk // LANES)))
        acc_ref[rows] = acc_ref[rows] * jnp.tile(alpha, (1, 2)) + _dot(p.astype(BF16), v1)
        m_ref[rows] = m_new


SB_T = 128


def _sb_prompt_kernel(q_ref, k_ref, v_ref, mk_ref, mv_ref, uo_ref, o_ref, r_ref, acc_ref):
    i = pl.program_id(2)
    t = SB_T
    qs = _stack_heads(q_ref[...])
    uo = uo_ref[...]

    def visit(kb, vb, rel):
        _sb_block(qs, kb, vb, uo, rel, r_ref, acc_ref)
        return jnp.min(r_ref[...])

    r_ref[...] = jnp.zeros_like(r_ref)
    acc_ref[...] = jnp.zeros_like(acc_ref)
    start = pl.multiple_of(i * t, t)
    mn = visit(k_ref[pl.ds(start, t), :], v_ref[pl.ds(start, t), :], lambda c, r: c < r)

    def cond(c):
        return jnp.logical_and(c[0] >= 0, c[1] < SB_DEAD)

    def body(c):
        s = pl.multiple_of(c[0] * t, t)
        return c[0] - 1, visit(k_ref[pl.ds(s, t), :], v_ref[pl.ds(s, t), :], None)

    _, mn = lax.while_loop(cond, body, (i - 1, mn))

    @pl.when(mn < SB_DEAD)
    def _():
        visit(mk_ref[...], mv_ref[...], lambda c, r: c < N_META)

    acc = acc_ref[...]
    for g in range(GROUP):
        o_ref[:, g * HEAD_DIM:(g + 1) * HEAD_DIM] = acc[g * t:(g + 1) * t].astype(o_ref.dtype)


def _sb_prompt(q, k, v, mk, mv, uo, batch, seq):
    t = SB_T
    nb = seq // t
    return pl.pallas_call(
        _sb_prompt_kernel,
        out_shape=jax.ShapeDtypeStruct((batch * seq, W_Q), BF16),
        grid=(batch, N_KV_HEADS, nb),
        in_specs=[pl.BlockSpec((t, GROUP * HEAD_DIM), lambda b, n, i: (b * nb + i, n)),
                  pl.BlockSpec((seq, HEAD_DIM), lambda b, n, i: (b, n)),
                  pl.BlockSpec((seq, HEAD_DIM), lambda b, n, i: (b, n)),
                  pl.BlockSpec((LANES, HEAD_DIM), lambda b, n, i: (0, n)),
                  pl.BlockSpec((LANES, HEAD_DIM), lambda b, n, i: (0, n)),
                  pl.BlockSpec(uo.shape, lambda b, n, i: (0, 0))],
        out_specs=pl.BlockSpec((t, GROUP * HEAD_DIM), lambda b, n, i: (b * nb + i, n)),
        scratch_shapes=[pltpu.VMEM((GROUP * t, t), F32), pltpu.VMEM((GROUP * t, HEAD_DIM), F32)],
        compiler_params=_params("parallel", "parallel", "arbitrary"),
        name="sb_prompt",
    )(q, k, v, mk, mv, uo)


FOX_T = 256


def _fox_prompt_kernel(q_ref, k_ref, v_ref, ck_ref, mk_ref, mv_ref, mck_ref, o_ref, m_ref, acc_ref):
    i = pl.program_id(2)
    t = FOX_T
    qs = _stack_heads(q_ref[...])
    m_ref[...] = jnp.full_like(m_ref, NEG_BIG)
    acc_ref[...] = jnp.zeros_like(acc_ref)

    def visit(s, width, rel):
        _fox_block(qs, k_ref[pl.ds(s, width), :], v_ref[pl.ds(s, width), :], ck_ref[:, pl.ds(s, width)],
                   rel, m_ref, acc_ref)

    visit(pl.multiple_of(i * t, t), t, lambda c, r: c <= r)

    def body(j, carry):
        visit(pl.multiple_of(j * 2 * t, 2 * t), 2 * t, None)
        return carry

    lax.fori_loop(0, i // 2, body, 0)

    @pl.when(i % 2 == 1)
    def _():
        visit(pl.multiple_of((i - 1) * t, t), t, None)

    _fox_block(qs, mk_ref[...], mv_ref[...], mck_ref[...],
               lambda c, r: c < N_META, m_ref, acc_ref)

    acc = acc_ref[...]
    o = acc[:, :HEAD_DIM] / acc[:, HEAD_DIM:]
    for g in range(GROUP):
        o_ref[:, g * HEAD_DIM:(g + 1) * HEAD_DIM] = o[g * t:(g + 1) * t].astype(o_ref.dtype)


def _fox_prompt(q, k, v, ck, mk, mv, mck, batch, seq):
    t = FOX_T
    nb = seq // t
    return pl.pallas_call(
        _fox_prompt_kernel,
        out_shape=jax.ShapeDtypeStruct((batch * seq, W_Q), BF16),
        grid=(batch, N_KV_HEADS, nb),
        in_specs=[pl.BlockSpec((t, GROUP * HEAD_DIM), lambda b, n, i: (b * nb + i, n)),
                  pl.BlockSpec((seq, HEAD_DIM), lambda b, n, i: (b, n)),
                  pl.BlockSpec((seq, HEAD_DIM), lambda b, n, i: (b, n)),
                  pl.BlockSpec((None, None, GROUP, seq), lambda b, n, i: (b, n, 0, 0)),
                  pl.BlockSpec((LANES, HEAD_DIM), lambda b, n, i: (0, n)),
                  pl.BlockSpec((LANES, HEAD_DIM), lambda b, n, i: (0, n)),
                  pl.BlockSpec((None, None, GROUP, LANES), lambda b, n, i: (b, n, 0, 0))],
        out_specs=pl.BlockSpec((t, GROUP * HEAD_DIM), lambda b, n, i: (b * nb + i, n)),
        scratch_shapes=[pltpu.VMEM((GROUP * t, LANES), F32), pltpu.VMEM((GROUP * t, 2 * LANES), F32)],
        compiler_params=_params("parallel", "parallel", "arbitrary"),
        name="fox_prompt",
    )(q, k, v, ck, mk, mv, mck)


def _meta_attn_kernel(sq_ref, sk_ref, sv_ref, fq_ref, fk_ref, fv_ref, ck_ref, uo_ref,
                      osb_ref, ofx_ref, m_ref, acc_ref, r_ref, sacc_ref):
    t = N_META
    valid = lambda c: c < N_META
    r_ref[...] = jnp.zeros_like(r_ref)
    sacc_ref[...] = jnp.zeros_like(sacc_ref)
    _sb_block(_stack_heads(sq_ref[...]), sk_ref[...], sv_ref[...], uo_ref[...],
              lambda c, r: jnp.logical_and(c < r, valid(c)), r_ref, sacc_ref)
    o = sacc_ref[...]
    m_ref[...] = jnp.full_like(m_ref, NEG_BIG)
    acc_ref[...] = jnp.zeros_like(acc_ref)
    _fox_block(_stack_heads(fq_ref[...]), fk_ref[...], fv_ref[...], ck_ref[...],
               lambda c, r: jnp.logical_and(c <= r, valid(c)), m_ref, acc_ref)
    acc = acc_ref[...]
    f = acc[:, :HEAD_DIM] / acc[:, HEAD_DIM:]
    for g in range(GROUP):
        osb_ref[:, g * HEAD_DIM:(g + 1) * HEAD_DIM] = o[g * t:(g + 1) * t].astype(osb_ref.dtype)
        ofx_ref[:, g * HEAD_DIM:(g + 1) * HEAD_DIM] = f[g * t:(g + 1) * t].astype(ofx_ref.dtype)


def _meta_attn(sq, sk, sv, fq, fk, fv, ck, uo):
    qspec = pl.BlockSpec((N_META, GROUP * HEAD_DIM), lambda n: (0, n))
    kspec = pl.BlockSpec((LANES, HEAD_DIM), lambda n: (0, n))
    return pl.pallas_call(
        _meta_attn_kernel,
        out_shape=[jax.ShapeDtypeStruct((N_META, W_Q), BF16)] * 2,
        grid=(N_KV_HEADS,),
        in_specs=[qspec, kspec, kspec, qspec, kspec, kspec,
                  pl.BlockSpec((None, GROUP, LANES), lambda n: (n, 0, 0)),
                  pl.BlockSpec(uo.shape, lambda n: (0, 0))],
        out_specs=[qspec, qspec],
        scratch_shapes=[pltpu.VMEM((GROUP * N_META, LANES), F32),
                        pltpu.VMEM((GROUP * N_META, 2 * LANES), F32),
                        pltpu.VMEM((GROUP * N_META, LANES), F32),
                        pltpu.VMEM((GROUP * N_META, HEAD_DIM), F32)],
        compiler_params=_params("parallel"),
        name="meta_attn",
    )(sq, sk, sv, fq, fk, fv, ck, uo)


PAGE_COLS = PAGE_SIZE * N_KV_HEADS
DEC_G = 8


def _own_head(shape):
    row = lax.broadcasted_iota(jnp.int32, shape, 0)
    col = lax.broadcasted_iota(jnp.int32, shape, 1)
    return (col % N_KV_HEADS) == (row // GROUP)


def _sb_decode_kernel(pt_ref, q_ref, ck_hbm, cv_hbm, uo_ref, o_ref, kbuf, vbuf, sem, r_ref, acc_ref):
    b = pl.program_id(0)
    n_pages = pt_ref.shape[1]
    q = q_ref[...]
    uo = uo_ref[...]
    own = _own_head((N_Q_HEADS, PAGE_COLS))

    def copies(p, slot):
        pid = pt_ref[b, p]
        return [pltpu.make_async_copy(ck_hbm.at[pid], kbuf.at[slot], sem.at[0, slot]),
                pltpu.make_async_copy(cv_hbm.at[pid], vbuf.at[slot], sem.at[1, slot])]

    r_ref[...] = jnp.zeros_like(r_ref)
    acc_ref[...] = jnp.zeros_like(acc_ref)
    for c in copies(n_pages - 1, (n_pages - 1) % 2):
        c.start()

    def cond(c):
        return jnp.logical_and(c[0] >= 0, c[1] < SB_DEAD)

    def body(c):
        p = c[0]
        slot = p % 2
        for cp in copies(p, slot):
            cp.wait()

        @pl.when(p > 0)
        def _():
            for cp in copies(p - 1, 1 - slot):
                cp.start()

        z = lax.dot_general(q, kbuf[slot].astype(BF16), _NT, preferred_element_type=F32)
        sp = jnp.where(own, _softplus2(z), 0.0)
        ct = _split_dot(_split2(sp), uo)
        r = r_ref[...]
        a = jnp.where(own, jnp.exp2(z - ct[:, :PAGE_COLS] - jnp.tile(r, (1, N_KV_HEADS))), 0.0)
        acc_ref[...] += _dot(a.astype(BF16), vbuf[slot].astype(BF16))
        r_new = r + ct[:, PAGE_COLS:]
        r_ref[...] = r_new
        return p - 1, jnp.min(r_new)

    p_end, _ = lax.while_loop(cond, body, (jnp.int32(n_pages - 1), jnp.float32(0.0)))

    @pl.when(p_end >= 0)
    def _():
        for cp in copies(p_end, p_end % 2):
            cp.wait()

    o_ref[...] = acc_ref[...].astype(o_ref.dtype)


def _sb_decode(page_table, q, cache_k, cache_v, uo):
    nb = q.shape[0]
    any_spec = pl.BlockSpec(memory_space=pl.ANY)
    return pl.pallas_call(
        _sb_decode_kernel,
        out_shape=jax.ShapeDtypeStruct((nb, N_Q_HEADS, HEAD_DIM), BF16),
        grid_spec=pltpu.PrefetchScalarGridSpec(
            num_scalar_prefetch=1,
            grid=(nb,),
            in_specs=[pl.BlockSpec((None, N_Q_HEADS, HEAD_DIM), lambda b, pt: (b, 0, 0)),
                      any_spec, any_spec,
                      pl.BlockSpec(uo.shape, lambda b, pt: (0, 0))],
            out_specs=pl.BlockSpec((None, N_Q_HEADS, HEAD_DIM), lambda b, pt: (b, 0, 0)),
            scratch_shapes=[pltpu.VMEM((2, PAGE_COLS, HEAD_DIM), F32),
                            pltpu.VMEM((2, PAGE_COLS, HEAD_DIM), F32),
                            pltpu.SemaphoreType.DMA((2, 2)),
                            pltpu.VMEM((N_Q_HEADS, LANES), F32),
                            pltpu.VMEM((N_Q_HEADS, HEAD_DIM), F32)]),
        compiler_params=_params("arbitrary"),
        name="sb_decode",
    )(page_table, q, cache_k, cache_v, uo)


def _fox_decode_kernel(pt_ref, q_ref, kn_ref, vn_ref, lfn_ref, ck_hbm, cv_hbm, lf_hbm, us_ref, o_ref,
                       kbuf, vbuf, lbuf, sem, m_ref, acc_ref, d_ref):
    n_seq, n_pages = pt_ref.shape
    n_grp = n_pages // DEC_G
    total = n_seq * n_grp
    us = us_ref[...]
    bias = jnp.where(_own_head((N_Q_HEADS, PAGE_COLS)), 0.0, NEG_BIG)

    def copies(t, slot):
        b = t // n_grp
        newest = n_pages - 1 - (t % n_grp) * DEC_G
        out = []
        for g in range(DEC_G):
            pid = pt_ref[b, newest - g]
            out += [pltpu.make_async_copy(ck_hbm.at[pid], kbuf.at[slot, g], sem.at[0, slot]),
                    pltpu.make_async_copy(cv_hbm.at[pid], vbuf.at[slot, g], sem.at[1, slot]),
                    pltpu.make_async_copy(lf_hbm.at[pid], lbuf.at[slot, g], sem.at[2, slot])]
        return out

    for cp in copies(0, 0):
        cp.start()

    def body(t, carry):
        slot = t % 2
        b = t // n_grp
        gi = t % n_grp

        @pl.when(t + 1 < total)
        def _():
            for cp in copies(t + 1, 1 - slot):
                cp.start()

        q = q_ref[b]

        @pl.when(gi == 0)
        def _():
            z_self = jnp.sum(q.astype(F32) * kn_ref[b].astype(F32), axis=1, keepdims=True)
            m_ref[...] = jnp.broadcast_to(z_self, m_ref.shape)
            acc_ref[...] = jnp.concatenate([vn_ref[b].astype(F32), jnp.ones((N_Q_HEADS, LANES), F32)], axis=1)
            d_ref[...] = jnp.broadcast_to(lfn_ref[b] * LOG2E, d_ref.shape)

        for cp in copies(t, slot):
            cp.wait()

        dt = _split_dot(_split3(lbuf[slot].reshape(DEC_G * N_Q_HEADS, PAGE_SIZE)), us) * LOG2E
        d = d_ref[...]
        zs = []
        for g in range(DEC_G):
            dg = dt[g * N_Q_HEADS:(g + 1) * N_Q_HEADS]
            z = lax.dot_general(q, kbuf[slot, g].astype(BF16), _NT, preferred_element_type=F32)
            zs.append(z + dg[:, :PAGE_COLS] + jnp.tile(d, (1, N_KV_HEADS)) + bias)
            d = d + dg[:, PAGE_COLS:]
        d_ref[...] = d
        z = jnp.concatenate(zs, axis=1)
        m_prev = m_ref[...]
        m_new = jnp.maximum(m_prev, jnp.max(z, axis=1, keepdims=True))
        alpha = jnp.exp2(m_prev - m_new)
        p = jnp.exp2(z - jnp.tile(m_new, (1, z.shape[1] // LANES)))
        lsum = jnp.sum(p, axis=1, keepdims=True)
        pb = p.astype(BF16)
        pv = _dot(pb[:, :PAGE_COLS], vbuf[slot, 0].astype(BF16))
        for g in range(1, DEC_G):
            pv += _dot(pb[:, g * PAGE_COLS:(g + 1) * PAGE_COLS], vbuf[slot, g].astype(BF16))
        acc = acc_ref[...] * jnp.tile(alpha, (1, 2)) + jnp.concatenate(
            [pv, jnp.broadcast_to(lsum, pv.shape)], axis=1)
        acc_ref[...] = acc
        m_ref[...] = m_new

        @pl.when(gi == n_grp - 1)
        def _():
            o_ref[b] = (acc[:, :HEAD_DIM] / acc[:, HEAD_DIM:]).astype(o_ref.dtype)

        return carry

    lax.fori_loop(0, total, body, 0)


def _fox_decode(page_table, q, k_new, v_new, lf_new, cache_k, cache_v, cache_lf_t, us):
    nb = q.shape[0]
    assert page_table.shape[1] % DEC_G == 0
    any_spec = pl.BlockSpec(memory_space=pl.ANY)
    head_spec = pl.BlockSpec((nb, N_Q_HEADS, HEAD_DIM), lambda i, pt: (0, 0, 0))
    return pl.pallas_call(
        _fox_decode_kernel,
        out_shape=jax.ShapeDtypeStruct((nb, N_Q_HEADS, HEAD_DIM), BF16),
        grid_spec=pltpu.PrefetchScalarGridSpec(
            num_scalar_prefetch=1,
            grid=(1,),
            in_specs=[head_spec, head_spec, head_spec,
                      pl.BlockSpec((nb, N_Q_HEADS, 1), lambda i, pt: (0, 0, 0)),
                      any_spec, any_spec, any_spec,
                      pl.BlockSpec(us.shape, lambda i, pt: (0, 0))],
            out_specs=head_spec,
            scratch_shapes=[pltpu.VMEM((2, DEC_G, PAGE_COLS, HEAD_DIM), F32),
                            pltpu.VMEM((2, DEC_G, PAGE_COLS, HEAD_DIM), F32),
                            pltpu.VMEM((2, DEC_G, N_Q_HEADS, PAGE_SIZE), F32),
                            pltpu.SemaphoreType.DMA((3, 2)),
                            pltpu.VMEM((N_Q_HEADS, LANES), F32),
                            pltpu.VMEM((N_Q_HEADS, 2 * LANES), F32),
                            pltpu.VMEM((N_Q_HEADS, LANES), F32)]),
        compiler_params=_params("arbitrary"),
        name="fox_decode",
    )(page_table, q, k_new, v_new, lf_new, cache_k, cache_v, cache_lf_t, us)


def _out_proj_kernel(osb_ref, ofx_ref, gsb_ref, gfx_ref, w_ref, x_ref, y_ref, merged_ref):
    @pl.when(pl.program_id(1) == 0)
    def _():
        merged = (gsb_ref[...].astype(F32) * osb_ref[...].astype(F32)
                  + gfx_ref[...].astype(F32) * ofx_ref[...].astype(F32))
        merged_ref[...] = merged.astype(BF16)

    y_ref[...] = x_ref[...] + _dot(merged_ref[...], w_ref[...])


def _out_proj(o_sb, o_fx, gates, w_out, x, tm, tn):
    rows = x.shape[0]
    wide = lambda c: pl.BlockSpec((tm, W_Q), lambda i, j: (i, c))
    return pl.pallas_call(
        _out_proj_kernel,
        out_shape=jax.ShapeDtypeStruct((rows, D_MODEL), F32),
        grid=(rows // tm, D_MODEL // tn),
        in_specs=[wide(0), wide(0), wide(0), wide(1),
                  pl.BlockSpec((W_Q, tn), lambda i, j: (0, j)),
                  pl.BlockSpec((tm, tn), lambda i, j: (i, j))],
        out_specs=pl.BlockSpec((tm, tn), lambda i, j: (i, j)),
        scratch_shapes=[pltpu.VMEM((tm, W_Q), BF16)],
        compiler_params=_params("parallel", "arbitrary"),
        name="out_proj",
    )(o_sb, o_fx, gates, gates, w_out, x)


def _ffn_kernel(y_ref, g_ref, wu_ref, wd_ref, o_ref, h_ref):
    @pl.when(pl.program_id(1) == 0)
    def _():
        y = y_ref[...]
        h_ref[...] = _rms(y, g_ref[...]).astype(BF16)
        o_ref[...] = y

    u = jnp.square(jnp.maximum(_dot(h_ref[...], wu_ref[...]), 0.0))
    o_ref[...] += _dot(u.astype(BF16), wd_ref[...])


def _ffn(y, g, w_up, w_down, tm, tf):
    rows = y.shape[0]
    return pl.pallas_call(
        _ffn_kernel,
        out_shape=jax.ShapeDtypeStruct((rows, D_MODEL), F32),
        grid=(rows // tm, D_FF // tf),
        in_specs=[pl.BlockSpec((tm, D_MODEL), lambda i, k: (i, 0)),
                  pl.BlockSpec((1, D_MODEL), lambda i, k: (0, 0)),
                  pl.BlockSpec((D_MODEL, tf), lambda i, k: (0, k)),
                  pl.BlockSpec((tf, D_MODEL), lambda i, k: (k, 0))],
        out_specs=pl.BlockSpec((tm, D_MODEL), lambda i, k: (i, 0)),
        scratch_shapes=[pltpu.VMEM((tm, D_MODEL), BF16)],
        compiler_params=_params("parallel", "arbitrary"),
        name="ffn",
    )(y, g, w_up, w_down)


def _project_rows(x, wts, tm, tn):
    h = _rms_bf16(x, wts["g_mix"], tm)
    sq = _proj(h, wts["w_sq"], wts["g_q"], "scale", tm, tn, "proj_sq")
    fq = _proj(h, wts["w_fq"], wts["g_q"], "norm_scale", tm, tn, "proj_fq")
    gates = _proj(h, wts["w_gate"], wts["g_q"], "sigmoid", tm, tn, "proj_gates")
    kv = _kv_proj(h, wts["w_kv"], wts["g_k"], tm)
    lf = _forget(x, wts["g_mix"], wts["w_fl_hi"], wts["w_fl_lo"], wts["b_f"], tm)
    return sq, fq, gates, kv, lf


def _finish_rows(x, o_sb, o_fx, gates, wts, tm, tn):
    y1 = _out_proj(o_sb, o_fx, gates, wts["w_out"], x, tm, tn)
    return _ffn(y1, wts["g_ffn"], wts["w_up"], wts["w_down"], tm, tn)


def kernel(x_prompt, x_sample, cache_sb_k, cache_sb_v, cache_fox_k, cache_fox_v, cache_fox_logf,
           page_table, meta_tokens, g_mix, w_in, b_forget, g_q, g_k, w_out, g_ffn, w_up, w_down):
    batch, seq, _ = x_prompt.shape
    dec = x_sample.shape[0]
    assert w_in.shape[0] == 1 and x_sample.shape[1] == 1 and N_META + dec <= AUX_ROWS
    pool = cache_sb_k.shape[1]

    w = w_in[0]
    w_fl = jnp.pad(w[:, FORGET_OFFSET:FORGET_OFFSET + N_Q_HEADS], ((0, 0), (0, LANES - N_Q_HEADS)))
    w_fl_hi = w_fl.astype(BF16)
    wts = {
        "g_mix": g_mix, "g_q": g_q, "g_k": g_k, "g_ffn": g_ffn, "b_f": b_forget,
        "w_sq": w[:, :W_Q].astype(BF16),
        "w_fq": w[:, W_Q + 2 * W_KV:2 * W_Q + 2 * W_KV].astype(BF16),
        "w_kv": jnp.concatenate([w[:, W_Q:W_Q + 2 * W_KV], w[:, 2 * W_Q + 2 * W_KV:FORGET_OFFSET]],
                                axis=1).astype(BF16),
        "w_gate": w[:, FORGET_OFFSET + N_Q_HEADS:].astype(BF16),
        "w_fl_hi": w_fl_hi, "w_fl_lo": (w_fl - w_fl_hi.astype(F32)).astype(BF16),
        "w_out": w_out[0].astype(BF16), "w_up": w_up[0].astype(BF16), "w_down": w_down[0].astype(BF16),
    }

    x_main = x_prompt.reshape(batch * seq, D_MODEL)
    x_aux = jnp.concatenate([meta_tokens, x_sample.reshape(dec, D_MODEL),
                             jnp.zeros((AUX_ROWS - N_META - dec, D_MODEL), F32)], axis=0)

    sq_m, fq_m, gates_m, kv_m, lf_m = _project_rows(x_main, wts, 512, 512)
    sq_a, fq_a, gates_a, kv_a, lf_a = _project_rows(x_aux, wts, AUX_ROWS, 512)
    sk_m, sv_m, fk_m, fv_m, skb_m, svb_m, fkb_m, fvb_m = kv_m
    sk_a, sv_a, fk_a, fv_a, skb_a, svb_a, fkb_a, fvb_a = kv_a

    lf_meta_t = jnp.pad(lf_a[:N_META].T, ((0, 0), (0, LANES - N_META)))
    lf_main_t = jnp.swapaxes(lf_m.reshape(batch, seq, N_Q_HEADS), 1, 2)
    ck_meta, ck_main = _cum_forget(lf_meta_t, lf_main_t)
    ck_meta = ck_meta.reshape(batch, N_KV_HEADS, GROUP, LANES)
    ck_main = ck_main.reshape(batch, N_KV_HEADS, GROUP, seq)

    pad_keys = lambda a: jnp.pad(a[:N_META], ((0, LANES - N_META), (0, 0)))
    mk_sb, mv_sb, mk_fx, mv_fx = pad_keys(skb_a), pad_keys(svb_a), pad_keys(fkb_a), pad_keys(fvb_a)

    j_idx = lax.broadcasted_iota(jnp.int32, (LANES, LANES), 0)
    s_idx = lax.broadcasted_iota(jnp.int32, (LANES, LANES), 1)
    ones = jnp.ones((LANES, LANES), BF16)
    uo = jnp.concatenate([(j_idx >= s_idx).astype(BF16), ones], axis=1)
    uo = jnp.concatenate([uo] * 2, axis=0)
    c_src = lax.broadcasted_iota(jnp.int32, (PAGE_COLS, PAGE_COLS), 0) // N_KV_HEADS
    c_dst = lax.broadcasted_iota(jnp.int32, (PAGE_COLS, PAGE_COLS), 1) // N_KV_HEADS
    uo_page = jnp.concatenate([(c_src >= c_dst).astype(BF16), jnp.ones((PAGE_COLS, LANES), BF16)], axis=1)
    uo_page = jnp.concatenate([uo_page] * 2, axis=0)
    j_key = lax.broadcasted_iota(jnp.int32, (PAGE_SIZE, PAGE_COLS), 0)
    c_key = lax.broadcasted_iota(jnp.int32, (PAGE_SIZE, PAGE_COLS), 1) // N_KV_HEADS
    us_page = jnp.concatenate([(j_key > c_key).astype(BF16), ones], axis=1)
    us_page = jnp.concatenate([us_page] * 3, axis=0)

    o_sb_m = _sb_prompt(sq_m, skb_m, svb_m, mk_sb, mv_sb, uo, batch, seq)
    o_fx_m = _fox_prompt(fq_m, fkb_m, fvb_m, ck_main, mk_fx, mv_fx, ck_meta, batch, seq)
    o_sb_meta, o_fx_meta = _meta_attn(sq_a[:N_META], mk_sb, mv_sb, fq_a[:N_META], mk_fx, mv_fx,
                                      ck_meta[0], uo)

    heads = lambda a: a[N_META:N_META + dec].reshape(dec, N_Q_HEADS, HEAD_DIM)
    kv_heads = lambda a: jnp.repeat(a[N_META:N_META + dec].reshape(dec, N_KV_HEADS, HEAD_DIM), GROUP, axis=1)
    pages = lambda c: c[0].reshape(pool, PAGE_COLS, HEAD_DIM)
    o_sb_dec = _sb_decode(page_table, heads(sq_a), pages(cache_sb_k), pages(cache_sb_v), uo_page)
    o_fx_dec = _fox_decode(page_table, heads(fq_a), kv_heads(fkb_a), kv_heads(fvb_a),
                           lf_a[N_META:N_META + dec].reshape(dec, N_Q_HEADS, 1),
                           pages(cache_fox_k), pages(cache_fox_v),
                           jnp.swapaxes(cache_fox_logf[0], 1, 2), us_page)

    tail = jnp.zeros((AUX_ROWS - N_META - dec, W_Q), BF16)
    o_sb_a = jnp.concatenate([o_sb_meta, o_sb_dec.reshape(dec, W_Q), tail], axis=0)
    o_fx_a = jnp.concatenate([o_fx_meta, o_fx_dec.reshape(dec, W_Q), tail], axis=0)

    y_main = _finish_rows(x_main, o_sb_m, o_fx_m, gates_m, wts, 512, 512)
    y_aux = _finish_rows(x_aux, o_sb_a, o_fx_a, gates_a, wts, AUX_ROWS, 512)

    def prompt_state(a_aux, a_main, tail_shape):
        meta = jnp.broadcast_to(a_aux[None, :N_META], (batch, N_META, a_aux.shape[1]))
        full = jnp.concatenate([meta, a_main.reshape(batch, seq, a_aux.shape[1])], axis=1)
        return full.reshape((1, batch, seq + N_META) + tail_shape)

    kv_shape = (N_KV_HEADS, HEAD_DIM)
    sample_state = lambda a, tail_shape: a[N_META:N_META + dec].reshape((1, dec, 1) + tail_shape)
    return (y_main.reshape(batch, seq, D_MODEL), y_aux[N_META:N_META + dec].reshape(dec, 1, D_MODEL),
            prompt_state(sk_a, sk_m, kv_shape), prompt_state(sv_a, sv_m, kv_shape),
            prompt_state(fk_a, fk_m, kv_shape), prompt_state(fv_a, fv_m, kv_shape),
            prompt_state(lf_a, lf_m, (N_Q_HEADS,)),
            sample_state(sk_a, kv_shape), sample_state(sv_a, kv_shape),
            sample_state(fk_a, kv_shape), sample_state(fv_a, kv_shape),
            sample_state(lf_a, (N_Q_HEADS,)))
```

```python
import functools

import jax
import jax.numpy as jnp
from jax import lax
from jax.experimental import pallas as pl
from jax.experimental.pallas import tpu as pltpu

D_MODEL = 2048
HEAD_DIM = 128
N_Q_HEADS = 16
N_KV_HEADS = 4
GROUP = 4
W_Q = N_Q_HEADS * HEAD_DIM
W_KV = N_KV_HEADS * HEAD_DIM
D_FF = 4 * D_MODEL
N_META = 16
PAGE_SIZE = 128
EPS = 1e-6
LOG2E = 1.4426950408889634
Q_SCALE = HEAD_DIM ** -0.5 * LOG2E
FORGET_OFFSET = 2 * W_Q + 4 * W_KV
AUX_ROWS = 32

BF16 = jnp.bfloat16
F32 = jnp.float32

LANES = 128
NEG_BIG = -1e30
SB_DEAD = 104.0 * LOG2E
VMEM_LIMIT = 48 * 1024 * 1024
FFN_VMEM_LIMIT = 58 * 1024 * 1024
TM_MAIN = 1024
TM_PROJ = 2048
TN = 512

_NT = (((1,), (1,)), ((), ()))


def _params(*sem):
    return pltpu.CompilerParams(dimension_semantics=sem, vmem_limit_bytes=VMEM_LIMIT)


def _rms(x, g):
    ms = jnp.mean(x * x, axis=-1, keepdims=True)
    return x * lax.rsqrt(ms + EPS) * g


def _softplus(z):
    return jnp.maximum(z, 0.0) + jnp.log1p(jnp.exp(-jnp.abs(z)))


def _softplus2(z2):
    return jnp.maximum(z2, 0.0) + jnp.log2(1.0 + jnp.exp2(-jnp.abs(z2)))


def _split2(x):
    hi = x.astype(BF16)
    lo = (x - hi.astype(F32)).astype(BF16)
    return hi, lo


def _split3(x):
    hi = x.astype(BF16)
    r = x - hi.astype(F32)
    mid = r.astype(BF16)
    lo = (r - mid.astype(F32)).astype(BF16)
    return hi, mid, lo


def _dot(a, b):
    return jnp.dot(a, b, preferred_element_type=F32)


def _split_dot(parts, m_stacked):
    return _dot(jnp.concatenate(parts, axis=1), m_stacked)


def _rms_forget_kernel(x_ref, g_ref, whi_ref, wlo_ref, b_ref, h_ref, lf_ref):
    h = _rms(x_ref[...], g_ref[...])
    h_ref[...] = h.astype(h_ref.dtype)
    hh, hl = _split2(h)
    z = _dot(hh, whi_ref[...]) + _dot(hl, whi_ref[...]) + _dot(hh, wlo_ref[...])
    z = z[:, :N_Q_HEADS] + b_ref[...]
    lf_ref[...] = -_softplus(-z)


def _rms_forget(x, g, w_hi, w_lo, b, tm):
    rows = x.shape[0]
    const = lambda shape: pl.BlockSpec(shape, lambda i: (0, 0))
    return pl.pallas_call(
        _rms_forget_kernel,
        out_shape=[jax.ShapeDtypeStruct((rows, D_MODEL), BF16),
                   jax.ShapeDtypeStruct((rows, N_Q_HEADS), F32)],
        grid=(rows // tm,),
        in_specs=[pl.BlockSpec((tm, D_MODEL), lambda i: (i, 0)),
                  const((1, D_MODEL)), const((D_MODEL, LANES)), const((D_MODEL, LANES)),
                  const((1, N_Q_HEADS))],
        out_specs=[pl.BlockSpec((tm, D_MODEL), lambda i: (i, 0)),
                   pl.BlockSpec((tm, N_Q_HEADS), lambda i: (i, 0))],
        compiler_params=_params("parallel"),
        name="rms_forget",
    )(x, g, w_hi, w_lo, b)


def _head_norm(acc, g):
    outs = []
    for hh in range(acc.shape[1] // HEAD_DIM):
        xh = acc[:, hh * HEAD_DIM:(hh + 1) * HEAD_DIM]
        outs.append(_rms(xh, g))
    return jnp.concatenate(outs, axis=1)


def _proj_kernel(h_ref, w_ref, g_ref, o_ref, *, mode):
    acc = _dot(h_ref[...], w_ref[...])
    if mode == "scale":
        acc = acc * Q_SCALE
    elif mode == "norm_scale":
        acc = _head_norm(acc, g_ref[...]) * Q_SCALE
    elif mode == "sigmoid":
        acc = jax.nn.sigmoid(acc)
    o_ref[...] = acc.astype(o_ref.dtype)


def _col_tiles(w, tn):
    k, n = w.shape
    return w.reshape(k, n // tn, tn).transpose(1, 0, 2)


def _proj(h, w_tiles, g, mode, tm, name):
    rows = h.shape[0]
    nt, _, tn = w_tiles.shape
    return pl.pallas_call(
        functools.partial(_proj_kernel, mode=mode),
        out_shape=jax.ShapeDtypeStruct((rows, nt * tn), BF16),
        grid=(rows // tm, nt),
        in_specs=[pl.BlockSpec((tm, D_MODEL), lambda i, j: (i, 0)),
                  pl.BlockSpec((None, D_MODEL, tn), lambda i, j: (j, 0, 0)),
                  pl.BlockSpec((1, HEAD_DIM), lambda i, j: (0, 0))],
        out_specs=pl.BlockSpec((tm, tn), lambda i, j: (i, j)),
        compiler_params=_params("parallel", "arbitrary"),
        name=name,
    )(h, w_tiles, g)


def _kv_kernel(h_ref, w_ref, gk_ref, *out_refs):
    j = pl.program_id(1)
    acc = _dot(h_ref[...], w_ref[...])
    for jj in range(4):
        @pl.when(j == jj)
        def _(jj=jj):
            val = _head_norm(acc, gk_ref[...]) if jj == 2 else acc
            out_refs[jj][...] = val
            out_refs[4 + jj][...] = val.astype(BF16)


def _kv_proj(h, w_kv, g_k, tm):
    rows = h.shape[0]
    blk = pl.BlockSpec((tm, W_KV), lambda i, j: (i, 0))
    return pl.pallas_call(
        _kv_kernel,
        out_shape=[jax.ShapeDtypeStruct((rows, W_KV), F32)] * 4
        + [jax.ShapeDtypeStruct((rows, W_KV), BF16)] * 4,
        grid=(rows // tm, 4),
        in_specs=[pl.BlockSpec((tm, D_MODEL), lambda i, j: (i, 0)),
                  pl.BlockSpec((None, D_MODEL, W_KV), lambda i, j: (j, 0, 0)),
                  pl.BlockSpec((1, HEAD_DIM), lambda i, j: (0, 0))],
        out_specs=[blk] * 8,
        compiler_params=_params("parallel", "arbitrary"),
        name="kv_proj",
    )(h, w_kv, g_k)


def _lane_cumsum(x):
    n = x.shape[-1]
    lane = lax.broadcasted_iota(jnp.int32, x.shape, x.ndim - 1)
    k = 1
    while k < n:
        x = x + jnp.where(lane >= k, pltpu.roll(x, k, axis=x.ndim - 1), 0.0)
        k *= 2
    return x


def _cum_kernel(meta_ref, main_ref, cmeta_ref, cmain_ref):
    cm = _lane_cumsum(meta_ref[...])
    cmeta_ref[...] = cm * LOG2E
    cmain_ref[...] = (_lane_cumsum(main_ref[...]) + cm[:, LANES - 1:LANES]) * LOG2E


def _cum_forget(lf_meta_t, lf_main_t):
    b, _, s = lf_main_t.shape
    return pl.pallas_call(
        _cum_kernel,
        out_shape=[jax.ShapeDtypeStruct((b, N_Q_HEADS, LANES), F32),
                   jax.ShapeDtypeStruct((b, N_Q_HEADS, s), F32)],
        grid=(b,),
        in_specs=[pl.BlockSpec((N_Q_HEADS, LANES), lambda i: (0, 0)),
                  pl.BlockSpec((None, N_Q_HEADS, s), lambda i: (i, 0, 0))],
        out_specs=[pl.BlockSpec((None, N_Q_HEADS, LANES), lambda i: (i, 0, 0)),
                   pl.BlockSpec((None, N_Q_HEADS, s), lambda i: (i, 0, 0))],
        compiler_params=_params("parallel"),
        name="cum_forget",
    )(lf_meta_t, lf_main_t)


def _stack_heads(q):
    return jnp.concatenate([q[:, g * HEAD_DIM:(g + 1) * HEAD_DIM] for g in range(GROUP)], axis=0)


SB_CHUNK = 512
FOX_CHUNK = 256


def _chunk_visible(rel, c, cr, t, tk):
    col = lax.broadcasted_iota(jnp.int32, (cr, tk), 1)
    row = (lax.broadcasted_iota(jnp.int32, (cr, tk), 0) + c * cr) % t
    return rel(col, row)


def _sb_block(qs, kb, vb, uo, rel, r_ref, acc_ref):
    r, tk = qs.shape[0], kb.shape[0]
    t = r // GROUP
    cr = min(r, SB_CHUNK)
    score = lambda c: lax.dot_general(qs[c * cr:(c + 1) * cr], kb, _NT, preferred_element_type=F32)
    z_next = score(0)
    for c in range(r // cr):
        rows = slice(c * cr, (c + 1) * cr)
        z = z_next
        if c + 1 < r // cr:
            z_next = score(c + 1)
        sp = _softplus2(z)
        if rel is not None:
            vis = _chunk_visible(rel, c, cr, t, tk)
            sp = jnp.where(vis, sp, 0.0)
        ct = _split_dot(_split2(sp), uo)
        r_prev = r_ref[rows]
        a = jnp.exp2(z - ct[:, :tk] - r_prev)
        if rel is not None:
            a = jnp.where(vis, a, 0.0)
        acc_ref[rows] += _dot(a.astype(BF16), vb)
        r_ref[rows] = r_prev + ct[:, tk:]


def _fox_block(qs, kb, vb, ckb, rel, m_ref, acc_ref):
    r, tk = qs.shape[0], kb.shape[0]
    t = r // GROUP
    cr = min(t, FOX_CHUNK)
    v1 = jnp.concatenate([vb, jnp.ones((tk, LANES), BF16)], axis=1)
    score = lambda c: lax.dot_general(qs[c * cr:(c + 1) * cr], kb, _NT, preferred_element_type=F32)
    z_next = score(0)
    for c in range(r // cr):
        rows = slice(c * cr, (c + 1) * cr)
        g = c * cr // t
        z = z_next - ckb[g:g + 1, :]
        if c + 1 < r // cr:
            z_next = score(c + 1)
        if rel is not None:
            z = jnp.where(_chunk_visible(rel, c, cr, t, tk), z, NEG_BIG)
        m_prev = m_ref[rows]
        m_new = jnp.maximum(m_prev, jnp.max(z, axis=1, keepdims=True))
        alpha = jnp.exp2(m_prev - m_new)
        p = jnp.exp2(z - jnp.tile(m_new, (1, tk // LANES)))
        acc_ref[rows] = acc_ref[rows] * jnp.tile(alpha, (1, 2)) + _dot(p.astype(BF16), v1)
        m_ref[rows] = m_new


SB_T = 128


def _store_heads(o_ref, o, t, gate_ref, add_ref=None):
    for g in range(GROUP):
        cols = slice(g * HEAD_DIM, (g + 1) * HEAD_DIM)
        val = o[g * t:(g + 1) * t] * gate_ref[:, cols].astype(F32)
        if add_ref is not None:
            val = val + add_ref[:, cols].astype(F32)
        o_ref[:, cols] = val.astype(o_ref.dtype)


def _sb_prompt_kernel(q_ref, k_ref, v_ref, mk_ref, mv_ref, uo_ref, gate_ref, o_ref, r_ref, acc_ref):
    i = pl.program_id(2)
    t = SB_T
    qs = _stack_heads(q_ref[...])
    uo = uo_ref[...]

    def visit(kb, vb, rel):
        _sb_block(qs, kb, vb, uo, rel, r_ref, acc_ref)
        return jnp.min(r_ref[...])

    r_ref[...] = jnp.zeros_like(r_ref)
    acc_ref[...] = jnp.zeros_like(acc_ref)
    start = pl.multiple_of(i * t, t)
    mn = visit(k_ref[pl.ds(start, t), :], v_ref[pl.ds(start, t), :], lambda c, r: c < r)

    def cond(c):
        return jnp.logical_and(c[0] >= 0, c[1] < SB_DEAD)

    def body(c):
        s = pl.multiple_of(c[0] * t, t)
        return c[0] - 1, visit(k_ref[pl.ds(s, t), :], v_ref[pl.ds(s, t), :], None)

    _, mn = lax.while_loop(cond, body, (i - 1, mn))

    @pl.when(mn < SB_DEAD)
    def _():
        visit(mk_ref[...], mv_ref[...], lambda c, r: c < N_META)

    _store_heads(o_ref, acc_ref[...], t, gate_ref)


def _sb_prompt(q, k, v, mk, mv, uo, gates, batch, seq):
    t = SB_T
    nb = seq // t
    tile = pl.BlockSpec((t, GROUP * HEAD_DIM), lambda b, n, i: (b * nb + i, n))
    return pl.pallas_call(
        _sb_prompt_kernel,
        out_shape=jax.ShapeDtypeStruct((batch * seq, W_Q), BF16),
        grid=(batch, N_KV_HEADS, nb),
        in_specs=[tile,
                  pl.BlockSpec((seq, HEAD_DIM), lambda b, n, i: (b, n)),
                  pl.BlockSpec((seq, HEAD_DIM), lambda b, n, i: (b, n)),
                  pl.BlockSpec((LANES, HEAD_DIM), lambda b, n, i: (0, n)),
                  pl.BlockSpec((LANES, HEAD_DIM), lambda b, n, i: (0, n)),
                  pl.BlockSpec(uo.shape, lambda b, n, i: (0, 0)),
                  tile],
        out_specs=tile,
        scratch_shapes=[pltpu.VMEM((GROUP * t, t), F32), pltpu.VMEM((GROUP * t, HEAD_DIM), F32)],
        compiler_params=_params("parallel", "parallel", "arbitrary"),
        name="sb_prompt",
    )(q, k, v, mk, mv, uo, gates)


FOX_T = 256


def _fox_prompt_kernel(q_ref, k_ref, v_ref, ck_ref, mk_ref, mv_ref, mck_ref, gate_ref, sb_ref,
                       o_ref, m_ref, acc_ref):
    i = pl.program_id(2)
    t = FOX_T
    qs = _stack_heads(q_ref[...])
    m_ref[...] = jnp.full_like(m_ref, NEG_BIG)
    acc_ref[...] = jnp.zeros_like(acc_ref)

    def visit(s, width, rel):
        _fox_block(qs, k_ref[pl.ds(s, width), :], v_ref[pl.ds(s, width), :], ck_ref[:, pl.ds(s, width)],
                   rel, m_ref, acc_ref)

    visit(pl.multiple_of(i * t, t), t, lambda c, r: c <= r)

    def body(j, carry):
        visit(pl.multiple_of(j * 2 * t, 2 * t), 2 * t, None)
        return carry

    lax.fori_loop(0, i // 2, body, 0)

    @pl.when(i % 2 == 1)
    def _():
        visit(pl.multiple_of((i - 1) * t, t), t, None)

    _fox_block(qs, mk_ref[...], mv_ref[...], mck_ref[...],
               lambda c, r: c < N_META, m_ref, acc_ref)

    acc = acc_ref[...]
    _store_heads(o_ref, acc[:, :HEAD_DIM] / acc[:, HEAD_DIM:], t, gate_ref, sb_ref)


def _fox_prompt(q, k, v, ck, mk, mv, mck, gates, sb_part, batch, seq):
    t = FOX_T
    nb = seq // t
    tile = pl.BlockSpec((t, GROUP * HEAD_DIM), lambda b, n, i: (b * nb + i, n))
    return pl.pallas_call(
        _fox_prompt_kernel,
        out_shape=jax.ShapeDtypeStruct((batch * seq, W_Q), BF16),
        grid=(batch, N_KV_HEADS, nb),
        in_specs=[tile,
                  pl.BlockSpec((seq, HEAD_DIM), lambda b, n, i: (b, n)),
                  pl.BlockSpec((seq, HEAD_DIM), lambda b, n, i: (b, n)),
                  pl.BlockSpec((None, None, GROUP, seq), lambda b, n, i: (b, n, 0, 0)),
                  pl.BlockSpec((LANES, HEAD_DIM), lambda b, n, i: (0, n)),
                  pl.BlockSpec((LANES, HEAD_DIM), lambda b, n, i: (0, n)),
                  pl.BlockSpec((None, None, GROUP, LANES), lambda b, n, i: (b, n, 0, 0)),
                  pl.BlockSpec((t, GROUP * HEAD_DIM), lambda b, n, i: (b * nb + i, N_KV_HEADS + n)),
                  tile],
        out_specs=tile,
        scratch_shapes=[pltpu.VMEM((GROUP * t, LANES), F32), pltpu.VMEM((GROUP * t, 2 * LANES), F32)],
        compiler_params=_params("parallel", "parallel", "arbitrary"),
        name="fox_prompt",
    )(q, k, v, ck, mk, mv, mck, gates, sb_part)


def _meta_attn_kernel(sq_ref, sk_ref, sv_ref, fq_ref, fk_ref, fv_ref, ck_ref, uo_ref, gsb_ref, gfx_ref,
                      o_ref, m_ref, acc_ref, r_ref, sacc_ref):
    t = N_META
    valid = lambda c: c < N_META
    r_ref[...] = jnp.zeros_like(r_ref)
    sacc_ref[...] = jnp.zeros_like(sacc_ref)
    _sb_block(_stack_heads(sq_ref[...]), sk_ref[...], sv_ref[...], uo_ref[...],
              lambda c, r: jnp.logical_and(c < r, valid(c)), r_ref, sacc_ref)
    o = sacc_ref[...]
    m_ref[...] = jnp.full_like(m_ref, NEG_BIG)
    acc_ref[...] = jnp.zeros_like(acc_ref)
    _fox_block(_stack_heads(fq_ref[...]), fk_ref[...], fv_ref[...], ck_ref[...],
               lambda c, r: jnp.logical_and(c <= r, valid(c)), m_ref, acc_ref)
    acc = acc_ref[...]
    f = acc[:, :HEAD_DIM] / acc[:, HEAD_DIM:]
    for g in range(GROUP):
        cols = slice(g * HEAD_DIM, (g + 1) * HEAD_DIM)
        merged = (gsb_ref[:, cols].astype(F32) * o[g * t:(g + 1) * t]
                  + gfx_ref[:, cols].astype(F32) * f[g * t:(g + 1) * t])
        o_ref[:, cols] = merged.astype(o_ref.dtype)


def _meta_attn(sq, sk, sv, fq, fk, fv, ck, uo, gates):
    qspec = pl.BlockSpec((N_META, GROUP * HEAD_DIM), lambda n: (0, n))
    kspec = pl.BlockSpec((LANES, HEAD_DIM), lambda n: (0, n))
    return pl.pallas_call(
        _meta_attn_kernel,
        out_shape=jax.ShapeDtypeStruct((N_META, W_Q), BF16),
        grid=(N_KV_HEADS,),
        in_specs=[qspec, kspec, kspec, qspec, kspec, kspec,
                  pl.BlockSpec((None, GROUP, LANES), lambda n: (n, 0, 0)),
                  pl.BlockSpec(uo.shape, lambda n: (0, 0)),
                  qspec,
                  pl.BlockSpec((N_META, GROUP * HEAD_DIM), lambda n: (0, N_KV_HEADS + n))],
        out_specs=qspec,
        scratch_shapes=[pltpu.VMEM((GROUP * N_META, LANES), F32),
                        pltpu.VMEM((GROUP * N_META, 2 * LANES), F32),
                        pltpu.VMEM((GROUP * N_META, LANES), F32),
                        pltpu.VMEM((GROUP * N_META, HEAD_DIM), F32)],
        compiler_params=_params("parallel"),
        name="meta_attn",
    )(sq, sk, sv, fq, fk, fv, ck, uo, gates, gates)


PAGE_COLS = PAGE_SIZE * N_KV_HEADS
DEC_G = 8


def _own_head(shape):
    row = lax.broadcasted_iota(jnp.int32, shape, 0)
    col = lax.broadcasted_iota(jnp.int32, shape, 1)
    return (col % N_KV_HEADS) == (row // GROUP)


def _sb_decode_kernel(pt_ref, q_ref, gate_ref, ck_hbm, cv_hbm, uo_ref, o_ref, kbuf, vbuf, sem, r_ref, acc_ref):
    b = pl.program_id(0)
    n_pages = pt_ref.shape[1]
    q = q_ref[...]
    uo = uo_ref[...]
    own = _own_head((N_Q_HEADS, PAGE_COLS))

    def copies(p, slot):
        pid = pt_ref[b, p]
        return [pltpu.make_async_copy(ck_hbm.at[pid], kbuf.at[slot], sem.at[0, slot]),
                pltpu.make_async_copy(cv_hbm.at[pid], vbuf.at[slot], sem.at[1, slot])]

    r_ref[...] = jnp.zeros_like(r_ref)
    acc_ref[...] = jnp.zeros_like(acc_ref)
    for c in copies(n_pages - 1, (n_pages - 1) % 2):
        c.start()

    def cond(c):
        return jnp.logical_and(c[0] >= 0, c[1] < SB_DEAD)

    def body(c):
        p = c[0]
        slot = p % 2
        for cp in copies(p, slot):
            cp.wait()

        @pl.when(p > 0)
        def _():
            for cp in copies(p - 1, 1 - slot):
                cp.start()

        z = lax.dot_general(q, kbuf[slot].astype(BF16), _NT, preferred_element_type=F32)
        sp = jnp.where(own, _softplus2(z), 0.0)
        ct = _split_dot(_split2(sp), uo)
        r = r_ref[...]
        a = jnp.where(own, jnp.exp2(z - ct[:, :PAGE_COLS] - jnp.tile(r, (1, N_KV_HEADS))), 0.0)
        acc_ref[...] += _dot(a.astype(BF16), vbuf[slot].astype(BF16))
        r_new = r + ct[:, PAGE_COLS:]
        r_ref[...] = r_new
        return p - 1, jnp.min(r_new)

    p_end, _ = lax.while_loop(cond, body, (jnp.int32(n_pages - 1), jnp.float32(0.0)))

    @pl.when(p_end >= 0)
    def _():
        for cp in copies(p_end, p_end % 2):
            cp.wait()

    o_ref[...] = (acc_ref[...] * gate_ref[...].astype(F32)).astype(o_ref.dtype)


def _sb_decode(page_table, q, gate, cache_k, cache_v, uo):
    nb = q.shape[0]
    any_spec = pl.BlockSpec(memory_space=pl.ANY)
    head_spec = pl.BlockSpec((None, N_Q_HEADS, HEAD_DIM), lambda b, pt: (b, 0, 0))
    return pl.pallas_call(
        _sb_decode_kernel,
        out_shape=jax.ShapeDtypeStruct((nb, N_Q_HEADS, HEAD_DIM), BF16),
        grid_spec=pltpu.PrefetchScalarGridSpec(
            num_scalar_prefetch=1,
            grid=(nb,),
            in_specs=[head_spec, head_spec, any_spec, any_spec,
                      pl.BlockSpec(uo.shape, lambda b, pt: (0, 0))],
            out_specs=head_spec,
            scratch_shapes=[pltpu.VMEM((2, PAGE_COLS, HEAD_DIM), F32),
                            pltpu.VMEM((2, PAGE_COLS, HEAD_DIM), F32),
                            pltpu.SemaphoreType.DMA((2, 2)),
                            pltpu.VMEM((N_Q_HEADS, LANES), F32),
                            pltpu.VMEM((N_Q_HEADS, HEAD_DIM), F32)]),
        compiler_params=_params("arbitrary"),
        name="sb_decode",
    )(page_table, q, gate, cache_k, cache_v, uo)


def _fox_decode_kernel(pt_ref, q_ref, kn_ref, vn_ref, lfn_ref, gate_ref, sb_ref, ck_hbm, cv_hbm, lf_hbm,
                       us_ref, o_ref, kbuf, vbuf, lbuf, sem, m_ref, acc_ref, d_ref):
    n_seq, n_pages = pt_ref.shape
    n_grp = n_pages // DEC_G
    total = n_seq * n_grp
    us = us_ref[...]
    bias = jnp.where(_own_head((N_Q_HEADS, PAGE_COLS)), 0.0, NEG_BIG)

    def copies(t, slot):
        b = t // n_grp
        newest = n_pages - 1 - (t % n_grp) * DEC_G
        out = []
        for g in range(DEC_G):
            pid = pt_ref[b, newest - g]
            out += [pltpu.make_async_copy(ck_hbm.at[pid], kbuf.at[slot, g], sem.at[0, slot]),
                    pltpu.make_async_copy(cv_hbm.at[pid], vbuf.at[slot, g], sem.at[1, slot]),
                    pltpu.make_async_copy(lf_hbm.at[pid], lbuf.at[slot, g], sem.at[2, slot])]
        return out

    for cp in copies(0, 0):
        cp.start()

    def body(t, carry):
        slot = t % 2
        b = t // n_grp
        gi = t % n_grp

        @pl.when(t + 1 < total)
        def _():
            for cp in copies(t + 1, 1 - slot):
                cp.start()

        q = q_ref[b]

        @pl.when(gi == 0)
        def _():
            z_self = jnp.sum(q.astype(F32) * kn_ref[b].astype(F32), axis=1, keepdims=True)
            m_ref[...] = jnp.broadcast_to(z_self, m_ref.shape)
            acc_ref[...] = jnp.concatenate([vn_ref[b].astype(F32), jnp.ones((N_Q_HEADS, LANES), F32)], axis=1)
            d_ref[...] = jnp.broadcast_to(lfn_ref[b] * LOG2E, d_ref.shape)

        for cp in copies(t, slot):
            cp.wait()

        dt = _split_dot(_split3(lbuf[slot].reshape(DEC_G * N_Q_HEADS, PAGE_SIZE)), us) * LOG2E
        d = d_ref[...]
        zs = []
        for g in range(DEC_G):
            dg = dt[g * N_Q_HEADS:(g + 1) * N_Q_HEADS]
            z = lax.dot_general(q, kbuf[slot, g].astype(BF16), _NT, preferred_element_type=F32)
            zs.append(z + dg[:, :PAGE_COLS] + jnp.tile(d, (1, N_KV_HEADS)) + bias)
            d = d + dg[:, PAGE_COLS:]
        d_ref[...] = d
        z = jnp.concatenate(zs, axis=1)
        m_prev = m_ref[...]
        m_new = jnp.maximum(m_prev, jnp.max(z, axis=1, keepdims=True))
        alpha = jnp.exp2(m_prev - m_new)
        p = jnp.exp2(z - jnp.tile(m_new, (1, z.shape[1] // LANES)))
        lsum = jnp.sum(p, axis=1, keepdims=True)
        pb = p.astype(BF16)
        pv = _dot(pb[:, :PAGE_COLS], vbuf[slot, 0].astype(BF16))
        for g in range(1, DEC_G):
            pv += _dot(pb[:, g * PAGE_COLS:(g + 1) * PAGE_COLS], vbuf[slot, g].astype(BF16))
        acc = acc_ref[...] * jnp.tile(alpha, (1, 2)) + jnp.concatenate(
            [pv, jnp.broadcast_to(lsum, pv.shape)], axis=1)
        acc_ref[...] = acc
        m_ref[...] = m_new

        @pl.when(gi == n_grp - 1)
        def _():
            merged = acc[:, :HEAD_DIM] / acc[:, HEAD_DIM:] * gate_ref[b].astype(F32) + sb_ref[b].astype(F32)
            o_ref[b] = merged.astype(o_ref.dtype)

        return carry

    lax.fori_loop(0, total, body, 0)


def _fox_decode(page_table, q, k_new, v_new, lf_new, gate, sb_part, cache_k, cache_v, cache_lf_t, us):
    nb = q.shape[0]
    assert page_table.shape[1] % DEC_G == 0
    any_spec = pl.BlockSpec(memory_space=pl.ANY)
    head_spec = pl.BlockSpec((nb, N_Q_HEADS, HEAD_DIM), lambda i, pt: (0, 0, 0))
    return pl.pallas_call(
        _fox_decode_kernel,
        out_shape=jax.ShapeDtypeStruct((nb, N_Q_HEADS, HEAD_DIM), BF16),
        grid_spec=pltpu.PrefetchScalarGridSpec(
            num_scalar_prefetch=1,
            grid=(1,),
            in_specs=[head_spec, head_spec, head_spec,
                      pl.BlockSpec((nb, N_Q_HEADS, 1), lambda i, pt: (0, 0, 0)),
                      head_spec, head_spec,
                      any_spec, any_spec, any_spec,
                      pl.BlockSpec(us.shape, lambda i, pt: (0, 0))],
            out_specs=head_spec,
            scratch_shapes=[pltpu.VMEM((2, DEC_G, PAGE_COLS, HEAD_DIM), F32),
                            pltpu.VMEM((2, DEC_G, PAGE_COLS, HEAD_DIM), F32),
                            pltpu.VMEM((2, DEC_G, N_Q_HEADS, PAGE_SIZE), F32),
                            pltpu.SemaphoreType.DMA((3, 2)),
                            pltpu.VMEM((N_Q_HEADS, LANES), F32),
                            pltpu.VMEM((N_Q_HEADS, 2 * LANES), F32),
                            pltpu.VMEM((N_Q_HEADS, LANES), F32)]),
        compiler_params=_params("arbitrary"),
        name="fox_decode",
    )(page_table, q, k_new, v_new, lf_new, gate, sb_part, cache_k, cache_v, cache_lf_t, us)


def _out_proj_kernel(m_ref, w_ref, x_ref, y_ref):
    y_ref[...] = x_ref[...] + _dot(m_ref[...], w_ref[...])


def _out_proj(merged, w_tiles, x, tm):
    rows = x.shape[0]
    nt, _, tn = w_tiles.shape
    return pl.pallas_call(
        _out_proj_kernel,
        out_shape=jax.ShapeDtypeStruct((rows, D_MODEL), F32),
        grid=(rows // tm, nt),
        in_specs=[pl.BlockSpec((tm, W_Q), lambda i, j: (i, 0)),
                  pl.BlockSpec((None, W_Q, tn), lambda i, j: (j, 0, 0)),
                  pl.BlockSpec((tm, tn), lambda i, j: (i, j))],
        out_specs=pl.BlockSpec((tm, tn), lambda i, j: (i, j)),
        compiler_params=_params("parallel", "arbitrary"),
        name="out_proj",
    )(merged, w_tiles, x)


def _ffn_kernel(y_ref, g_ref, wu_ref, wd_ref, o_ref, h_ref):
    @pl.when(pl.program_id(1) == 0)
    def _():
        y = y_ref[...]
        h_ref[...] = _rms(y, g_ref[...]).astype(BF16)
        o_ref[...] = y

    u = jnp.square(jnp.maximum(_dot(h_ref[...], wu_ref[...]), 0.0))
    o_ref[...] += _dot(u.astype(BF16), wd_ref[...])


def _ffn(y, g, w_up_tiles, w_down, tm):
    rows = y.shape[0]
    nk, _, tf = w_up_tiles.shape
    return pl.pallas_call(
        _ffn_kernel,
        out_shape=jax.ShapeDtypeStruct((rows, D_MODEL), F32),
        grid=(rows // tm, nk),
        in_specs=[pl.BlockSpec((tm, D_MODEL), lambda i, k: (i, 0), pipeline_mode=pl.Buffered(1)),
                  pl.BlockSpec((1, D_MODEL), lambda i, k: (0, 0)),
                  pl.BlockSpec((None, D_MODEL, tf), lambda i, k: (k, 0, 0)),
                  pl.BlockSpec((tf, D_MODEL), lambda i, k: (k, 0))],
        out_specs=pl.BlockSpec((tm, D_MODEL), lambda i, k: (i, 0)),
        scratch_shapes=[pltpu.VMEM((tm, D_MODEL), BF16)],
        compiler_params=pltpu.CompilerParams(dimension_semantics=("parallel", "arbitrary"),
                                             vmem_limit_bytes=FFN_VMEM_LIMIT),
        name="ffn",
    )(y, g, w_up_tiles, w_down)


def _row_tile(rows, preferred):
    return preferred if rows % preferred == 0 else rows


def _project_rows(x, wts):
    rows = x.shape[0]
    tm = _row_tile(rows, TM_PROJ)
    h, lf = _rms_forget(x, wts["g_mix"], wts["w_fl_hi"], wts["w_fl_lo"], wts["b_f"], _row_tile(rows, TM_MAIN // 2))
    sq = _proj(h, wts["w_sq"], wts["g_q"], "scale", tm, "proj_sq")
    fq = _proj(h, wts["w_fq"], wts["g_q"], "norm_scale", tm, "proj_fq")
    gates = _proj(h, wts["w_gate"], wts["g_q"], "sigmoid", tm, "proj_gates")
    kv = _kv_proj(h, wts["w_kv"], wts["g_k"], _row_tile(rows, TM_MAIN))
    return sq, fq, gates, kv, lf


def _finish_rows(x, merged, wts):
    rows = x.shape[0]
    y1 = _out_proj(merged, wts["w_out"], x, _row_tile(rows, TM_PROJ))
    return _ffn(y1, wts["g_ffn"], wts["w_up"], wts["w_down"], _row_tile(rows, TM_MAIN))


def kernel(x_prompt, x_sample, cache_sb_k, cache_sb_v, cache_fox_k, cache_fox_v, cache_fox_logf,
           page_table, meta_tokens, g_mix, w_in, b_forget, g_q, g_k, w_out, g_ffn, w_up, w_down):
    batch, seq, _ = x_prompt.shape
    dec = x_sample.shape[0]
    assert w_in.shape[0] == 1 and x_sample.shape[1] == 1 and N_META + dec <= AUX_ROWS
    pool = cache_sb_k.shape[1]

    w = w_in[0]
    w_fl = jnp.pad(w[:, FORGET_OFFSET:FORGET_OFFSET + N_Q_HEADS], ((0, 0), (0, LANES - N_Q_HEADS)))
    w_fl_hi = w_fl.astype(BF16)
    wts = {
        "g_mix": g_mix, "g_q": g_q, "g_k": g_k, "g_ffn": g_ffn, "b_f": b_forget,
        "w_sq": _col_tiles(w[:, :W_Q].astype(BF16), TN),
        "w_fq": _col_tiles(w[:, W_Q + 2 * W_KV:2 * W_Q + 2 * W_KV].astype(BF16), TN),
        "w_kv": _col_tiles(jnp.concatenate([w[:, W_Q:W_Q + 2 * W_KV], w[:, 2 * W_Q + 2 * W_KV:FORGET_OFFSET]],
                                           axis=1).astype(BF16), W_KV),
        "w_gate": _col_tiles(w[:, FORGET_OFFSET + N_Q_HEADS:].astype(BF16), TN),
        "w_fl_hi": w_fl_hi, "w_fl_lo": (w_fl - w_fl_hi.astype(F32)).astype(BF16),
        "w_out": _col_tiles(w_out[0].astype(BF16), TN),
        "w_up": _col_tiles(w_up[0].astype(BF16), TN), "w_down": w_down[0].astype(BF16),
    }

    x_main = x_prompt.reshape(batch * seq, D_MODEL)
    x_aux = jnp.concatenate([meta_tokens, x_sample.reshape(dec, D_MODEL),
                             jnp.zeros((AUX_ROWS - N_META - dec, D_MODEL), F32)], axis=0)

    sq_m, fq_m, gates_m, kv_m, lf_m = _project_rows(x_main, wts)
    sq_a, fq_a, gates_a, kv_a, lf_a = _project_rows(x_aux, wts)
    sk_m, sv_m, fk_m, fv_m, skb_m, svb_m, fkb_m, fvb_m = kv_m
    sk_a, sv_a, fk_a, fv_a, skb_a, svb_a, fkb_a, fvb_a = kv_a

    lf_meta_t = jnp.pad(lf_a[:N_META].T, ((0, 0), (0, LANES - N_META)))
    lf_main_t = jnp.swapaxes(lf_m.reshape(batch, seq, N_Q_HEADS), 1, 2)
    ck_meta, ck_main = _cum_forget(lf_meta_t, lf_main_t)
    ck_meta = ck_meta.reshape(batch, N_KV_HEADS, GROUP, LANES)
    ck_main = ck_main.reshape(batch, N_KV_HEADS, GROUP, seq)

    pad_keys = lambda a: jnp.pad(a[:N_META], ((0, LANES - N_META), (0, 0)))
    mk_sb, mv_sb, mk_fx, mv_fx = pad_keys(skb_a), pad_keys(svb_a), pad_keys(fkb_a), pad_keys(fvb_a)

    j_idx = lax.broadcasted_iota(jnp.int32, (LANES, LANES), 0)
    s_idx = lax.broadcasted_iota(jnp.int32, (LANES, LANES), 1)
    ones = jnp.ones((LANES, LANES), BF16)
    uo = jnp.concatenate([(j_idx >= s_idx).astype(BF16), ones], axis=1)
    uo = jnp.concatenate([uo] * 2, axis=0)
    c_src = lax.broadcasted_iota(jnp.int32, (PAGE_COLS, PAGE_COLS), 0) // N_KV_HEADS
    c_dst = lax.broadcasted_iota(jnp.int32, (PAGE_COLS, PAGE_COLS), 1) // N_KV_HEADS
    uo_page = jnp.concatenate([(c_src >= c_dst).astype(BF16), jnp.ones((PAGE_COLS, LANES), BF16)], axis=1)
    uo_page = jnp.concatenate([uo_page] * 2, axis=0)
    j_key = lax.broadcasted_iota(jnp.int32, (PAGE_SIZE, PAGE_COLS), 0)
    c_key = lax.broadcasted_iota(jnp.int32, (PAGE_SIZE, PAGE_COLS), 1) // N_KV_HEADS
    us_page = jnp.concatenate([(j_key > c_key).astype(BF16), ones], axis=1)
    us_page = jnp.concatenate([us_page] * 3, axis=0)

    sb_m = _sb_prompt(sq_m, skb_m, svb_m, mk_sb, mv_sb, uo, gates_m, batch, seq)
    merged_m = _fox_prompt(fq_m, fkb_m, fvb_m, ck_main, mk_fx, mv_fx, ck_meta, gates_m, sb_m, batch, seq)
    merged_meta = _meta_attn(sq_a[:N_META], mk_sb, mv_sb, fq_a[:N_META], mk_fx, mv_fx,
                             ck_meta[0], uo, gates_a[:N_META])

    heads = lambda a: a[N_META:N_META + dec].reshape(dec, N_Q_HEADS, HEAD_DIM)
    kv_heads = lambda a: jnp.repeat(a[N_META:N_META + dec].reshape(dec, N_KV_HEADS, HEAD_DIM), GROUP, axis=1)
    pages = lambda c: c[0].reshape(pool, PAGE_COLS, HEAD_DIM)
    sb_dec = _sb_decode(page_table, heads(sq_a), heads(gates_a[:, :W_Q]),
                        pages(cache_sb_k), pages(cache_sb_v), uo_page)
    merged_dec = _fox_decode(page_table, heads(fq_a), kv_heads(fkb_a), kv_heads(fvb_a),
                             lf_a[N_META:N_META + dec].reshape(dec, N_Q_HEADS, 1),
                             heads(gates_a[:, W_Q:]), sb_dec,
                             pages(cache_fox_k), pages(cache_fox_v),
                             jnp.swapaxes(cache_fox_logf[0], 1, 2), us_page)

    tail = jnp.zeros((AUX_ROWS - N_META - dec, W_Q), BF16)
    merged_a = jnp.concatenate([merged_meta, merged_dec.reshape(dec, W_Q), tail], axis=0)

    y_main = _finish_rows(x_main, merged_m, wts)
    y_aux = _finish_rows(x_aux, merged_a, wts)

    def prompt_state(a_aux, a_main, tail_shape):
        meta = jnp.broadcast_to(a_aux[None, :N_META], (batch, N_META, a_aux.shape[1]))
        full = jnp.concatenate([meta, a_main.reshape(batch, seq, a_aux.shape[1])], axis=1)
        return full.reshape((1, batch, seq + N_META) + tail_shape)

    kv_shape = (N_KV_HEADS, HEAD_DIM)
    sample_state = lambda a, tail_shape: a[N_META:N_META + dec].reshape((1, dec, 1) + tail_shape)
    return (y_main.reshape(batch, seq, D_MODEL), y_aux[N_META:N_META + dec].reshape(dec, 1, D_MODEL),
            prompt_state(sk_a, sk_m, kv_shape), prompt_state(sv_a, sv_m, kv_shape),
            prompt_state(fk_a, fk_m, kv_shape), prompt_state(fv_a, fv_m, kv_shape),
            prompt_state(lf_a, lf_m, (N_Q_HEADS,)),
            sample_state(sk_a, kv_shape), sample_state(sv_a, kv_shape),
            sample_state(fk_a, kv_shape), sample_state(fv_a, kv_shape),
            sample_state(lf_a, (N_Q_HEADS,)))
```

```python
import functools

import jax
import jax.numpy as jnp
from jax import lax
from jax.experimental import pallas as pl
from jax.experimental.pallas import tpu as pltpu

D_MODEL = 2048
HEAD_DIM = 128
N_Q_HEADS = 16
N_KV_HEADS = 4
GROUP = 4
W_Q = N_Q_HEADS * HEAD_DIM
W_KV = N_KV_HEADS * HEAD_DIM
D_FF = 4 * D_MODEL
N_META = 16
PAGE_SIZE = 128
EPS = 1e-6
LOG2E = 1.4426950408889634
Q_SCALE = HEAD_DIM ** -0.5 * LOG2E
FORGET_OFFSET = 2 * W_Q + 4 * W_KV
AUX_ROWS = 32

BF16 = jnp.bfloat16
F32 = jnp.float32

LANES = 128
SUBLANES = 8
NEG_BIG = -1e30
SB_DEAD = 104.0 * LOG2E
VMEM_LIMIT = 48 * 1024 * 1024
FFN_VMEM_LIMIT = 58 * 1024 * 1024
TM_MAIN = 1024
TM_PROJ = 2048
TN = 512

_NT = (((1,), (1,)), ((), ()))


def _params(*sem):
    return pltpu.CompilerParams(dimension_semantics=sem, vmem_limit_bytes=VMEM_LIMIT)


def _rms(x, g):
    ms = jnp.mean(x * x, axis=-1, keepdims=True)
    return x * lax.rsqrt(ms + EPS) * g


def _softplus(z):
    return jnp.maximum(z, 0.0) + jnp.log1p(jnp.exp(-jnp.abs(z)))


def _softplus2(z2):
    return jnp.maximum(z2, 0.0) + jnp.log2(1.0 + jnp.exp2(-jnp.abs(z2)))


def _split2(x):
    hi = x.astype(BF16)
    lo = (x - hi.astype(F32)).astype(BF16)
    return hi, lo


def _split3(x):
    hi = x.astype(BF16)
    r = x - hi.astype(F32)
    mid = r.astype(BF16)
    lo = (r - mid.astype(F32)).astype(BF16)
    return hi, mid, lo


def _dot(a, b):
    return jnp.dot(a, b, preferred_element_type=F32)


def _dot_t(a, bt):
    return lax.dot_general(a, bt, _NT, preferred_element_type=F32)


def _split_dot(parts, m_stacked):
    return _dot(jnp.concatenate(parts, axis=1), m_stacked)


def _rms_forget_kernel(x_ref, g_ref, whi_ref, wlo_ref, b_ref, h_ref, lf_ref):
    h = _rms(x_ref[...], g_ref[...])
    h_ref[...] = h.astype(h_ref.dtype)
    hh, hl = _split2(h)
    z = _dot_t(hh, whi_ref[...]) + _dot_t(hl, whi_ref[...]) + _dot_t(hh, wlo_ref[...])
    z = z[:, :N_Q_HEADS] + b_ref[...]
    lf_ref[...] = -_softplus(-z)


def _rms_forget(x, g, w_hi, w_lo, b, tm):
    rows = x.shape[0]
    const = lambda shape: pl.BlockSpec(shape, lambda i: (0, 0))
    return pl.pallas_call(
        _rms_forget_kernel,
        out_shape=[jax.ShapeDtypeStruct((rows, D_MODEL), BF16),
                   jax.ShapeDtypeStruct((rows, N_Q_HEADS), F32)],
        grid=(rows // tm,),
        in_specs=[pl.BlockSpec((tm, D_MODEL), lambda i: (i, 0)),
                  const((1, D_MODEL)), const((LANES, D_MODEL)), const((LANES, D_MODEL)),
                  const((1, N_Q_HEADS))],
        out_specs=[pl.BlockSpec((tm, D_MODEL), lambda i: (i, 0)),
                   pl.BlockSpec((tm, N_Q_HEADS), lambda i: (i, 0))],
        compiler_params=_params("parallel"),
        name="rms_forget",
    )(x, g, w_hi, w_lo, b)


def _head_norm(acc, g):
    outs = []
    for hh in range(acc.shape[1] // HEAD_DIM):
        xh = acc[:, hh * HEAD_DIM:(hh + 1) * HEAD_DIM]
        outs.append(_rms(xh, g))
    return jnp.concatenate(outs, axis=1)


def _proj_kernel(h_ref, wt_ref, g_ref, o_ref, *, mode):
    acc = _dot_t(h_ref[...], wt_ref[...].astype(BF16))
    if mode == "scale":
        acc = acc * Q_SCALE
    elif mode == "norm_scale":
        acc = _head_norm(acc, g_ref[...]) * Q_SCALE
    elif mode == "sigmoid":
        acc = jax.nn.sigmoid(acc)
    o_ref[...] = acc.astype(o_ref.dtype)


def _wt_rows_spec(row0):
    if row0 % TN == 0:
        return pl.BlockSpec((TN, D_MODEL), lambda i, j: (row0 // TN + j, 0))
    assert row0 % SUBLANES == 0
    return pl.BlockSpec((pl.Element(TN), pl.Element(D_MODEL)),
                        lambda i, j: (pl.multiple_of(row0 + j * TN, SUBLANES), 0))


def _proj(h, wt, row0, n, g, mode, tm, name):
    rows = h.shape[0]
    return pl.pallas_call(
        functools.partial(_proj_kernel, mode=mode),
        out_shape=jax.ShapeDtypeStruct((rows, n), BF16),
        grid=(rows // tm, n // TN),
        in_specs=[pl.BlockSpec((tm, D_MODEL), lambda i, j: (i, 0)),
                  _wt_rows_spec(row0),
                  pl.BlockSpec((1, HEAD_DIM), lambda i, j: (0, 0))],
        out_specs=pl.BlockSpec((tm, TN), lambda i, j: (i, j)),
        compiler_params=_params("parallel", "arbitrary"),
        name=name,
    )(h, wt, g)


def _kv_kernel(h_ref, wt_ref, gk_ref, *out_refs):
    j = pl.program_id(1)
    acc = _dot_t(h_ref[...], wt_ref[...].astype(BF16))
    for jj in range(4):
        @pl.when(j == jj)
        def _(jj=jj):
            val = _head_norm(acc, gk_ref[...]) if jj == 2 else acc
            out_refs[jj][...] = val
            out_refs[4 + jj][...] = val.astype(BF16)


def _kv_proj(h, wt, g_k, tm):
    rows = h.shape[0]
    blk = pl.BlockSpec((tm, W_KV), lambda i, j: (i, 0))
    sb_kv, fox_kv = W_Q // W_KV, (2 * W_Q + 2 * W_KV) // W_KV
    return pl.pallas_call(
        _kv_kernel,
        out_shape=[jax.ShapeDtypeStruct((rows, W_KV), F32)] * 4
        + [jax.ShapeDtypeStruct((rows, W_KV), BF16)] * 4,
        grid=(rows // tm, 4),
        in_specs=[pl.BlockSpec((tm, D_MODEL), lambda i, j: (i, 0)),
                  pl.BlockSpec((W_KV, D_MODEL), lambda i, j: (sb_kv + j + (fox_kv - sb_kv - 2) * (j // 2), 0)),
                  pl.BlockSpec((1, HEAD_DIM), lambda i, j: (0, 0))],
        out_specs=[blk] * 8,
        compiler_params=_params("parallel", "arbitrary"),
        name="kv_proj",
    )(h, wt, g_k)


def _lane_cumsum(x):
    n = x.shape[-1]
    lane = lax.broadcasted_iota(jnp.int32, x.shape, x.ndim - 1)
    k = 1
    while k < n:
        x = x + jnp.where(lane >= k, pltpu.roll(x, k, axis=x.ndim - 1), 0.0)
        k *= 2
    return x


def _cum_kernel(meta_ref, main_ref, cmeta_ref, cmain_ref):
    cm = _lane_cumsum(meta_ref[...])
    cmeta_ref[...] = cm * LOG2E
    cmain_ref[...] = (_lane_cumsum(main_ref[...]) + cm[:, LANES - 1:LANES]) * LOG2E


def _cum_forget(lf_meta_t, lf_main_t):
    b, _, s = lf_main_t.shape
    return pl.pallas_call(
        _cum_kernel,
        out_shape=[jax.ShapeDtypeStruct((b, N_Q_HEADS, LANES), F32),
                   jax.ShapeDtypeStruct((b, N_Q_HEADS, s), F32)],
        grid=(b,),
        in_specs=[pl.BlockSpec((N_Q_HEADS, LANES), lambda i: (0, 0)),
                  pl.BlockSpec((None, N_Q_HEADS, s), lambda i: (i, 0, 0))],
        out_specs=[pl.BlockSpec((None, N_Q_HEADS, LANES), lambda i: (i, 0, 0)),
                   pl.BlockSpec((None, N_Q_HEADS, s), lambda i: (i, 0, 0))],
        compiler_params=_params("parallel"),
        name="cum_forget",
    )(lf_meta_t, lf_main_t)


def _stack_heads(q):
    return jnp.concatenate([q[:, g * HEAD_DIM:(g + 1) * HEAD_DIM] for g in range(GROUP)], axis=0)


SB_CHUNK = 512
FOX_CHUNK = 256


def _chunk_visible(rel, c, cr, t, tk):
    col = lax.broadcasted_iota(jnp.int32, (cr, tk), 1)
    row = (lax.broadcasted_iota(jnp.int32, (cr, tk), 0) + c * cr) % t
    return rel(col, row)


def _sb_block(qs, kb, vb, uo, rel, r_ref, acc_ref):
    r, tk = qs.shape[0], kb.shape[0]
    t = r // GROUP
    cr = min(r, SB_CHUNK)
    score = lambda c: lax.dot_general(qs[c * cr:(c + 1) * cr], kb, _NT, preferred_element_type=F32)
    z_next = score(0)
    for c in range(r // cr):
        rows = slice(c * cr, (c + 1) * cr)
        z = z_next
        if c + 1 < r // cr:
            z_next = score(c + 1)
        sp = _softplus2(z)
        if rel is not None:
            vis = _chunk_visible(rel, c, cr, t, tk)
            sp = jnp.where(vis, sp, 0.0)
        ct = _split_dot(_split2(sp), uo)
        r_prev = r_ref[rows]
        a = jnp.exp2(z - ct[:, :tk] - r_prev)
        if rel is not None:
            a = jnp.where(vis, a, 0.0)
        acc_ref[rows] += _dot(a.astype(BF16), vb)
        r_ref[rows] = r_prev + ct[:, tk:]


def _fox_block(qs, kb, vb, ckb, rel, m_ref, acc_ref):
    r, tk = qs.shape[0], kb.shape[0]
    t = r // GROUP
    cr = min(t, FOX_CHUNK)
    v1 = jnp.concatenate([vb, jnp.ones((tk, LANES), BF16)], axis=1)
    score = lambda c: lax.dot_general(qs[c * cr:(c + 1) * cr], kb, _NT, preferred_element_type=F32)
    z_next = score(0)
    for c in range(r // cr):
        rows = slice(c * cr, (c + 1) * cr)
        g = c * cr // t
        z = z_next - ckb[g:g + 1, :]
        if c + 1 < r // cr:
            z_next = score(c + 1)
        if rel is not None:
            z = jnp.where(_chunk_visible(rel, c, cr, t, tk), z, NEG_BIG)
        m_prev = m_ref[rows]
        m_new = jnp.maximum(m_prev, jnp.max(z, axis=1, keepdims=True))
        alpha = jnp.exp2(m_prev - m_new)
        p = jnp.exp2(z - jnp.tile(m_new, (1, tk // LANES)))
        acc_ref[rows] = acc_ref[rows] * jnp.tile(alpha, (1, 2)) + _dot(p.astype(BF16), v1)
        m_ref[rows] = m_new


SB_T = 128


def _store_heads(o_ref, o, t, gate_ref, add_ref=None):
    for g in range(GROUP):
        cols = slice(g * HEAD_DIM, (g + 1) * HEAD_DIM)
        val = o[g * t:(g + 1) * t] * gate_ref[:, cols].astype(F32)
        if add_ref is not None:
            val = val + add_ref[:, cols].astype(F32)
        o_ref[:, cols] = val.astype(o_ref.dtype)


def _sb_prompt_kernel(q_ref, k_ref, v_ref, mk_ref, mv_ref, uo_ref, gate_ref, o_ref, r_ref, acc_ref):
    i = pl.program_id(2)
    t = SB_T
    qs = _stack_heads(q_ref[...])
    uo = uo_ref[...]

    def visit(kb, vb, rel):
        _sb_block(qs, kb, vb, uo, rel, r_ref, acc_ref)
        return jnp.min(r_ref[...])

    r_ref[...] = jnp.zeros_like(r_ref)
    acc_ref[...] = jnp.zeros_like(acc_ref)
    start = pl.multiple_of(i * t, t)
    mn = visit(k_ref[pl.ds(start, t), :], v_ref[pl.ds(start, t), :], lambda c, r: c < r)

    def cond(c):
        return jnp.logical_and(c[0] >= 0, c[1] < SB_DEAD)

    def body(c):
        s = pl.multiple_of(c[0] * t, t)
        return c[0] - 1, visit(k_ref[pl.ds(s, t), :], v_ref[pl.ds(s, t), :], None)

    _, mn = lax.while_loop(cond, body, (i - 1, mn))

    @pl.when(mn < SB_DEAD)
    def _():
        visit(mk_ref[...], mv_ref[...], lambda c, r: c < N_META)

    _store_heads(o_ref, acc_ref[...], t, gate_ref)


def _sb_prompt(q, k, v, mk, mv, uo, gates, batch, seq):
    t = SB_T
    nb = seq // t
    tile = pl.BlockSpec((t, GROUP * HEAD_DIM), lambda b, n, i: (b * nb + i, n))
    return pl.pallas_call(
        _sb_prompt_kernel,
        out_shape=jax.ShapeDtypeStruct((batch * seq, W_Q), BF16),
        grid=(batch, N_KV_HEADS, nb),
        in_specs=[tile,
                  pl.BlockSpec((seq, HEAD_DIM), lambda b, n, i: (b, n)),
                  pl.BlockSpec((seq, HEAD_DIM), lambda b, n, i: (b, n)),
                  pl.BlockSpec((LANES, HEAD_DIM), lambda b, n, i: (0, n)),
                  pl.BlockSpec((LANES, HEAD_DIM), lambda b, n, i: (0, n)),
                  pl.BlockSpec(uo.shape, lambda b, n, i: (0, 0)),
                  tile],
        out_specs=tile,
        scratch_shapes=[pltpu.VMEM((GROUP * t, t), F32), pltpu.VMEM((GROUP * t, HEAD_DIM), F32)],
        compiler_params=_params("parallel", "parallel", "arbitrary"),
        name="sb_prompt",
    )(q, k, v, mk, mv, uo, gates)


FOX_T = 256


def _fox_prompt_kernel(q_ref, k_ref, v_ref, ck_ref, mk_ref, mv_ref, mck_ref, gate_ref, sb_ref,
                       o_ref, m_ref, acc_ref):
    i = pl.program_id(2)
    t = FOX_T
    qs = _stack_heads(q_ref[...])
    m_ref[...] = jnp.full_like(m_ref, NEG_BIG)
    acc_ref[...] = jnp.zeros_like(acc_ref)

    def visit(s, width, rel):
        _fox_block(qs, k_ref[pl.ds(s, width), :], v_ref[pl.ds(s, width), :], ck_ref[:, pl.ds(s, width)],
                   rel, m_ref, acc_ref)

    visit(pl.multiple_of(i * t, t), t, lambda c, r: c <= r)

    def body(j, carry):
        visit(pl.multiple_of(j * 2 * t, 2 * t), 2 * t, None)
        return carry

    lax.fori_loop(0, i // 2, body, 0)

    @pl.when(i % 2 == 1)
    def _():
        visit(pl.multiple_of((i - 1) * t, t), t, None)

    _fox_block(qs, mk_ref[...], mv_ref[...], mck_ref[...],
               lambda c, r: c < N_META, m_ref, acc_ref)

    acc = acc_ref[...]
    _store_heads(o_ref, acc[:, :HEAD_DIM] / acc[:, HEAD_DIM:], t, gate_ref, sb_ref)


def _fox_prompt(q, k, v, ck, mk, mv, mck, gates, sb_part, batch, seq):
    t = FOX_T
    nb = seq // t
    tile = pl.BlockSpec((t, GROUP * HEAD_DIM), lambda b, n, i: (b * nb + i, n))
    return pl.pallas_call(
        _fox_prompt_kernel,
        out_shape=jax.ShapeDtypeStruct((batch * seq, W_Q), BF16),
        grid=(batch, N_KV_HEADS, nb),
        in_specs=[tile,
                  pl.BlockSpec((seq, HEAD_DIM), lambda b, n, i: (b, n)),
                  pl.BlockSpec((seq, HEAD_DIM), lambda b, n, i: (b, n)),
                  pl.BlockSpec((None, None, GROUP, seq), lambda b, n, i: (b, n, 0, 0)),
                  pl.BlockSpec((LANES, HEAD_DIM), lambda b, n, i: (0, n)),
                  pl.BlockSpec((LANES, HEAD_DIM), lambda b, n, i: (0, n)),
                  pl.BlockSpec((None, None, GROUP, LANES), lambda b, n, i: (b, n, 0, 0)),
                  pl.BlockSpec((t, GROUP * HEAD_DIM), lambda b, n, i: (b * nb + i, N_KV_HEADS + n)),
                  tile],
        out_specs=tile,
        scratch_shapes=[pltpu.VMEM((GROUP * t, LANES), F32), pltpu.VMEM((GROUP * t, 2 * LANES), F32)],
        compiler_params=_params("parallel", "parallel", "arbitrary"),
        name="fox_prompt",
    )(q, k, v, ck, mk, mv, mck, gates, sb_part)


def _meta_attn_kernel(sq_ref, sk_ref, sv_ref, fq_ref, fk_ref, fv_ref, ck_ref, uo_ref, gsb_ref, gfx_ref,
                      o_ref, m_ref, acc_ref, r_ref, sacc_ref):
    t = N_META
    valid = lambda c: c < N_META
    r_ref[...] = jnp.zeros_like(r_ref)
    sacc_ref[...] = jnp.zeros_like(sacc_ref)
    _sb_block(_stack_heads(sq_ref[...]), sk_ref[...], sv_ref[...], uo_ref[...],
              lambda c, r: jnp.logical_and(c < r, valid(c)), r_ref, sacc_ref)
    o = sacc_ref[...]
    m_ref[...] = jnp.full_like(m_ref, NEG_BIG)
    acc_ref[...] = jnp.zeros_like(acc_ref)
    _fox_block(_stack_heads(fq_ref[...]), fk_ref[...], fv_ref[...], ck_ref[...],
               lambda c, r: jnp.logical_and(c <= r, valid(c)), m_ref, acc_ref)
    acc = acc_ref[...]
    f = acc[:, :HEAD_DIM] / acc[:, HEAD_DIM:]
    for g in range(GROUP):
        cols = slice(g * HEAD_DIM, (g + 1) * HEAD_DIM)
        merged = (gsb_ref[:, cols].astype(F32) * o[g * t:(g + 1) * t]
                  + gfx_ref[:, cols].astype(F32) * f[g * t:(g + 1) * t])
        o_ref[:, cols] = merged.astype(o_ref.dtype)


def _meta_attn(sq, sk, sv, fq, fk, fv, ck, uo, gates):
    qspec = pl.BlockSpec((N_META, GROUP * HEAD_DIM), lambda n: (0, n))
    kspec = pl.BlockSpec((LANES, HEAD_DIM), lambda n: (0, n))
    return pl.pallas_call(
        _meta_attn_kernel,
        out_shape=jax.ShapeDtypeStruct((N_META, W_Q), BF16),
        grid=(N_KV_HEADS,),
        in_specs=[qspec, kspec, kspec, qspec, kspec, kspec,
                  pl.BlockSpec((None, GROUP, LANES), lambda n: (n, 0, 0)),
                  pl.BlockSpec(uo.shape, lambda n: (0, 0)),
                  qspec,
                  pl.BlockSpec((N_META, GROUP * HEAD_DIM), lambda n: (0, N_KV_HEADS + n))],
        out_specs=qspec,
        scratch_shapes=[pltpu.VMEM((GROUP * N_META, LANES), F32),
                        pltpu.VMEM((GROUP * N_META, 2 * LANES), F32),
                        pltpu.VMEM((GROUP * N_META, LANES), F32),
                        pltpu.VMEM((GROUP * N_META, HEAD_DIM), F32)],
        compiler_params=_params("parallel"),
        name="meta_attn",
    )(sq, sk, sv, fq, fk, fv, ck, uo, gates, gates)


PAGE_COLS = PAGE_SIZE * N_KV_HEADS
DEC_G = 8


def _own_head(shape):
    row = lax.broadcasted_iota(jnp.int32, shape, 0)
    col = lax.broadcasted_iota(jnp.int32, shape, 1)
    return (col % N_KV_HEADS) == (row // GROUP)


def _sb_decode_kernel(pt_ref, q_ref, gate_ref, ck_hbm, cv_hbm, uo_ref, o_ref, kbuf, vbuf, sem, r_ref, acc_ref):
    b = pl.program_id(0)
    n_pages = pt_ref.shape[1]
    q = q_ref[...]
    uo = uo_ref[...]
    own = _own_head((N_Q_HEADS, PAGE_COLS))

    def copies(p, slot):
        pid = pt_ref[b, p]
        return [pltpu.make_async_copy(ck_hbm.at[pid], kbuf.at[slot], sem.at[0, slot]),
                pltpu.make_async_copy(cv_hbm.at[pid], vbuf.at[slot], sem.at[1, slot])]

    r_ref[...] = jnp.zeros_like(r_ref)
    acc_ref[...] = jnp.zeros_like(acc_ref)
    for c in copies(n_pages - 1, (n_pages - 1) % 2):
        c.start()

    def cond(c):
        return jnp.logical_and(c[0] >= 0, c[1] < SB_DEAD)

    def body(c):
        p = c[0]
        slot = p % 2
        for cp in copies(p, slot):
            cp.wait()

        @pl.when(p > 0)
        def _():
            for cp in copies(p - 1, 1 - slot):
                cp.start()

        z = lax.dot_general(q, kbuf[slot].astype(BF16), _NT, preferred_element_type=F32)
        sp = jnp.where(own, _softplus2(z), 0.0)
        ct = _split_dot(_split2(sp), uo)
        r = r_ref[...]
        a = jnp.where(own, jnp.exp2(z - ct[:, :PAGE_COLS] - jnp.tile(r, (1, N_KV_HEADS))), 0.0)
        acc_ref[...] += _dot(a.astype(BF16), vbuf[slot].astype(BF16))
        r_new = r + ct[:, PAGE_COLS:]
        r_ref[...] = r_new
        return p - 1, jnp.min(r_new)

    p_end, _ = lax.while_loop(cond, body, (jnp.int32(n_pages - 1), jnp.float32(0.0)))

    @pl.when(p_end >= 0)
    def _():
        for cp in copies(p_end, p_end % 2):
            cp.wait()

    o_ref[...] = (acc_ref[...] * gate_ref[...].astype(F32)).astype(o_ref.dtype)


def _sb_decode(page_table, q, gate, cache_k, cache_v, uo):
    nb = q.shape[0]
    any_spec = pl.BlockSpec(memory_space=pl.ANY)
    head_spec = pl.BlockSpec((None, N_Q_HEADS, HEAD_DIM), lambda b, pt: (b, 0, 0))
    return pl.pallas_call(
        _sb_decode_kernel,
        out_shape=jax.ShapeDtypeStruct((nb, N_Q_HEADS, HEAD_DIM), BF16),
        grid_spec=pltpu.PrefetchScalarGridSpec(
            num_scalar_prefetch=1,
            grid=(nb,),
            in_specs=[head_spec, head_spec, any_spec, any_spec,
                      pl.BlockSpec(uo.shape, lambda b, pt: (0, 0))],
            out_specs=head_spec,
            scratch_shapes=[pltpu.VMEM((2, PAGE_COLS, HEAD_DIM), F32),
                            pltpu.VMEM((2, PAGE_COLS, HEAD_DIM), F32),
                            pltpu.SemaphoreType.DMA((2, 2)),
                            pltpu.VMEM((N_Q_HEADS, LANES), F32),
                            pltpu.VMEM((N_Q_HEADS, HEAD_DIM), F32)]),
        compiler_params=_params("arbitrary"),
        name="sb_decode",
    )(page_table, q, gate, cache_k, cache_v, uo)


def _fox_decode_kernel(pt_ref, q_ref, kn_ref, vn_ref, lfn_ref, gate_ref, sb_ref, ck_hbm, cv_hbm, lf_hbm,
                       us_ref, o_ref, kbuf, vbuf, lbuf, sem, m_ref, acc_ref, d_ref):
    n_seq, n_pages = pt_ref.shape
    n_grp = n_pages // DEC_G
    total = n_seq * n_grp
    us = us_ref[...]
    bias = jnp.where(_own_head((N_Q_HEADS, PAGE_COLS)), 0.0, NEG_BIG)

    def copies(t, slot):
        b = t // n_grp
        newest = n_pages - 1 - (t % n_grp) * DEC_G
        out = []
        for g in range(DEC_G):
            pid = pt_ref[b, newest - g]
            out += [pltpu.make_async_copy(ck_hbm.at[pid], kbuf.at[slot, g], sem.at[0, slot]),
                    pltpu.make_async_copy(cv_hbm.at[pid], vbuf.at[slot, g], sem.at[1, slot]),
                    pltpu.make_async_copy(lf_hbm.at[pid], lbuf.at[slot, g], sem.at[2, slot])]
        return out

    for cp in copies(0, 0):
        cp.start()

    def body(t, carry):
        slot = t % 2
        b = t // n_grp
        gi = t % n_grp

        @pl.when(t + 1 < total)
        def _():
            for cp in copies(t + 1, 1 - slot):
                cp.start()

        q = q_ref[b]

        @pl.when(gi == 0)
        def _():
            z_self = jnp.sum(q.astype(F32) * kn_ref[b].astype(F32), axis=1, keepdims=True)
            m_ref[...] = jnp.broadcast_to(z_self, m_ref.shape)
            acc_ref[...] = jnp.concatenate([vn_ref[b].astype(F32), jnp.ones((N_Q_HEADS, LANES), F32)], axis=1)
            d_ref[...] = jnp.broadcast_to(lfn_ref[b] * LOG2E, d_ref.shape)

        for cp in copies(t, slot):
            cp.wait()

        dt = _split_dot(_split3(lbuf[slot].reshape(DEC_G * N_Q_HEADS, PAGE_SIZE)), us) * LOG2E
        d = d_ref[...]
        zs = []
        for g in range(DEC_G):
            dg = dt[g * N_Q_HEADS:(g + 1) * N_Q_HEADS]
            z = lax.dot_general(q, kbuf[slot, g].astype(BF16), _NT, preferred_element_type=F32)
            zs.append(z + dg[:, :PAGE_COLS] + jnp.tile(d, (1, N_KV_HEADS)) + bias)
            d = d + dg[:, PAGE_COLS:]
        d_ref[...] = d
        z = jnp.concatenate(zs, axis=1)
        m_prev = m_ref[...]
        m_new = jnp.maximum(m_prev, jnp.max(z, axis=1, keepdims=True))
        alpha = jnp.exp2(m_prev - m_new)
        p = jnp.exp2(z - jnp.tile(m_new, (1, z.shape[1] // LANES)))
        lsum = jnp.sum(p, axis=1, keepdims=True)
        pb = p.astype(BF16)
        pv = _dot(pb[:, :PAGE_COLS], vbuf[slot, 0].astype(BF16))
        for g in range(1, DEC_G):
            pv += _dot(pb[:, g * PAGE_COLS:(g + 1) * PAGE_COLS], vbuf[slot, g].astype(BF16))
        acc = acc_ref[...] * jnp.tile(alpha, (1, 2)) + jnp.concatenate(
            [pv, jnp.broadcast_to(lsum, pv.shape)], axis=1)
        acc_ref[...] = acc
        m_ref[...] = m_new

        @pl.when(gi == n_grp - 1)
        def _():
            merged = acc[:, :HEAD_DIM] / acc[:, HEAD_DIM:] * gate_ref[b].astype(F32) + sb_ref[b].astype(F32)
            o_ref[b] = merged.astype(o_ref.dtype)

        return carry

    lax.fori_loop(0, total, body, 0)


def _fox_decode(page_table, q, k_new, v_new, lf_new, gate, sb_part, cache_k, cache_v, cache_lf_t, us):
    nb = q.shape[0]
    assert page_table.shape[1] % DEC_G == 0
    any_spec = pl.BlockSpec(memory_space=pl.ANY)
    head_spec = pl.BlockSpec((nb, N_Q_HEADS, HEAD_DIM), lambda i, pt: (0, 0, 0))
    return pl.pallas_call(
        _fox_decode_kernel,
        out_shape=jax.ShapeDtypeStruct((nb, N_Q_HEADS, HEAD_DIM), BF16),
        grid_spec=pltpu.PrefetchScalarGridSpec(
            num_scalar_prefetch=1,
            grid=(1,),
            in_specs=[head_spec, head_spec, head_spec,
                      pl.BlockSpec((nb, N_Q_HEADS, 1), lambda i, pt: (0, 0, 0)),
                      head_spec, head_spec,
                      any_spec, any_spec, any_spec,
                      pl.BlockSpec(us.shape, lambda i, pt: (0, 0))],
            out_specs=head_spec,
            scratch_shapes=[pltpu.VMEM((2, DEC_G, PAGE_COLS, HEAD_DIM), F32),
                            pltpu.VMEM((2, DEC_G, PAGE_COLS, HEAD_DIM), F32),
                            pltpu.VMEM((2, DEC_G, N_Q_HEADS, PAGE_SIZE), F32),
                            pltpu.SemaphoreType.DMA((3, 2)),
                            pltpu.VMEM((N_Q_HEADS, LANES), F32),
                            pltpu.VMEM((N_Q_HEADS, 2 * LANES), F32),
                            pltpu.VMEM((N_Q_HEADS, LANES), F32)]),
        compiler_params=_params("arbitrary"),
        name="fox_decode",
    )(page_table, q, k_new, v_new, lf_new, gate, sb_part, cache_k, cache_v, cache_lf_t, us)


def _out_proj_kernel(m_ref, w_ref, x_ref, y_ref):
    y_ref[...] = x_ref[...] + _dot(m_ref[...], w_ref[...])


def _out_proj(merged, w, x, tm):
    rows = x.shape[0]
    return pl.pallas_call(
        _out_proj_kernel,
        out_shape=jax.ShapeDtypeStruct((rows, D_MODEL), F32),
        grid=(rows // tm, D_MODEL // TN),
        in_specs=[pl.BlockSpec((tm, W_Q), lambda i, j: (i, 0)),
                  pl.BlockSpec((W_Q, TN), lambda i, j: (0, j)),
                  pl.BlockSpec((tm, TN), lambda i, j: (i, j))],
        out_specs=pl.BlockSpec((tm, TN), lambda i, j: (i, j)),
        compiler_params=_params("parallel", "arbitrary"),
        name="out_proj",
    )(merged, w, x)


def _ffn_kernel(y_ref, g_ref, wu_ref, wd_ref, o_ref, h_ref):
    @pl.when(pl.program_id(1) == 0)
    def _():
        y = y_ref[...]
        h_ref[...] = _rms(y, g_ref[...]).astype(BF16)
        o_ref[...] = y

    u = jnp.square(jnp.maximum(_dot(h_ref[...], wu_ref[...]), 0.0))
    o_ref[...] += _dot(u.astype(BF16), wd_ref[...])


def _ffn(y, g, w_up, w_down, tm):
    rows = y.shape[0]
    return pl.pallas_call(
        _ffn_kernel,
        out_shape=jax.ShapeDtypeStruct((rows, D_MODEL), F32),
        grid=(rows // tm, D_FF // TN),
        in_specs=[pl.BlockSpec((tm, D_MODEL), lambda i, k: (i, 0), pipeline_mode=pl.Buffered(1)),
                  pl.BlockSpec((1, D_MODEL), lambda i, k: (0, 0)),
                  pl.BlockSpec((D_MODEL, TN), lambda i, k: (0, k)),
                  pl.BlockSpec((TN, D_MODEL), lambda i, k: (k, 0))],
        out_specs=pl.BlockSpec((tm, D_MODEL), lambda i, k: (i, 0)),
        scratch_shapes=[pltpu.VMEM((tm, D_MODEL), BF16)],
        compiler_params=pltpu.CompilerParams(dimension_semantics=("parallel", "arbitrary"),
                                             vmem_limit_bytes=FFN_VMEM_LIMIT),
        name="ffn",
    )(y, g, w_up, w_down)


def _row_tile(rows, preferred):
    return preferred if rows % preferred == 0 else rows


def _project_rows(x, wts):
    rows = x.shape[0]
    tm = _row_tile(rows, TM_PROJ)
    h, lf = _rms_forget(x, wts["g_mix"], wts["w_fl_hi"], wts["w_fl_lo"], wts["b_f"], _row_tile(rows, TM_MAIN // 2))
    wt = wts["w_in_t"]
    sq = _proj(h, wt, 0, W_Q, wts["g_q"], "scale", tm, "proj_sq")
    fq = _proj(h, wt, W_Q + 2 * W_KV, W_Q, wts["g_q"], "norm_scale", tm, "proj_fq")
    gates = _proj(h, wt, FORGET_OFFSET + N_Q_HEADS, 2 * D_MODEL, wts["g_q"], "sigmoid", tm, "proj_gates")
    kv = _kv_proj(h, wt, wts["g_k"], _row_tile(rows, TM_MAIN))
    return sq, fq, gates, kv, lf


def _finish_rows(x, merged, wts):
    rows = x.shape[0]
    y1 = _out_proj(merged, wts["w_out"], x, _row_tile(rows, TM_PROJ))
    return _ffn(y1, wts["g_ffn"], wts["w_up"], wts["w_down"], _row_tile(rows, TM_MAIN))


def kernel(x_prompt, x_sample, cache_sb_k, cache_sb_v, cache_fox_k, cache_fox_v, cache_fox_logf,
           page_table, meta_tokens, g_mix, w_in, b_forget, g_q, g_k, w_out, g_ffn, w_up, w_down):
    batch, seq, _ = x_prompt.shape
    dec = x_sample.shape[0]
    assert w_in.shape[0] == 1 and x_sample.shape[1] == 1 and N_META + dec <= AUX_ROWS
    pool = cache_sb_k.shape[1]

    w_in_t = jnp.swapaxes(w_in[0], 0, 1)
    w_fl = jnp.pad(w_in_t[FORGET_OFFSET:FORGET_OFFSET + N_Q_HEADS], ((0, LANES - N_Q_HEADS), (0, 0)))
    w_fl_hi = w_fl.astype(BF16)
    wts = {
        "g_mix": g_mix, "g_q": g_q, "g_k": g_k, "g_ffn": g_ffn, "b_f": b_forget,
        "w_in_t": w_in_t,
        "w_fl_hi": w_fl_hi, "w_fl_lo": (w_fl - w_fl_hi.astype(F32)).astype(BF16),
        "w_out": w_out[0].astype(BF16), "w_up": w_up[0].astype(BF16), "w_down": w_down[0].astype(BF16),
    }

    x_main = x_prompt.reshape(batch * seq, D_MODEL)
    x_aux = jnp.concatenate([meta_tokens, x_sample.reshape(dec, D_MODEL),
                             jnp.zeros((AUX_ROWS - N_META - dec, D_MODEL), F32)], axis=0)

    sq_m, fq_m, gates_m, kv_m, lf_m = _project_rows(x_main, wts)
    sq_a, fq_a, gates_a, kv_a, lf_a = _project_rows(x_aux, wts)
    sk_m, sv_m, fk_m, fv_m, skb_m, svb_m, fkb_m, fvb_m = kv_m
    sk_a, sv_a, fk_a, fv_a, skb_a, svb_a, fkb_a, fvb_a = kv_a

    lf_meta_t = jnp.pad(lf_a[:N_META].T, ((0, 0), (0, LANES - N_META)))
    lf_main_t = jnp.swapaxes(lf_m.reshape(batch, seq, N_Q_HEADS), 1, 2)
    ck_meta, ck_main = _cum_forget(lf_meta_t, lf_main_t)
    ck_meta = ck_meta.reshape(batch, N_KV_HEADS, GROUP, LANES)
    ck_main = ck_main.reshape(batch, N_KV_HEADS, GROUP, seq)

    pad_keys = lambda a: jnp.pad(a[:N_META], ((0, LANES - N_META), (0, 0)))
    mk_sb, mv_sb, mk_fx, mv_fx = pad_keys(skb_a), pad_keys(svb_a), pad_keys(fkb_a), pad_keys(fvb_a)

    j_idx = lax.broadcasted_iota(jnp.int32, (LANES, LANES), 0)
    s_idx = lax.broadcasted_iota(jnp.int32, (LANES, LANES), 1)
    ones = jnp.ones((LANES, LANES), BF16)
    uo = jnp.concatenate([(j_idx >= s_idx).astype(BF16), ones], axis=1)
    uo = jnp.concatenate([uo] * 2, axis=0)
    c_src = lax.broadcasted_iota(jnp.int32, (PAGE_COLS, PAGE_COLS), 0) // N_KV_HEADS
    c_dst = lax.broadcasted_iota(jnp.int32, (PAGE_COLS, PAGE_COLS), 1) // N_KV_HEADS
    uo_page = jnp.concatenate([(c_src >= c_dst).astype(BF16), jnp.ones((PAGE_COLS, LANES), BF16)], axis=1)
    uo_page = jnp.concatenate([uo_page] * 2, axis=0)
    j_key = lax.broadcasted_iota(jnp.int32, (PAGE_SIZE, PAGE_COLS), 0)
    c_key = lax.broadcasted_iota(jnp.int32, (PAGE_SIZE, PAGE_COLS), 1) // N_KV_HEADS
    us_page = jnp.concatenate([(j_key > c_key).astype(BF16), ones], axis=1)
    us_page = jnp.concatenate([us_page] * 3, axis=0)

    sb_m = _sb_prompt(sq_m, skb_m, svb_m, mk_sb, mv_sb, uo, gates_m, batch, seq)
    merged_m = _fox_prompt(fq_m, fkb_m, fvb_m, ck_main, mk_fx, mv_fx, ck_meta, gates_m, sb_m, batch, seq)
    merged_meta = _meta_attn(sq_a[:N_META], mk_sb, mv_sb, fq_a[:N_META], mk_fx, mv_fx,
                             ck_meta[0], uo, gates_a[:N_META])

    heads = lambda a: a[N_META:N_META + dec].reshape(dec, N_Q_HEADS, HEAD_DIM)
    kv_heads = lambda a: jnp.repeat(a[N_META:N_META + dec].reshape(dec, N_KV_HEADS, HEAD_DIM), GROUP, axis=1)
    pages = lambda c: c[0].reshape(pool, PAGE_COLS, HEAD_DIM)
    sb_dec = _sb_decode(page_table, heads(sq_a), heads(gates_a[:, :W_Q]),
                        pages(cache_sb_k), pages(cache_sb_v), uo_page)
    merged_dec = _fox_decode(page_table, heads(fq_a), kv_heads(fkb_a), kv_heads(fvb_a),
                             lf_a[N_META:N_META + dec].reshape(dec, N_Q_HEADS, 1),
                             heads(gates_a[:, W_Q:]), sb_dec,
                             pages(cache_fox_k), pages(cache_fox_v),
                             jnp.swapaxes(cache_fox_logf[0], 1, 2), us_page)

    tail = jnp.zeros((AUX_ROWS - N_META - dec, W_Q), BF16)
    merged_a = jnp.concatenate([merged_meta, merged_dec.reshape(dec, W_Q), tail], axis=0)

    y_main = _finish_rows(x_main, merged_m, wts)
    y_aux = _finish_rows(x_aux, merged_a, wts)

    def prompt_state(a_aux, a_main, tail_shape):
        meta = jnp.broadcast_to(a_aux[None, :N_META], (batch, N_META, a_aux.shape[1]))
        full = jnp.concatenate([meta, a_main.reshape(batch, seq, a_aux.shape[1])], axis=1)
        return full.reshape((1, batch, seq + N_META) + tail_shape)

    kv_shape = (N_KV_HEADS, HEAD_DIM)
    sample_state = lambda a, tail_shape: a[N_META:N_META + dec].reshape((1, dec, 1) + tail_shape)
    return (y_main.reshape(batch, seq, D_MODEL), y_aux[N_META:N_META + dec].reshape(dec, 1, D_MODEL),
            prompt_state(sk_a, sk_m, kv_shape), prompt_state(sv_a, sv_m, kv_shape),
            prompt_state(fk_a, fk_m, kv_shape), prompt_state(fv_a, fv_m, kv_shape),
            prompt_state(lf_a, lf_m, (N_Q_HEADS,)),
            sample_state(sk_a, kv_shape), sample_state(sv_a, kv_shape),
            sample_state(fk_a, kv_shape), sample_state(fv_a, kv_shape),
            sample_state(lf_a, (N_Q_HEADS,)))
```

```python
import functools

import jax
import jax.numpy as jnp
from jax import lax
from jax.experimental import pallas as pl
from jax.experimental.pallas import tpu as pltpu

D_MODEL = 2048
HEAD_DIM = 128
N_Q_HEADS = 16
N_KV_HEADS = 4
GROUP = 4
W_Q = N_Q_HEADS * HEAD_DIM
W_KV = N_KV_HEADS * HEAD_DIM
D_FF = 4 * D_MODEL
N_META = 16
PAGE_SIZE = 128
EPS = 1e-6
LOG2E = 1.4426950408889634
Q_SCALE = HEAD_DIM ** -0.5 * LOG2E
FORGET_OFFSET = 2 * W_Q + 4 * W_KV
AUX_ROWS = 32

BF16 = jnp.bfloat16
F32 = jnp.float32

LANES = 128
SUBLANES = 8
NEG_BIG = -1e30
SB_DEAD = 104.0 * LOG2E
VMEM_LIMIT = 48 * 1024 * 1024
FFN_VMEM_LIMIT = 58 * 1024 * 1024
TM_MAIN = 1024
TM_PROJ = 2048
TN = 512

_NT = (((1,), (1,)), ((), ()))


def _params(*sem):
    return pltpu.CompilerParams(dimension_semantics=sem, vmem_limit_bytes=VMEM_LIMIT)


def _rms(x, g):
    ms = jnp.mean(x * x, axis=-1, keepdims=True)
    return x * lax.rsqrt(ms + EPS) * g


def _softplus(z):
    return jnp.maximum(z, 0.0) + jnp.log1p(jnp.exp(-jnp.abs(z)))


def _softplus2(z2):
    return jnp.maximum(z2, 0.0) + jnp.log2(1.0 + jnp.exp2(-jnp.abs(z2)))


def _split2(x):
    hi = x.astype(BF16)
    lo = (x - hi.astype(F32)).astype(BF16)
    return hi, lo


def _split3(x):
    hi = x.astype(BF16)
    r = x - hi.astype(F32)
    mid = r.astype(BF16)
    lo = (r - mid.astype(F32)).astype(BF16)
    return hi, mid, lo


def _dot(a, b):
    return jnp.dot(a, b, preferred_element_type=F32)


def _dot_t(a, bt):
    return lax.dot_general(a, bt, _NT, preferred_element_type=F32)


def _split_dot(parts, m_stacked):
    return _dot(jnp.concatenate(parts, axis=1), m_stacked)


def _rms_forget_kernel(x_ref, g_ref, whi_ref, wlo_ref, b_ref, h_ref, lf_ref):
    h = _rms(x_ref[...], g_ref[...])
    h_ref[...] = h.astype(h_ref.dtype)
    hh, hl = _split2(h)
    z = _dot_t(hh, whi_ref[...]) + _dot_t(hl, whi_ref[...]) + _dot_t(hh, wlo_ref[...])
    z = z[:, :N_Q_HEADS] + b_ref[...]
    lf_ref[...] = -_softplus(-z)


def _rms_forget(x, g, w_hi, w_lo, b, tm):
    rows = x.shape[0]
    const = lambda shape: pl.BlockSpec(shape, lambda i: (0, 0))
    return pl.pallas_call(
        _rms_forget_kernel,
        out_shape=[jax.ShapeDtypeStruct((rows, D_MODEL), BF16),
                   jax.ShapeDtypeStruct((rows, N_Q_HEADS), F32)],
        grid=(rows // tm,),
        in_specs=[pl.BlockSpec((tm, D_MODEL), lambda i: (i, 0)),
                  const((1, D_MODEL)), const((LANES, D_MODEL)), const((LANES, D_MODEL)),
                  const((1, N_Q_HEADS))],
        out_specs=[pl.BlockSpec((tm, D_MODEL), lambda i: (i, 0)),
                   pl.BlockSpec((tm, N_Q_HEADS), lambda i: (i, 0))],
        compiler_params=_params("parallel"),
        name="rms_forget",
    )(x, g, w_hi, w_lo, b)


def _head_norm(acc, g):
    outs = []
    for hh in range(acc.shape[1] // HEAD_DIM):
        xh = acc[:, hh * HEAD_DIM:(hh + 1) * HEAD_DIM]
        outs.append(_rms(xh, g))
    return jnp.concatenate(outs, axis=1)


def _proj_kernel(h_ref, wt_ref, g_ref, o_ref, *, mode):
    acc = _dot_t(h_ref[...], wt_ref[...].astype(BF16))
    if mode == "scale":
        acc = acc * Q_SCALE
    elif mode == "norm_scale":
        acc = _head_norm(acc, g_ref[...]) * Q_SCALE
    elif mode == "sigmoid":
        acc = jax.nn.sigmoid(acc)
    o_ref[...] = acc.astype(o_ref.dtype)


def _wt_rows_spec(row0):
    if row0 % TN == 0:
        return pl.BlockSpec((TN, D_MODEL), lambda i, j: (row0 // TN + j, 0))
    assert row0 % SUBLANES == 0
    return pl.BlockSpec((pl.Element(TN), pl.Element(D_MODEL)),
                        lambda i, j: (pl.multiple_of(row0 + j * TN, SUBLANES), 0))


def _proj(h, wt, row0, n, g, mode, tm, name):
    rows = h.shape[0]
    return pl.pallas_call(
        functools.partial(_proj_kernel, mode=mode),
        out_shape=jax.ShapeDtypeStruct((rows, n), BF16),
        grid=(rows // tm, n // TN),
        in_specs=[pl.BlockSpec((tm, D_MODEL), lambda i, j: (i, 0)),
                  _wt_rows_spec(row0),
                  pl.BlockSpec((1, HEAD_DIM), lambda i, j: (0, 0))],
        out_specs=pl.BlockSpec((tm, TN), lambda i, j: (i, j)),
        compiler_params=_params("parallel", "arbitrary"),
        name=name,
    )(h, wt, g)


def _kv_kernel(h_ref, wt_ref, gk_ref, *out_refs):
    j = pl.program_id(1)
    acc = _dot_t(h_ref[...], wt_ref[...].astype(BF16))
    for jj in range(4):
        @pl.when(j == jj)
        def _(jj=jj):
            val = _head_norm(acc, gk_ref[...]) if jj == 2 else acc
            out_refs[jj][...] = val
            out_refs[4 + jj][...] = val.astype(BF16)


def _kv_wt_spec():
    sb_kv, fox_kv = W_Q // W_KV, (2 * W_Q + 2 * W_KV) // W_KV
    return pl.BlockSpec((W_KV, D_MODEL), lambda i, j: (sb_kv + j + (fox_kv - sb_kv - 2) * (j // 2), 0))


def _kv_state_kernel(h_ref, wt_ref, gk_ref, meta_ref, *refs, tiles_per_batch):
    state = refs[:4]
    bf_refs = refs[4:8]
    stage, sem, meta_sem = refs[8:]
    i, j = pl.program_id(0), pl.program_id(1)
    n_i = pl.num_programs(0)
    tm = h_ref.shape[0]
    b, it = i // tiles_per_batch, i % tiles_per_batch
    row0 = pl.multiple_of(N_META * N_KV_HEADS + it * tm * N_KV_HEADS, SUBLANES)
    acc = _dot_t(h_ref[...], wt_ref[...].astype(BF16))

    def tile_copy(jj, slot):
        return pltpu.make_async_copy(stage.at[slot], state[jj].at[b, pl.ds(row0, tm * N_KV_HEADS), :],
                                     sem.at[slot])

    def meta_copy(jj):
        return pltpu.make_async_copy(meta_ref.at[jj], state[jj].at[b, pl.ds(0, N_META * N_KV_HEADS), :],
                                     meta_sem.at[0])

    for jj in range(4):
        @pl.when(j == jj)
        def _(jj=jj):
            slot = jj % 2
            val = _head_norm(acc, gk_ref[...]) if jj == 2 else acc
            bf_refs[jj][...] = val.astype(BF16)

            @pl.when(jnp.logical_or(i > 0, jj >= 2))
            def _():
                tile_copy(jj, slot).wait()

            for n in range(N_KV_HEADS):
                stage[slot, pl.ds(n, tm, stride=N_KV_HEADS), :] = val[:, n * HEAD_DIM:(n + 1) * HEAD_DIM]
            tile_copy(jj, slot).start()

            @pl.when(it == 0)
            def _():
                meta_copy(jj).start()

    @pl.when(jnp.logical_and(i == n_i - 1, j == 3))
    def _():
        tile_copy(2, 0).wait()
        tile_copy(3, 1).wait()
        for _ in range(4 * (n_i // tiles_per_batch)):
            meta_copy(0).wait()


def _kv_state_proj(h, wt, g_k, meta_rows, tm, batch, seq):
    rows = h.shape[0]
    tiles_per_batch = seq // tm
    state_rows = (N_META + seq) * N_KV_HEADS
    any_spec = pl.BlockSpec(memory_space=pl.ANY)
    return pl.pallas_call(
        functools.partial(_kv_state_kernel, tiles_per_batch=tiles_per_batch),
        out_shape=[jax.ShapeDtypeStruct((batch, state_rows, HEAD_DIM), F32)] * 4
        + [jax.ShapeDtypeStruct((rows, W_KV), BF16)] * 4,
        grid=(rows // tm, 4),
        in_specs=[pl.BlockSpec((tm, D_MODEL), lambda i, j: (i, 0)),
                  _kv_wt_spec(),
                  pl.BlockSpec((1, HEAD_DIM), lambda i, j: (0, 0)),
                  pl.BlockSpec(meta_rows.shape, lambda i, j: (0, 0, 0))],
        out_specs=[any_spec] * 4 + [pl.BlockSpec((tm, W_KV), lambda i, j: (i, 0))] * 4,
        scratch_shapes=[pltpu.VMEM((2, tm * N_KV_HEADS, HEAD_DIM), F32),
                        pltpu.SemaphoreType.DMA((2,)),
                        pltpu.SemaphoreType.DMA((1,))],
        compiler_params=_params("arbitrary", "arbitrary"),
        name="kv_state_proj",
    )(h, wt, g_k, meta_rows)


def _kv_proj(h, wt, g_k, tm):
    rows = h.shape[0]
    blk = pl.BlockSpec((tm, W_KV), lambda i, j: (i, 0))
    return pl.pallas_call(
        _kv_kernel,
        out_shape=[jax.ShapeDtypeStruct((rows, W_KV), F32)] * 4
        + [jax.ShapeDtypeStruct((rows, W_KV), BF16)] * 4,
        grid=(rows // tm, 4),
        in_specs=[pl.BlockSpec((tm, D_MODEL), lambda i, j: (i, 0)),
                  _kv_wt_spec(),
                  pl.BlockSpec((1, HEAD_DIM), lambda i, j: (0, 0))],
        out_specs=[blk] * 8,
        compiler_params=_params("parallel", "arbitrary"),
        name="kv_proj",
    )(h, wt, g_k)


def _lane_cumsum(x):
    n = x.shape[-1]
    lane = lax.broadcasted_iota(jnp.int32, x.shape, x.ndim - 1)
    k = 1
    while k < n:
        x = x + jnp.where(lane >= k, pltpu.roll(x, k, axis=x.ndim - 1), 0.0)
        k *= 2
    return x


def _cum_kernel(meta_ref, main_ref, cmeta_ref, cmain_ref):
    cm = _lane_cumsum(meta_ref[...])
    cmeta_ref[...] = cm * LOG2E
    cmain_ref[...] = (_lane_cumsum(main_ref[...]) + cm[:, LANES - 1:LANES]) * LOG2E


def _cum_forget(lf_meta_t, lf_main_t):
    b, _, s = lf_main_t.shape
    return pl.pallas_call(
        _cum_kernel,
        out_shape=[jax.ShapeDtypeStruct((b, N_Q_HEADS, LANES), F32),
                   jax.ShapeDtypeStruct((b, N_Q_HEADS, s), F32)],
        grid=(b,),
        in_specs=[pl.BlockSpec((N_Q_HEADS, LANES), lambda i: (0, 0)),
                  pl.BlockSpec((None, N_Q_HEADS, s), lambda i: (i, 0, 0))],
        out_specs=[pl.BlockSpec((None, N_Q_HEADS, LANES), lambda i: (i, 0, 0)),
                   pl.BlockSpec((None, N_Q_HEADS, s), lambda i: (i, 0, 0))],
        compiler_params=_params("parallel"),
        name="cum_forget",
    )(lf_meta_t, lf_main_t)


def _stack_heads(q):
    return jnp.concatenate([q[:, g * HEAD_DIM:(g + 1) * HEAD_DIM] for g in range(GROUP)], axis=0)


SB_CHUNK = 512
FOX_CHUNK = 512


def _chunk_visible(rel, c, cr, t, tk):
    col = lax.broadcasted_iota(jnp.int32, (cr, tk), 1)
    row = (lax.broadcasted_iota(jnp.int32, (cr, tk), 0) + c * cr) % t
    return rel(col, row)


def _sb_block(qs, kb, vb, uo, rel, r_ref, acc_ref):
    r, tk = qs.shape[0], kb.shape[0]
    t = r // GROUP
    cr = min(r, SB_CHUNK)
    score = lambda c: lax.dot_general(qs[c * cr:(c + 1) * cr], kb, _NT, preferred_element_type=F32)
    z_next = score(0)
    for c in range(r // cr):
        rows = slice(c * cr, (c + 1) * cr)
        z = z_next
        if c + 1 < r // cr:
            z_next = score(c + 1)
        sp = _softplus2(z)
        if rel is not None:
            vis = _chunk_visible(rel, c, cr, t, tk)
            sp = jnp.where(vis, sp, 0.0)
        ct = _split_dot(_split2(sp), uo)
        r_prev = r_ref[rows]
        a = jnp.exp2(z - ct[:, :tk] - r_prev)
        if rel is not None:
            a = jnp.where(vis, a, 0.0)
        acc_ref[rows] += _dot(a.astype(BF16), vb)
        r_ref[rows] = r_prev + ct[:, tk:]


def _sb_window(qs, kb, vb, uo, r_ref, acc_ref):
    r = qs.shape[0]
    t = r // GROUP
    nb = kb.shape[0] // t
    z = _dot_t(qs, kb)
    sp = _softplus2(z)
    vis = _chunk_visible(lambda c, row: c < row, 0, r, t, t)
    newer = None
    weights = [None] * nb
    for b in reversed(range(nb)):
        cols = slice(b * t, (b + 1) * t)
        diag = b == nb - 1
        spb = jnp.where(vis, sp[:, cols], 0.0) if diag else sp[:, cols]
        ct = _split_dot(_split2(spb), uo)
        arg = z[:, cols] - ct[:, :t]
        a = jnp.exp2(arg if newer is None else arg - newer)
        weights[b] = (jnp.where(vis, a, 0.0) if diag else a).astype(BF16)
        newer = ct[:, t:] if newer is None else newer + ct[:, t:]
    acc_ref[...] = _dot(jnp.concatenate(weights, axis=1), vb)
    r_ref[...] = newer


def _fox_block(qs, kb, vb, ckb, rel, m_ref, acc_ref):
    r, tk = qs.shape[0], kb.shape[0]
    t = r // GROUP
    cr = min(r, max(t, FOX_CHUNK))
    hc = cr // t
    v1 = jnp.concatenate([vb, jnp.ones((tk, LANES), BF16)], axis=1)
    score = lambda c: lax.dot_general(qs[c * cr:(c + 1) * cr], kb, _NT, preferred_element_type=F32)
    z_next = score(0)
    for c in range(r // cr):
        rows = slice(c * cr, (c + 1) * cr)
        bias = ckb[c * hc:(c + 1) * hc]
        z = (z_next.reshape(hc, t, tk) - bias[:, None, :]).reshape(cr, tk)
        if c + 1 < r // cr:
            z_next = score(c + 1)
        if rel is not None:
            z = jnp.where(_chunk_visible(rel, c, cr, t, tk), z, NEG_BIG)
        m_prev = m_ref[rows]
        m_new = jnp.maximum(m_prev, jnp.max(z, axis=1, keepdims=True))
        alpha = jnp.exp2(m_prev - m_new)
        p = jnp.exp2(z - jnp.tile(m_new, (1, tk // LANES)))
        acc_ref[rows] = acc_ref[rows] * jnp.tile(alpha, (1, 2)) + _dot(p.astype(BF16), v1)
        m_ref[rows] = m_new


SB_T = 128
SB_WIN = 3


def _store_heads(o_ref, o, t, gate_ref, add_ref=None):
    for g in range(GROUP):
        cols = slice(g * HEAD_DIM, (g + 1) * HEAD_DIM)
        val = o[g * t:(g + 1) * t] * gate_ref[:, cols].astype(F32)
        if add_ref is not None:
            val = val + add_ref[:, cols].astype(F32)
        o_ref[:, cols] = val.astype(o_ref.dtype)


def _sb_prompt_kernel(q_ref, k_ref, v_ref, mk_ref, mv_ref, uo_ref, gate_ref, o_ref, r_ref, acc_ref):
    i = pl.program_id(2)
    t = SB_T
    qs = _stack_heads(q_ref[...])
    uo = uo_ref[...]

    def visit(kb, vb, rel):
        _sb_block(qs, kb, vb, uo, rel, r_ref, acc_ref)
        return jnp.min(r_ref[...])

    windowed = i >= SB_WIN - 1

    @pl.when(windowed)
    def _():
        s = pl.multiple_of((i - (SB_WIN - 1)) * t, t)
        _sb_window(qs, k_ref[pl.ds(s, SB_WIN * t), :], v_ref[pl.ds(s, SB_WIN * t), :], uo, r_ref, acc_ref)

    @pl.when(jnp.logical_not(windowed))
    def _():
        r_ref[...] = jnp.zeros_like(r_ref)
        acc_ref[...] = jnp.zeros_like(acc_ref)
        s = pl.multiple_of(i * t, t)
        _sb_block(qs, k_ref[pl.ds(s, t), :], v_ref[pl.ds(s, t), :], uo, lambda c, r: c < r, r_ref, acc_ref)

    def cond(c):
        return jnp.logical_and(c[0] >= 0, c[1] < SB_DEAD)

    def body(c):
        s = pl.multiple_of(c[0] * t, t)
        return c[0] - 1, visit(k_ref[pl.ds(s, t), :], v_ref[pl.ds(s, t), :], None)

    _, mn = lax.while_loop(cond, body, (jnp.where(windowed, i - SB_WIN, i - 1), jnp.min(r_ref[...])))

    @pl.when(mn < SB_DEAD)
    def _():
        visit(mk_ref[...], mv_ref[...], lambda c, r: c < N_META)

    _store_heads(o_ref, acc_ref[...], t, gate_ref)


def _sb_prompt(q, k, v, mk, mv, uo, gates, batch, seq):
    t = SB_T
    nb = seq // t
    tile = pl.BlockSpec((t, GROUP * HEAD_DIM), lambda b, n, i: (b * nb + i, n))
    return pl.pallas_call(
        _sb_prompt_kernel,
        out_shape=jax.ShapeDtypeStruct((batch * seq, W_Q), BF16),
        grid=(batch, N_KV_HEADS, nb),
        in_specs=[tile,
                  pl.BlockSpec((seq, HEAD_DIM), lambda b, n, i: (b, n)),
                  pl.BlockSpec((seq, HEAD_DIM), lambda b, n, i: (b, n)),
                  pl.BlockSpec((LANES, HEAD_DIM), lambda b, n, i: (0, n)),
                  pl.BlockSpec((LANES, HEAD_DIM), lambda b, n, i: (0, n)),
                  pl.BlockSpec(uo.shape, lambda b, n, i: (0, 0)),
                  tile],
        out_specs=tile,
        scratch_shapes=[pltpu.VMEM((GROUP * t, t), F32), pltpu.VMEM((GROUP * t, HEAD_DIM), F32)],
        compiler_params=_params("parallel", "parallel", "arbitrary"),
        name="sb_prompt",
    )(q, k, v, mk, mv, uo, gates)


FOX_T = 256


def _fox_prompt_kernel(q_ref, k_ref, v_ref, ck_ref, mk_ref, mv_ref, mck_ref, gate_ref, sb_ref,
                       o_ref, m_ref, acc_ref):
    i = pl.program_id(2)
    t = FOX_T
    qs = _stack_heads(q_ref[...])
    m_ref[...] = jnp.full_like(m_ref, NEG_BIG)
    acc_ref[...] = jnp.zeros_like(acc_ref)

    def visit(s, width, rel):
        _fox_block(qs, k_ref[pl.ds(s, width), :], v_ref[pl.ds(s, width), :], ck_ref[:, pl.ds(s, width)],
                   rel, m_ref, acc_ref)

    visit(pl.multiple_of(i * t, t), t, lambda c, r: c <= r)

    def body(j, carry):
        visit(pl.multiple_of(j * 2 * t, 2 * t), 2 * t, None)
        return carry

    lax.fori_loop(0, i // 2, body, 0)

    @pl.when(i % 2 == 1)
    def _():
        visit(pl.multiple_of((i - 1) * t, t), t, None)

    _fox_block(qs, mk_ref[...], mv_ref[...], mck_ref[...],
               lambda c, r: c < N_META, m_ref, acc_ref)

    acc = acc_ref[...]
    _store_heads(o_ref, acc[:, :HEAD_DIM] / acc[:, HEAD_DIM:], t, gate_ref, sb_ref)


def _fox_prompt(q, k, v, ck, mk, mv, mck, gates, sb_part, batch, seq):
    t = FOX_T
    nb = seq // t
    tile = pl.BlockSpec((t, GROUP * HEAD_DIM), lambda b, n, i: (b * nb + i, n))
    return pl.pallas_call(
        _fox_prompt_kernel,
        out_shape=jax.ShapeDtypeStruct((batch * seq, W_Q), BF16),
        grid=(batch, N_KV_HEADS, nb),
        in_specs=[tile,
                  pl.BlockSpec((seq, HEAD_DIM), lambda b, n, i: (b, n)),
                  pl.BlockSpec((seq, HEAD_DIM), lambda b, n, i: (b, n)),
                  pl.BlockSpec((None, None, GROUP, seq), lambda b, n, i: (b, n, 0, 0)),
                  pl.BlockSpec((LANES, HEAD_DIM), lambda b, n, i: (0, n)),
                  pl.BlockSpec((LANES, HEAD_DIM), lambda b, n, i: (0, n)),
                  pl.BlockSpec((None, None, GROUP, LANES), lambda b, n, i: (b, n, 0, 0)),
                  pl.BlockSpec((t, GROUP * HEAD_DIM), lambda b, n, i: (b * nb + i, N_KV_HEADS + n)),
                  tile],
        out_specs=tile,
        scratch_shapes=[pltpu.VMEM((GROUP * t, LANES), F32), pltpu.VMEM((GROUP * t, 2 * LANES), F32)],
        compiler_params=_params("parallel", "parallel", "arbitrary"),
        name="fox_prompt",
    )(q, k, v, ck, mk, mv, mck, gates, sb_part)


def _meta_attn_kernel(sq_ref, sk_ref, sv_ref, fq_ref, fk_ref, fv_ref, ck_ref, uo_ref, gsb_ref, gfx_ref,
                      o_ref, m_ref, acc_ref, r_ref, sacc_ref):
    t = N_META
    valid = lambda c: c < N_META
    r_ref[...] = jnp.zeros_like(r_ref)
    sacc_ref[...] = jnp.zeros_like(sacc_ref)
    _sb_block(_stack_heads(sq_ref[...]), sk_ref[...], sv_ref[...], uo_ref[...],
              lambda c, r: jnp.logical_and(c < r, valid(c)), r_ref, sacc_ref)
    o = sacc_ref[...]
    m_ref[...] = jnp.full_like(m_ref, NEG_BIG)
    acc_ref[...] = jnp.zeros_like(acc_ref)
    _fox_block(_stack_heads(fq_ref[...]), fk_ref[...], fv_ref[...], ck_ref[...],
               lambda c, r: jnp.logical_and(c <= r, valid(c)), m_ref, acc_ref)
    acc = acc_ref[...]
    f = acc[:, :HEAD_DIM] / acc[:, HEAD_DIM:]
    for g in range(GROUP):
        cols = slice(g * HEAD_DIM, (g + 1) * HEAD_DIM)
        merged = (gsb_ref[:, cols].astype(F32) * o[g * t:(g + 1) * t]
                  + gfx_ref[:, cols].astype(F32) * f[g * t:(g + 1) * t])
        o_ref[:, cols] = merged.astype(o_ref.dtype)


def _meta_attn(sq, sk, sv, fq, fk, fv, ck, uo, gates):
    qspec = pl.BlockSpec((N_META, GROUP * HEAD_DIM), lambda n: (0, n))
    kspec = pl.BlockSpec((LANES, HEAD_DIM), lambda n: (0, n))
    return pl.pallas_call(
        _meta_attn_kernel,
        out_shape=jax.ShapeDtypeStruct((N_META, W_Q), BF16),
        grid=(N_KV_HEADS,),
        in_specs=[qspec, kspec, kspec, qspec, kspec, kspec,
                  pl.BlockSpec((None, GROUP, LANES), lambda n: (n, 0, 0)),
                  pl.BlockSpec(uo.shape, lambda n: (0, 0)),
                  qspec,
                  pl.BlockSpec((N_META, GROUP * HEAD_DIM), lambda n: (0, N_KV_HEADS + n))],
        out_specs=qspec,
        scratch_shapes=[pltpu.VMEM((GROUP * N_META, LANES), F32),
                        pltpu.VMEM((GROUP * N_META, 2 * LANES), F32),
                        pltpu.VMEM((GROUP * N_META, LANES), F32),
                        pltpu.VMEM((GROUP * N_META, HEAD_DIM), F32)],
        compiler_params=_params("parallel"),
        name="meta_attn",
    )(sq, sk, sv, fq, fk, fv, ck, uo, gates, gates)


PAGE_COLS = PAGE_SIZE * N_KV_HEADS
DEC_G = 16


def _own_head(shape):
    row = lax.broadcasted_iota(jnp.int32, shape, 0)
    col = lax.broadcasted_iota(jnp.int32, shape, 1)
    return (col % N_KV_HEADS) == (row // GROUP)


def _sb_decode_kernel(pt_ref, q_ref, gate_ref, ck_hbm, cv_hbm, uo_ref, o_ref, kbuf, vbuf, sem, r_ref, acc_ref):
    b = pl.program_id(0)
    n_pages = pt_ref.shape[1]
    q = q_ref[...]
    uo = uo_ref[...]
    own = _own_head((N_Q_HEADS, PAGE_COLS))

    def copies(p, slot):
        pid = pt_ref[b, p]
        return [pltpu.make_async_copy(ck_hbm.at[pid], kbuf.at[slot], sem.at[0, slot]),
                pltpu.make_async_copy(cv_hbm.at[pid], vbuf.at[slot], sem.at[1, slot])]

    r_ref[...] = jnp.zeros_like(r_ref)
    acc_ref[...] = jnp.zeros_like(acc_ref)
    for c in copies(n_pages - 1, (n_pages - 1) % 2):
        c.start()

    def cond(c):
        return jnp.logical_and(c[0] >= 0, c[1] < SB_DEAD)

    def body(c):
        p = c[0]
        slot = p % 2
        for cp in copies(p, slot):
            cp.wait()

        @pl.when(p > 0)
        def _():
            for cp in copies(p - 1, 1 - slot):
                cp.start()

        z = lax.dot_general(q, kbuf[slot].astype(BF16), _NT, preferred_element_type=F32)
        sp = jnp.where(own, _softplus2(z), 0.0)
        ct = _split_dot(_split2(sp), uo)
        r = r_ref[...]
        a = jnp.where(own, jnp.exp2(z - ct[:, :PAGE_COLS] - jnp.tile(r, (1, N_KV_HEADS))), 0.0)
        acc_ref[...] += _dot(a.astype(BF16), vbuf[slot].astype(BF16))
        r_new = r + ct[:, PAGE_COLS:]
        r_ref[...] = r_new
        return p - 1, jnp.min(r_new)

    p_end, _ = lax.while_loop(cond, body, (jnp.int32(n_pages - 1), jnp.float32(0.0)))

    @pl.when(p_end >= 0)
    def _():
        for cp in copies(p_end, p_end % 2):
            cp.wait()

    o_ref[...] = (acc_ref[...] * gate_ref[...].astype(F32)).astype(o_ref.dtype)


def _sb_decode(page_table, q, gate, cache_k, cache_v, uo):
    nb = q.shape[0]
    any_spec = pl.BlockSpec(memory_space=pl.ANY)
    head_spec = pl.BlockSpec((None, N_Q_HEADS, HEAD_DIM), lambda b, pt: (b, 0, 0))
    return pl.pallas_call(
        _sb_decode_kernel,
        out_shape=jax.ShapeDtypeStruct((nb, N_Q_HEADS, HEAD_DIM), BF16),
        grid_spec=pltpu.PrefetchScalarGridSpec(
            num_scalar_prefetch=1,
            grid=(nb,),
            in_specs=[head_spec, head_spec, any_spec, any_spec,
                      pl.BlockSpec(uo.shape, lambda b, pt: (0, 0))],
            out_specs=head_spec,
            scratch_shapes=[pltpu.VMEM((2, PAGE_COLS, HEAD_DIM), F32),
                            pltpu.VMEM((2, PAGE_COLS, HEAD_DIM), F32),
                            pltpu.SemaphoreType.DMA((2, 2)),
                            pltpu.VMEM((N_Q_HEADS, LANES), F32),
                            pltpu.VMEM((N_Q_HEADS, HEAD_DIM), F32)]),
        compiler_params=_params("arbitrary"),
        name="sb_decode",
    )(page_table, q, gate, cache_k, cache_v, uo)


def _fox_decode_kernel(pt_ref, q_ref, kn_ref, vn_ref, lfn_ref, gate_ref, sb_ref, ck_hbm, cv_hbm, lf_hbm,
                       us_ref, o_ref, kbuf, vbuf, lbuf, sem, m_ref, acc_ref, d_ref):
    n_seq, n_pages = pt_ref.shape
    n_grp = n_pages // DEC_G
    total = n_seq * n_grp
    us = us_ref[...]
    bias = jnp.where(_own_head((N_Q_HEADS, PAGE_COLS)), 0.0, NEG_BIG)

    def copies(t, slot):
        b = t // n_grp
        newest = n_pages - 1 - (t % n_grp) * DEC_G
        out = []
        for g in range(DEC_G):
            pid = pt_ref[b, newest - g]
            out += [pltpu.make_async_copy(ck_hbm.at[pid], kbuf.at[slot, g], sem.at[0, slot]),
                    pltpu.make_async_copy(cv_hbm.at[pid], vbuf.at[slot, g], sem.at[1, slot]),
                    pltpu.make_async_copy(lf_hbm.at[pid], lbuf.at[slot, g], sem.at[2, slot])]
        return out

    for cp in copies(0, 0):
        cp.start()

    def body(t, carry):
        slot = t % 2
        b = t // n_grp
        gi = t % n_grp

        @pl.when(t + 1 < total)
        def _():
            for cp in copies(t + 1, 1 - slot):
                cp.start()

        q = q_ref[b]

        @pl.when(gi == 0)
        def _():
            z_self = jnp.sum(q.astype(F32) * kn_ref[b].astype(F32), axis=1, keepdims=True)
            m_ref[...] = jnp.broadcast_to(z_self, m_ref.shape)
            acc_ref[...] = jnp.concatenate([vn_ref[b].astype(F32), jnp.ones((N_Q_HEADS, LANES), F32)], axis=1)
            d_ref[...] = jnp.broadcast_to(lfn_ref[b] * LOG2E, d_ref.shape)

        for cp in copies(t, slot):
            cp.wait()

        dt = _split_dot(_split3(lbuf[slot].reshape(DEC_G * N_Q_HEADS, PAGE_SIZE)), us) * LOG2E
        d = d_ref[...]
        zs = []
        for g in range(DEC_G):
            dg = dt[g * N_Q_HEADS:(g + 1) * N_Q_HEADS]
            z = lax.dot_general(q, kbuf[slot, g].astype(BF16), _NT, preferred_element_type=F32)
            zs.append(z + dg[:, :PAGE_COLS] + jnp.tile(d, (1, N_KV_HEADS)) + bias)
            d = d + dg[:, PAGE_COLS:]
        d_ref[...] = d
        z = jnp.concatenate(zs, axis=1)
        m_prev = m_ref[...]
        m_new = jnp.maximum(m_prev, jnp.max(z, axis=1, keepdims=True))
        alpha = jnp.exp2(m_prev - m_new)
        p = jnp.exp2(z - jnp.tile(m_new, (1, z.shape[1] // LANES)))
        lsum = jnp.sum(p, axis=1, keepdims=True)
        pb = p.astype(BF16)
        pv = _dot(pb[:, :PAGE_COLS], vbuf[slot, 0].astype(BF16))
        for g in range(1, DEC_G):
            pv += _dot(pb[:, g * PAGE_COLS:(g + 1) * PAGE_COLS], vbuf[slot, g].astype(BF16))
        acc = acc_ref[...] * jnp.tile(alpha, (1, 2)) + jnp.concatenate(
            [pv, jnp.broadcast_to(lsum, pv.shape)], axis=1)
        acc_ref[...] = acc
        m_ref[...] = m_new

        @pl.when(gi == n_grp - 1)
        def _():
            merged = acc[:, :HEAD_DIM] / acc[:, HEAD_DIM:] * gate_ref[b].astype(F32) + sb_ref[b].astype(F32)
            o_ref[b] = merged.astype(o_ref.dtype)

        return carry

    lax.fori_loop(0, total, body, 0)


def _fox_decode(page_table, q, k_new, v_new, lf_new, gate, sb_part, cache_k, cache_v, cache_lf_t, us):
    nb = q.shape[0]
    assert page_table.shape[1] % DEC_G == 0
    any_spec = pl.BlockSpec(memory_space=pl.ANY)
    head_spec = pl.BlockSpec((nb, N_Q_HEADS, HEAD_DIM), lambda i, pt: (0, 0, 0))
    return pl.pallas_call(
        _fox_decode_kernel,
        out_shape=jax.ShapeDtypeStruct((nb, N_Q_HEADS, HEAD_DIM), BF16),
        grid_spec=pltpu.PrefetchScalarGridSpec(
            num_scalar_prefetch=1,
            grid=(1,),
            in_specs=[head_spec, head_spec, head_spec,
                      pl.BlockSpec((nb, N_Q_HEADS, 1), lambda i, pt: (0, 0, 0)),
                      head_spec, head_spec,
                      any_spec, any_spec, any_spec,
                      pl.BlockSpec(us.shape, lambda i, pt: (0, 0))],
            out_specs=head_spec,
            scratch_shapes=[pltpu.VMEM((2, DEC_G, PAGE_COLS, HEAD_DIM), F32),
                            pltpu.VMEM((2, DEC_G, PAGE_COLS, HEAD_DIM), F32),
                            pltpu.VMEM((2, DEC_G, N_Q_HEADS, PAGE_SIZE), F32),
                            pltpu.SemaphoreType.DMA((3, 2)),
                            pltpu.VMEM((N_Q_HEADS, LANES), F32),
                            pltpu.VMEM((N_Q_HEADS, 2 * LANES), F32),
                            pltpu.VMEM((N_Q_HEADS, LANES), F32)]),
        compiler_params=_params("arbitrary"),
        name="fox_decode",
    )(page_table, q, k_new, v_new, lf_new, gate, sb_part, cache_k, cache_v, cache_lf_t, us)


def _out_proj_kernel(m_ref, w_ref, x_ref, y_ref):
    y_ref[...] = x_ref[...] + _dot(m_ref[...], w_ref[...])


def _out_proj(merged, w, x, tm):
    rows = x.shape[0]
    return pl.pallas_call(
        _out_proj_kernel,
        out_shape=jax.ShapeDtypeStruct((rows, D_MODEL), F32),
        grid=(rows // tm, D_MODEL // TN),
        in_specs=[pl.BlockSpec((tm, W_Q), lambda i, j: (i, 0)),
                  pl.BlockSpec((W_Q, TN), lambda i, j: (0, j)),
                  pl.BlockSpec((tm, TN), lambda i, j: (i, j))],
        out_specs=pl.BlockSpec((tm, TN), lambda i, j: (i, j)),
        compiler_params=_params("parallel", "arbitrary"),
        name="out_proj",
    )(merged, w, x)


def _ffn_kernel(y_ref, g_ref, wu_ref, wd_ref, o_ref, h_ref):
    @pl.when(pl.program_id(1) == 0)
    def _():
        y = y_ref[...]
        h_ref[...] = _rms(y, g_ref[...]).astype(BF16)
        o_ref[...] = y

    u = jnp.square(jnp.maximum(_dot(h_ref[...], wu_ref[...]), 0.0))
    o_ref[...] += _dot(u.astype(BF16), wd_ref[...])


def _ffn(y, g, w_up, w_down, tm):
    rows = y.shape[0]
    return pl.pallas_call(
        _ffn_kernel,
        out_shape=jax.ShapeDtypeStruct((rows, D_MODEL), F32),
        grid=(rows // tm, D_FF // TN),
        in_specs=[pl.BlockSpec((tm, D_MODEL), lambda i, k: (i, 0), pipeline_mode=pl.Buffered(1)),
                  pl.BlockSpec((1, D_MODEL), lambda i, k: (0, 0)),
                  pl.BlockSpec((D_MODEL, TN), lambda i, k: (0, k)),
                  pl.BlockSpec((TN, D_MODEL), lambda i, k: (k, 0))],
        out_specs=pl.BlockSpec((tm, D_MODEL), lambda i, k: (i, 0)),
        scratch_shapes=[pltpu.VMEM((tm, D_MODEL), BF16)],
        compiler_params=pltpu.CompilerParams(dimension_semantics=("parallel", "arbitrary"),
                                             vmem_limit_bytes=FFN_VMEM_LIMIT),
        name="ffn",
    )(y, g, w_up, w_down)


def _row_tile(rows, preferred):
    return preferred if rows % preferred == 0 else rows


def _project_rows(x, wts, prompt_shape=None, kv_meta=None):
    rows = x.shape[0]
    tm = _row_tile(rows, TM_PROJ)
    h, lf = _rms_forget(x, wts["g_mix"], wts["w_fl_hi"], wts["w_fl_lo"], wts["b_f"], _row_tile(rows, TM_MAIN // 2))
    wt = wts["w_in_t"]
    sq = _proj(h, wt, 0, W_Q, wts["g_q"], "scale", tm, "proj_sq")
    fq = _proj(h, wt, W_Q + 2 * W_KV, W_Q, wts["g_q"], "norm_scale", tm, "proj_fq")
    gates = _proj(h, wt, FORGET_OFFSET + N_Q_HEADS, 2 * D_MODEL, wts["g_q"], "sigmoid", tm, "proj_gates")
    if prompt_shape is None:
        kv = _kv_proj(h, wt, wts["g_k"], _row_tile(rows, TM_MAIN))
    else:
        kv = _kv_state_proj(h, wt, wts["g_k"], kv_meta, TM_MAIN, *prompt_shape)
    return sq, fq, gates, kv, lf


def _finish_rows(x, merged, wts):
    rows = x.shape[0]
    y1 = _out_proj(merged, wts["w_out"], x, _row_tile(rows, TM_PROJ))
    return _ffn(y1, wts["g_ffn"], wts["w_up"], wts["w_down"], _row_tile(rows, TM_MAIN))


def kernel(x_prompt, x_sample, cache_sb_k, cache_sb_v, cache_fox_k, cache_fox_v, cache_fox_logf,
           page_table, meta_tokens, g_mix, w_in, b_forget, g_q, g_k, w_out, g_ffn, w_up, w_down):
    batch, seq, _ = x_prompt.shape
    dec = x_sample.shape[0]
    assert w_in.shape[0] == 1 and x_sample.shape[1] == 1 and N_META + dec <= AUX_ROWS
    pool = cache_sb_k.shape[1]

    w_in_t = jnp.swapaxes(w_in[0], 0, 1)
    w_fl = jnp.pad(w_in_t[FORGET_OFFSET:FORGET_OFFSET + N_Q_HEADS], ((0, LANES - N_Q_HEADS), (0, 0)))
    w_fl_hi = w_fl.astype(BF16)
    wts = {
        "g_mix": g_mix, "g_q": g_q, "g_k": g_k, "g_ffn": g_ffn, "b_f": b_forget,
        "w_in_t": w_in_t,
        "w_fl_hi": w_fl_hi, "w_fl_lo": (w_fl - w_fl_hi.astype(F32)).astype(BF16),
        "w_out": w_out[0].astype(BF16), "w_up": w_up[0].astype(BF16), "w_down": w_down[0].astype(BF16),
    }

    x_main = x_prompt.reshape(batch * seq, D_MODEL)
    x_aux = jnp.concatenate([meta_tokens, x_sample.reshape(dec, D_MODEL),
                             jnp.zeros((AUX_ROWS - N_META - dec, D_MODEL), F32)], axis=0)

    sq_a, fq_a, gates_a, kv_a, lf_a = _project_rows(x_aux, wts)
    sk_a, sv_a, fk_a, fv_a, skb_a, svb_a, fkb_a, fvb_a = kv_a
    kv_meta = jnp.stack([a[:N_META].reshape(N_META * N_KV_HEADS, HEAD_DIM) for a in (sk_a, sv_a, fk_a, fv_a)])
    sq_m, fq_m, gates_m, kv_m, lf_m = _project_rows(x_main, wts, (batch, seq), kv_meta)
    sk_st, sv_st, fk_st, fv_st, skb_m, svb_m, fkb_m, fvb_m = kv_m

    lf_meta_t = jnp.pad(lf_a[:N_META].T, ((0, 0), (0, LANES - N_META)))
    lf_main_t = jnp.swapaxes(lf_m.reshape(batch, seq, N_Q_HEADS), 1, 2)
    ck_meta, ck_main = _cum_forget(lf_meta_t, lf_main_t)
    ck_meta = ck_meta.reshape(batch, N_KV_HEADS, GROUP, LANES)
    ck_main = ck_main.reshape(batch, N_KV_HEADS, GROUP, seq)

    pad_keys = lambda a: jnp.pad(a[:N_META], ((0, LANES - N_META), (0, 0)))
    mk_sb, mv_sb, mk_fx, mv_fx = pad_keys(skb_a), pad_keys(svb_a), pad_keys(fkb_a), pad_keys(fvb_a)

    j_idx = lax.broadcasted_iota(jnp.int32, (LANES, LANES), 0)
    s_idx = lax.broadcasted_iota(jnp.int32, (LANES, LANES), 1)
    ones = jnp.ones((LANES, LANES), BF16)
    uo = jnp.concatenate([(j_idx >= s_idx).astype(BF16), ones], axis=1)
    uo = jnp.concatenate([uo] * 2, axis=0)
    c_src = lax.broadcasted_iota(jnp.int32, (PAGE_COLS, PAGE_COLS), 0) // N_KV_HEADS
    c_dst = lax.broadcasted_iota(jnp.int32, (PAGE_COLS, PAGE_COLS), 1) // N_KV_HEADS
    uo_page = jnp.concatenate([(c_src >= c_dst).astype(BF16), jnp.ones((PAGE_COLS, LANES), BF16)], axis=1)
    uo_page = jnp.concatenate([uo_page] * 2, axis=0)
    j_key = lax.broadcasted_iota(jnp.int32, (PAGE_SIZE, PAGE_COLS), 0)
    c_key = lax.broadcasted_iota(jnp.int32, (PAGE_SIZE, PAGE_COLS), 1) // N_KV_HEADS
    us_page = jnp.concatenate([(j_key > c_key).astype(BF16), ones], axis=1)
    us_page = jnp.concatenate([us_page] * 3, axis=0)

    sb_m = _sb_prompt(sq_m, skb_m, svb_m, mk_sb, mv_sb, uo, gates_m, batch, seq)
    merged_m = _fox_prompt(fq_m, fkb_m, fvb_m, ck_main, mk_fx, mv_fx, ck_meta, gates_m, sb_m, batch, seq)
    merged_meta = _meta_attn(sq_a[:N_META], mk_sb, mv_sb, fq_a[:N_META], mk_fx, mv_fx,
                             ck_meta[0], uo, gates_a[:N_META])

    heads = lambda a: a[N_META:N_META + dec].reshape(dec, N_Q_HEADS, HEAD_DIM)
    kv_heads = lambda a: jnp.repeat(a[N_META:N_META + dec].reshape(dec, N_KV_HEADS, HEAD_DIM), GROUP, axis=1)
    pages = lambda c: c[0].reshape(pool, PAGE_COLS, HEAD_DIM)
    sb_dec = _sb_decode(page_table, heads(sq_a), heads(gates_a[:, :W_Q]),
                        pages(cache_sb_k), pages(cache_sb_v), uo_page)
    merged_dec = _fox_decode(page_table, heads(fq_a), kv_heads(fkb_a), kv_heads(fvb_a),
                             lf_a[N_META:N_META + dec].reshape(dec, N_Q_HEADS, 1),
                             heads(gates_a[:, W_Q:]), sb_dec,
                             pages(cache_fox_k), pages(cache_fox_v),
                             jnp.swapaxes(cache_fox_logf[0], 1, 2), us_page)

    tail = jnp.zeros((AUX_ROWS - N_META - dec, W_Q), BF16)
    merged_a = jnp.concatenate([merged_meta, merged_dec.reshape(dec, W_Q), tail], axis=0)

    y_main = _finish_rows(x_main, merged_m, wts)
    y_aux = _finish_rows(x_aux, merged_a, wts)

    def prompt_state(a_aux, a_main, tail_shape):
        meta = jnp.broadcast_to(a_aux[None, :N_META], (batch, N_META, a_aux.shape[1]))
        full = jnp.concatenate([meta, a_main.reshape(batch, seq, a_aux.shape[1])], axis=1)
        return full.reshape((1, batch, seq + N_META) + tail_shape)

    kv_shape = (N_KV_HEADS, HEAD_DIM)
    kv_state = lambda st: st.reshape((1, batch, seq + N_META) + kv_shape)
    sample_state = lambda a, tail_shape: a[N_META:N_META + dec].reshape((1, dec, 1) + tail_shape)
    return (y_main.reshape(batch, seq, D_MODEL), y_aux[N_META:N_META + dec].reshape(dec, 1, D_MODEL),
            kv_state(sk_st), kv_state(sv_st), kv_state(fk_st), kv_state(fv_st),
            prompt_state(lf_a, lf_m, (N_Q_HEADS,)),
            sample_state(sk_a, kv_shape), sample_state(sv_a, kv_shape),
            sample_state(fk_a, kv_shape), sample_state(fv_a, kv_shape),
            sample_state(lf_a, (N_Q_HEADS,)))
```

```python
import functools

import jax
import jax.numpy as jnp
from jax import lax
from jax.experimental import pallas as pl
from jax.experimental.pallas import tpu as pltpu

D_MODEL = 2048
HEAD_DIM = 128
N_Q_HEADS = 16
N_KV_HEADS = 4
GROUP = 4
W_Q = N_Q_HEADS * HEAD_DIM
W_KV = N_KV_HEADS * HEAD_DIM
D_FF = 4 * D_MODEL
N_META = 16
PAGE_SIZE = 128
EPS = 1e-6
LOG2E = 1.4426950408889634
Q_SCALE = HEAD_DIM ** -0.5 * LOG2E
FORGET_OFFSET = 2 * W_Q + 4 * W_KV
AUX_ROWS = 32

BF16 = jnp.bfloat16
F32 = jnp.float32

LANES = 128
SUBLANES = 8
NEG_BIG = -1e30
SB_DEAD = 104.0 * LOG2E
VMEM_LIMIT = 48 * 1024 * 1024
FFN_VMEM_LIMIT = 58 * 1024 * 1024
TM_MAIN = 1024
TM_PROJ = 2048
TN = 512

_NT = (((1,), (1,)), ((), ()))


def _params(*sem):
    return pltpu.CompilerParams(dimension_semantics=sem, vmem_limit_bytes=VMEM_LIMIT)


def _rms(x, g):
    ms = jnp.mean(x * x, axis=-1, keepdims=True)
    return x * lax.rsqrt(ms + EPS) * g


def _softplus(z):
    return jnp.maximum(z, 0.0) + jnp.log1p(jnp.exp(-jnp.abs(z)))


def _softplus2(z2):
    return jnp.maximum(z2, 0.0) + jnp.log2(1.0 + jnp.exp2(-jnp.abs(z2)))


def _split2(x):
    hi = x.astype(BF16)
    lo = (x - hi.astype(F32)).astype(BF16)
    return hi, lo


def _split3(x):
    hi = x.astype(BF16)
    r = x - hi.astype(F32)
    mid = r.astype(BF16)
    lo = (r - mid.astype(F32)).astype(BF16)
    return hi, mid, lo


def _dot(a, b):
    return jnp.dot(a, b, preferred_element_type=F32)


def _dot_t(a, bt):
    return lax.dot_general(a, bt, _NT, preferred_element_type=F32)


def _split_dot(parts, m_stacked):
    return _dot(jnp.concatenate(parts, axis=1), m_stacked)


def _rms_forget_kernel(x_ref, g_ref, whi_ref, wlo_ref, b_ref, h_ref, lf_ref):
    h = _rms(x_ref[...], g_ref[...])
    h_ref[...] = h.astype(h_ref.dtype)
    hh, hl = _split2(h)
    z = _dot_t(hh, whi_ref[...]) + _dot_t(hl, whi_ref[...]) + _dot_t(hh, wlo_ref[...])
    z = z[:, :N_Q_HEADS] + b_ref[...]
    lf_ref[...] = -_softplus(-z)


def _rms_forget(x, g, w_hi, w_lo, b, tm):
    rows = x.shape[0]
    const = lambda shape: pl.BlockSpec(shape, lambda i: (0, 0))
    return pl.pallas_call(
        _rms_forget_kernel,
        out_shape=[jax.ShapeDtypeStruct((rows, D_MODEL), BF16),
                   jax.ShapeDtypeStruct((rows, N_Q_HEADS), F32)],
        grid=(rows // tm,),
        in_specs=[pl.BlockSpec((tm, D_MODEL), lambda i: (i, 0)),
                  const((1, D_MODEL)), const((LANES, D_MODEL)), const((LANES, D_MODEL)),
                  const((1, N_Q_HEADS))],
        out_specs=[pl.BlockSpec((tm, D_MODEL), lambda i: (i, 0)),
                   pl.BlockSpec((tm, N_Q_HEADS), lambda i: (i, 0))],
        compiler_params=_params("parallel"),
        name="rms_forget",
    )(x, g, w_hi, w_lo, b)


def _head_norm(acc, g):
    outs = []
    for hh in range(acc.shape[1] // HEAD_DIM):
        xh = acc[:, hh * HEAD_DIM:(hh + 1) * HEAD_DIM]
        outs.append(_rms(xh, g))
    return jnp.concatenate(outs, axis=1)


def _proj_kernel(h_ref, wt_ref, g_ref, o_ref, *, mode):
    acc = _dot_t(h_ref[...], wt_ref[...].astype(BF16))
    if mode == "scale":
        acc = acc * Q_SCALE
    elif mode == "norm_scale":
        acc = _head_norm(acc, g_ref[...]) * Q_SCALE
    elif mode == "sigmoid":
        acc = jax.nn.sigmoid(acc)
    o_ref[...] = acc.astype(o_ref.dtype)


def _wt_rows_spec(row0):
    if row0 % TN == 0:
        return pl.BlockSpec((TN, D_MODEL), lambda i, j: (row0 // TN + j, 0))
    assert row0 % SUBLANES == 0
    return pl.BlockSpec((pl.Element(TN), pl.Element(D_MODEL)),
                        lambda i, j: (pl.multiple_of(row0 + j * TN, SUBLANES), 0))


def _proj(h, wt, row0, n, g, mode, tm, name):
    rows = h.shape[0]
    return pl.pallas_call(
        functools.partial(_proj_kernel, mode=mode),
        out_shape=jax.ShapeDtypeStruct((rows, n), BF16),
        grid=(rows // tm, n // TN),
        in_specs=[pl.BlockSpec((tm, D_MODEL), lambda i, j: (i, 0)),
                  _wt_rows_spec(row0),
                  pl.BlockSpec((1, HEAD_DIM), lambda i, j: (0, 0))],
        out_specs=pl.BlockSpec((tm, TN), lambda i, j: (i, j)),
        compiler_params=_params("parallel", "arbitrary"),
        name=name,
    )(h, wt, g)


def _kv_kernel(h_ref, wt_ref, gk_ref, *out_refs):
    j = pl.program_id(1)
    acc = _dot_t(h_ref[...], wt_ref[...].astype(BF16))
    for jj in range(4):
        @pl.when(j == jj)
        def _(jj=jj):
            val = _head_norm(acc, gk_ref[...]) if jj == 2 else acc
            out_refs[jj][...] = val
            out_refs[4 + jj][...] = val.astype(BF16)


def _kv_wt_spec():
    sb_kv, fox_kv = W_Q // W_KV, (2 * W_Q + 2 * W_KV) // W_KV
    return pl.BlockSpec((W_KV, D_MODEL), lambda i, j: (sb_kv + j + (fox_kv - sb_kv - 2) * (j // 2), 0))


def _kv_state_kernel(h_ref, wt_ref, gk_ref, meta_ref, *refs, tiles_per_batch):
    state = refs[:4]
    bf_refs = refs[4:8]
    stage, sem, meta_sem = refs[8:]
    i, j = pl.program_id(0), pl.program_id(1)
    n_i = pl.num_programs(0)
    tm = h_ref.shape[0]
    b, it = i // tiles_per_batch, i % tiles_per_batch
    row0 = pl.multiple_of(N_META * N_KV_HEADS + it * tm * N_KV_HEADS, SUBLANES)
    acc = _dot_t(h_ref[...], wt_ref[...].astype(BF16))

    def tile_copy(jj, slot):
        return pltpu.make_async_copy(stage.at[slot], state[jj].at[b, pl.ds(row0, tm * N_KV_HEADS), :],
                                     sem.at[slot])

    def meta_copy(jj):
        return pltpu.make_async_copy(meta_ref.at[jj], state[jj].at[b, pl.ds(0, N_META * N_KV_HEADS), :],
                                     meta_sem.at[0])

    for jj in range(4):
        @pl.when(j == jj)
        def _(jj=jj):
            slot = jj % 2
            val = _head_norm(acc, gk_ref[...]) if jj == 2 else acc
            bf_refs[jj][...] = val.astype(BF16)

            @pl.when(jnp.logical_or(i > 0, jj >= 2))
            def _():
                tile_copy(jj, slot).wait()

            for n in range(N_KV_HEADS):
                stage[slot, pl.ds(n, tm, stride=N_KV_HEADS), :] = val[:, n * HEAD_DIM:(n + 1) * HEAD_DIM]
            tile_copy(jj, slot).start()

            @pl.when(it == 0)
            def _():
                meta_copy(jj).start()

    @pl.when(jnp.logical_and(i == n_i - 1, j == 3))
    def _():
        tile_copy(2, 0).wait()
        tile_copy(3, 1).wait()
        for _ in range(4 * (n_i // tiles_per_batch)):
            meta_copy(0).wait()


def _kv_state_proj(h, wt, g_k, meta_rows, tm, batch, seq):
    rows = h.shape[0]
    tiles_per_batch = seq // tm
    state_rows = (N_META + seq) * N_KV_HEADS
    any_spec = pl.BlockSpec(memory_space=pl.ANY)
    return pl.pallas_call(
        functools.partial(_kv_state_kernel, tiles_per_batch=tiles_per_batch),
        out_shape=[jax.ShapeDtypeStruct((batch, state_rows, HEAD_DIM), F32)] * 4
        + [jax.ShapeDtypeStruct((rows, W_KV), BF16)] * 4,
        grid=(rows // tm, 4),
        in_specs=[pl.BlockSpec((tm, D_MODEL), lambda i, j: (i, 0)),
                  _kv_wt_spec(),
                  pl.BlockSpec((1, HEAD_DIM), lambda i, j: (0, 0)),
                  pl.BlockSpec(meta_rows.shape, lambda i, j: (0, 0, 0))],
        out_specs=[any_spec] * 4 + [pl.BlockSpec((tm, W_KV), lambda i, j: (i, 0))] * 4,
        scratch_shapes=[pltpu.VMEM((2, tm * N_KV_HEADS, HEAD_DIM), F32),
                        pltpu.SemaphoreType.DMA((2,)),
                        pltpu.SemaphoreType.DMA((1,))],
        compiler_params=_params("arbitrary", "arbitrary"),
        name="kv_state_proj",
    )(h, wt, g_k, meta_rows)


def _kv_proj(h, wt, g_k, tm):
    rows = h.shape[0]
    blk = pl.BlockSpec((tm, W_KV), lambda i, j: (i, 0))
    return pl.pallas_call(
        _kv_kernel,
        out_shape=[jax.ShapeDtypeStruct((rows, W_KV), F32)] * 4
        + [jax.ShapeDtypeStruct((rows, W_KV), BF16)] * 4,
        grid=(rows // tm, 4),
        in_specs=[pl.BlockSpec((tm, D_MODEL), lambda i, j: (i, 0)),
                  _kv_wt_spec(),
                  pl.BlockSpec((1, HEAD_DIM), lambda i, j: (0, 0))],
        out_specs=[blk] * 8,
        compiler_params=_params("parallel", "arbitrary"),
        name="kv_proj",
    )(h, wt, g_k)


def _lane_cumsum(x):
    n = x.shape[-1]
    lane = lax.broadcasted_iota(jnp.int32, x.shape, x.ndim - 1)
    k = 1
    while k < n:
        x = x + jnp.where(lane >= k, pltpu.roll(x, k, axis=x.ndim - 1), 0.0)
        k *= 2
    return x


def _cum_kernel(meta_ref, main_ref, cmeta_ref, cmain_ref):
    cm = _lane_cumsum(meta_ref[...])
    cmeta_ref[...] = cm * LOG2E
    cmain_ref[...] = (_lane_cumsum(main_ref[...]) + cm[:, LANES - 1:LANES]) * LOG2E


def _cum_forget(lf_meta_t, lf_main_t):
    b, _, s = lf_main_t.shape
    return pl.pallas_call(
        _cum_kernel,
        out_shape=[jax.ShapeDtypeStruct((b, N_Q_HEADS, LANES), F32),
                   jax.ShapeDtypeStruct((b, N_Q_HEADS, s), F32)],
        grid=(b,),
        in_specs=[pl.BlockSpec((N_Q_HEADS, LANES), lambda i: (0, 0)),
                  pl.BlockSpec((None, N_Q_HEADS, s), lambda i: (i, 0, 0))],
        out_specs=[pl.BlockSpec((None, N_Q_HEADS, LANES), lambda i: (i, 0, 0)),
                   pl.BlockSpec((None, N_Q_HEADS, s), lambda i: (i, 0, 0))],
        compiler_params=_params("parallel"),
        name="cum_forget",
    )(lf_meta_t, lf_main_t)


def _stack_heads(q):
    return jnp.concatenate([q[:, g * HEAD_DIM:(g + 1) * HEAD_DIM] for g in range(GROUP)], axis=0)


SB_CHUNK = 512
FOX_CHUNK = 512


def _chunk_visible(rel, c, cr, t, tk):
    col = lax.broadcasted_iota(jnp.int32, (cr, tk), 1)
    row = (lax.broadcasted_iota(jnp.int32, (cr, tk), 0) + c * cr) % t
    return rel(col, row)


def _sb_block(qs, kb, vb, uo, rel, r_ref, acc_ref):
    r, tk = qs.shape[0], kb.shape[0]
    t = r // GROUP
    cr = min(r, SB_CHUNK)
    score = lambda c: lax.dot_general(qs[c * cr:(c + 1) * cr], kb, _NT, preferred_element_type=F32)
    z_next = score(0)
    for c in range(r // cr):
        rows = slice(c * cr, (c + 1) * cr)
        z = z_next
        if c + 1 < r // cr:
            z_next = score(c + 1)
        sp = _softplus2(z)
        if rel is not None:
            vis = _chunk_visible(rel, c, cr, t, tk)
            sp = jnp.where(vis, sp, 0.0)
        ct = _split_dot(_split2(sp), uo)
        r_prev = r_ref[rows]
        a = jnp.exp2(z - ct[:, :tk] - r_prev)
        if rel is not None:
            a = jnp.where(vis, a, 0.0)
        acc_ref[rows] += _dot(a.astype(BF16), vb)
        r_ref[rows] = r_prev + ct[:, tk:]


def _sb_window(qs, kb, vb, uo, r_ref, acc_ref):
    r = qs.shape[0]
    t = r // GROUP
    nb = kb.shape[0] // t
    z = _dot_t(qs, kb)
    sp = _softplus2(z)
    vis = _chunk_visible(lambda c, row: c < row, 0, r, t, t)
    newer = None
    weights = [None] * nb
    for b in reversed(range(nb)):
        cols = slice(b * t, (b + 1) * t)
        diag = b == nb - 1
        spb = jnp.where(vis, sp[:, cols], 0.0) if diag else sp[:, cols]
        ct = _split_dot(_split2(spb), uo)
        arg = z[:, cols] - ct[:, :t]
        a = jnp.exp2(arg if newer is None else arg - newer)
        weights[b] = (jnp.where(vis, a, 0.0) if diag else a).astype(BF16)
        newer = ct[:, t:] if newer is None else newer + ct[:, t:]
    acc_ref[...] = _dot(jnp.concatenate(weights, axis=1), vb)
    r_ref[...] = newer


def _fox_block(qs, kb, vb, ckb, rel, m_ref, acc_ref):
    r, tk = qs.shape[0], kb.shape[0]
    t = r // GROUP
    cr = min(r, FOX_CHUNK)
    hc = max(cr // t, 1)
    v1 = jnp.concatenate([vb, jnp.ones((tk, LANES), BF16)], axis=1)
    score = lambda c: lax.dot_general(qs[c * cr:(c + 1) * cr], kb, _NT, preferred_element_type=F32)
    z_next = score(0)
    for c in range(r // cr):
        rows = slice(c * cr, (c + 1) * cr)
        g = c * cr // t
        bias = ckb[g:g + hc]
        z = (z_next.reshape(hc, cr // hc, tk) - bias[:, None, :]).reshape(cr, tk)
        if c + 1 < r // cr:
            z_next = score(c + 1)
        if rel is not None:
            z = jnp.where(_chunk_visible(rel, c, cr, t, tk), z, NEG_BIG)
        m_prev = m_ref[rows]
        m_new = jnp.maximum(m_prev, jnp.max(z, axis=1, keepdims=True))
        alpha = jnp.exp2(m_prev - m_new)
        p = jnp.exp2(z - jnp.tile(m_new, (1, tk // LANES)))
        acc_ref[rows] = acc_ref[rows] * jnp.tile(alpha, (1, 2)) + _dot(p.astype(BF16), v1)
        m_ref[rows] = m_new


SB_T = 128
SB_WIN = 3
SB_SUB = 2


def _store_heads(o_ref, o, t, gate_ref, add_ref=None, row0=0):
    rows = slice(row0, row0 + t)
    for g in range(GROUP):
        cols = slice(g * HEAD_DIM, (g + 1) * HEAD_DIM)
        val = o[g * t:(g + 1) * t] * gate_ref[rows, cols].astype(F32)
        if add_ref is not None:
            val = val + add_ref[rows, cols].astype(F32)
        o_ref[rows, cols] = val.astype(o_ref.dtype)


def _sb_prompt_kernel(q_ref, k_ref, v_ref, mk_ref, mv_ref, uo_ref, gate_ref, o_ref, r_ref, acc_ref):
    i = pl.program_id(2)
    t = SB_T
    uo = uo_ref[...]
    qs = [_stack_heads(q_ref[sub * t:(sub + 1) * t]) for sub in range(SB_SUB)]
    tile = lambda sub: i * SB_SUB + sub

    def first_block_only(sub):
        r_ref[sub] = jnp.zeros(r_ref.shape[1:], F32)
        acc_ref[sub] = jnp.zeros(acc_ref.shape[1:], F32)
        s = pl.multiple_of(tile(sub) * t, t)
        _sb_block(qs[sub], k_ref[pl.ds(s, t), :], v_ref[pl.ds(s, t), :], uo, lambda c, r: c < r,
                  r_ref.at[sub], acc_ref.at[sub])

    assert SB_SUB >= SB_WIN - 1

    @pl.when(i > 0)
    def _():
        for sub in range(SB_SUB):
            s = pl.multiple_of((tile(sub) - (SB_WIN - 1)) * t, t)
            _sb_window(qs[sub], k_ref[pl.ds(s, SB_WIN * t), :], v_ref[pl.ds(s, SB_WIN * t), :], uo,
                       r_ref.at[sub], acc_ref.at[sub])

    @pl.when(i == 0)
    def _():
        for sub in range(SB_SUB):
            if sub >= SB_WIN - 1:
                _sb_window(qs[sub], k_ref[pl.ds((sub - (SB_WIN - 1)) * t, SB_WIN * t), :],
                           v_ref[pl.ds((sub - (SB_WIN - 1)) * t, SB_WIN * t), :], uo,
                           r_ref.at[sub], acc_ref.at[sub])
            else:
                first_block_only(sub)

    for sub in range(SB_SUB):
        def visit(kb, vb, rel, sub=sub):
            _sb_block(qs[sub], kb, vb, uo, rel, r_ref.at[sub], acc_ref.at[sub])
            return jnp.min(r_ref[sub])

        def cond(c):
            return jnp.logical_and(c[0] >= 0, c[1] < SB_DEAD)

        def body(c, visit=visit):
            s = pl.multiple_of(c[0] * t, t)
            return c[0] - 1, visit(k_ref[pl.ds(s, t), :], v_ref[pl.ds(s, t), :], None)

        covered = jnp.where(jnp.logical_or(i > 0, sub >= SB_WIN - 1), SB_WIN, 1)
        _, mn = lax.while_loop(cond, body, (tile(sub) - covered, jnp.min(r_ref[sub])))

        @pl.when(mn < SB_DEAD)
        def _(visit=visit):
            visit(mk_ref[...], mv_ref[...], lambda c, r: c < N_META)

        _store_heads(o_ref, acc_ref[sub], t, gate_ref, row0=sub * t)


def _sb_prompt(q, k, v, mk, mv, uo, gates, batch, seq):
    t = SB_T * SB_SUB
    nb = seq // t
    tile = pl.BlockSpec((t, GROUP * HEAD_DIM), lambda b, n, i: (b * nb + i, n))
    return pl.pallas_call(
        _sb_prompt_kernel,
        out_shape=jax.ShapeDtypeStruct((batch * seq, W_Q), BF16),
        grid=(batch, N_KV_HEADS, nb),
        in_specs=[tile,
                  pl.BlockSpec((seq, HEAD_DIM), lambda b, n, i: (b, n)),
                  pl.BlockSpec((seq, HEAD_DIM), lambda b, n, i: (b, n)),
                  pl.BlockSpec((LANES, HEAD_DIM), lambda b, n, i: (0, n)),
                  pl.BlockSpec((LANES, HEAD_DIM), lambda b, n, i: (0, n)),
                  pl.BlockSpec(uo.shape, lambda b, n, i: (0, 0)),
                  tile],
        out_specs=tile,
        scratch_shapes=[pltpu.VMEM((SB_SUB, GROUP * SB_T, SB_T), F32),
                        pltpu.VMEM((SB_SUB, GROUP * SB_T, HEAD_DIM), F32)],
        compiler_params=_params("parallel", "parallel", "arbitrary"),
        name="sb_prompt",
    )(q, k, v, mk, mv, uo, gates)


FOX_T = 512
FOX_TK = 512


def _fox_prompt_kernel(q_ref, k_ref, v_ref, ck_ref, mk_ref, mv_ref, mck_ref, gate_ref, sb_ref,
                       o_ref, m_ref, acc_ref):
    i = pl.program_id(2)
    t = FOX_T
    qs = _stack_heads(q_ref[...])
    m_ref[...] = jnp.full_like(m_ref, NEG_BIG)
    acc_ref[...] = jnp.zeros_like(acc_ref)

    def visit(s, width, rel):
        _fox_block(qs, k_ref[pl.ds(s, width), :], v_ref[pl.ds(s, width), :], ck_ref[:, pl.ds(s, width)],
                   rel, m_ref, acc_ref)

    visit(pl.multiple_of(i * t, t), t, lambda c, r: c <= r)

    per_visit = FOX_TK // t

    def body(j, carry):
        visit(pl.multiple_of(j * FOX_TK, FOX_TK), FOX_TK, None)
        return carry

    lax.fori_loop(0, i // per_visit, body, 0)

    for left in range(per_visit - 1):
        @pl.when(i % per_visit > left)
        def _(left=left):
            visit(pl.multiple_of((i - i % per_visit + left) * t, t), t, None)

    _fox_block(qs, mk_ref[...], mv_ref[...], mck_ref[...],
               lambda c, r: c < N_META, m_ref, acc_ref)

    acc = acc_ref[...]
    _store_heads(o_ref, acc[:, :HEAD_DIM] / acc[:, HEAD_DIM:], t, gate_ref, sb_ref)


def _fox_prompt(q, k, v, ck, mk, mv, mck, gates, sb_part, batch, seq):
    t = FOX_T
    nb = seq // t
    tile = pl.BlockSpec((t, GROUP * HEAD_DIM), lambda b, n, i: (b * nb + i, n))
    return pl.pallas_call(
        _fox_prompt_kernel,
        out_shape=jax.ShapeDtypeStruct((batch * seq, W_Q), BF16),
        grid=(batch, N_KV_HEADS, nb),
        in_specs=[tile,
                  pl.BlockSpec((seq, HEAD_DIM), lambda b, n, i: (b, n)),
                  pl.BlockSpec((seq, HEAD_DIM), lambda b, n, i: (b, n)),
                  pl.BlockSpec((None, None, GROUP, seq), lambda b, n, i: (b, n, 0, 0)),
                  pl.BlockSpec((LANES, HEAD_DIM), lambda b, n, i: (0, n)),
                  pl.BlockSpec((LANES, HEAD_DIM), lambda b, n, i: (0, n)),
                  pl.BlockSpec((None, None, GROUP, LANES), lambda b, n, i: (b, n, 0, 0)),
                  pl.BlockSpec((t, GROUP * HEAD_DIM), lambda b, n, i: (b * nb + i, N_KV_HEADS + n)),
                  tile],
        out_specs=tile,
        scratch_shapes=[pltpu.VMEM((GROUP * t, LANES), F32), pltpu.VMEM((GROUP * t, 2 * LANES), F32)],
        compiler_params=_params("parallel", "parallel", "arbitrary"),
        name="fox_prompt",
    )(q, k, v, ck, mk, mv, mck, gates, sb_part)


def _meta_attn_kernel(sq_ref, sk_ref, sv_ref, fq_ref, fk_ref, fv_ref, ck_ref, uo_ref, gsb_ref, gfx_ref,
                      o_ref, m_ref, acc_ref, r_ref, sacc_ref):
    t = N_META
    valid = lambda c: c < N_META
    r_ref[...] = jnp.zeros_like(r_ref)
    sacc_ref[...] = jnp.zeros_like(sacc_ref)
    _sb_block(_stack_heads(sq_ref[...]), sk_ref[...], sv_ref[...], uo_ref[...],
              lambda c, r: jnp.logical_and(c < r, valid(c)), r_ref, sacc_ref)
    o = sacc_ref[...]
    m_ref[...] = jnp.full_like(m_ref, NEG_BIG)
    acc_ref[...] = jnp.zeros_like(acc_ref)
    _fox_block(_stack_heads(fq_ref[...]), fk_ref[...], fv_ref[...], ck_ref[...],
               lambda c, r: jnp.logical_and(c <= r, valid(c)), m_ref, acc_ref)
    acc = acc_ref[...]
    f = acc[:, :HEAD_DIM] / acc[:, HEAD_DIM:]
    for g in range(GROUP):
        cols = slice(g * HEAD_DIM, (g + 1) * HEAD_DIM)
        merged = (gsb_ref[:, cols].astype(F32) * o[g * t:(g + 1) * t]
                  + gfx_ref[:, cols].astype(F32) * f[g * t:(g + 1) * t])
        o_ref[:, cols] = merged.astype(o_ref.dtype)


def _meta_attn(sq, sk, sv, fq, fk, fv, ck, uo, gates):
    qspec = pl.BlockSpec((N_META, GROUP * HEAD_DIM), lambda n: (0, n))
    kspec = pl.BlockSpec((LANES, HEAD_DIM), lambda n: (0, n))
    return pl.pallas_call(
        _meta_attn_kernel,
        out_shape=jax.ShapeDtypeStruct((N_META, W_Q), BF16),
        grid=(N_KV_HEADS,),
        in_specs=[qspec, kspec, kspec, qspec, kspec, kspec,
                  pl.BlockSpec((None, GROUP, LANES), lambda n: (n, 0, 0)),
                  pl.BlockSpec(uo.shape, lambda n: (0, 0)),
                  qspec,
                  pl.BlockSpec((N_META, GROUP * HEAD_DIM), lambda n: (0, N_KV_HEADS + n))],
        out_specs=qspec,
        scratch_shapes=[pltpu.VMEM((GROUP * N_META, LANES), F32),
                        pltpu.VMEM((GROUP * N_META, 2 * LANES), F32),
                        pltpu.VMEM((GROUP * N_META, LANES), F32),
                        pltpu.VMEM((GROUP * N_META, HEAD_DIM), F32)],
        compiler_params=_params("parallel"),
        name="meta_attn",
    )(sq, sk, sv, fq, fk, fv, ck, uo, gates, gates)


PAGE_COLS = PAGE_SIZE * N_KV_HEADS
DEC_G = 16


def _own_head(shape):
    row = lax.broadcasted_iota(jnp.int32, shape, 0)
    col = lax.broadcasted_iota(jnp.int32, shape, 1)
    return (col % N_KV_HEADS) == (row // GROUP)


def _sb_decode_kernel(pt_ref, q_ref, gate_ref, ck_hbm, cv_hbm, uo_ref, o_ref, kbuf, vbuf, sem, r_ref, acc_ref):
    b = pl.program_id(0)
    n_pages = pt_ref.shape[1]
    q = q_ref[...]
    uo = uo_ref[...]
    own = _own_head((N_Q_HEADS, PAGE_COLS))

    def copies(p, slot):
        pid = pt_ref[b, p]
        return [pltpu.make_async_copy(ck_hbm.at[pid], kbuf.at[slot], sem.at[0, slot]),
                pltpu.make_async_copy(cv_hbm.at[pid], vbuf.at[slot], sem.at[1, slot])]

    r_ref[...] = jnp.zeros_like(r_ref)
    acc_ref[...] = jnp.zeros_like(acc_ref)
    for c in copies(n_pages - 1, (n_pages - 1) % 2):
        c.start()

    def cond(c):
        return jnp.logical_and(c[0] >= 0, c[1] < SB_DEAD)

    def body(c):
        p = c[0]
        slot = p % 2
        for cp in copies(p, slot):
            cp.wait()

        @pl.when(p > 0)
        def _():
            for cp in copies(p - 1, 1 - slot):
                cp.start()

        z = lax.dot_general(q, kbuf[slot].astype(BF16), _NT, preferred_element_type=F32)
        sp = jnp.where(own, _softplus2(z), 0.0)
        ct = _split_dot(_split2(sp), uo)
        r = r_ref[...]
        a = jnp.where(own, jnp.exp2(z - ct[:, :PAGE_COLS] - jnp.tile(r, (1, N_KV_HEADS))), 0.0)
        acc_ref[...] += _dot(a.astype(BF16), vbuf[slot].astype(BF16))
        r_new = r + ct[:, PAGE_COLS:]
        r_ref[...] = r_new
        return p - 1, jnp.min(r_new)

    p_end, _ = lax.while_loop(cond, body, (jnp.int32(n_pages - 1), jnp.float32(0.0)))

    @pl.when(p_end >= 0)
    def _():
        for cp in copies(p_end, p_end % 2):
            cp.wait()

    o_ref[...] = (acc_ref[...] * gate_ref[...].astype(F32)).astype(o_ref.dtype)


def _sb_decode(page_table, q, gate, cache_k, cache_v, uo):
    nb = q.shape[0]
    any_spec = pl.BlockSpec(memory_space=pl.ANY)
    head_spec = pl.BlockSpec((None, N_Q_HEADS, HEAD_DIM), lambda b, pt: (b, 0, 0))
    return pl.pallas_call(
        _sb_decode_kernel,
        out_shape=jax.ShapeDtypeStruct((nb, N_Q_HEADS, HEAD_DIM), BF16),
        grid_spec=pltpu.PrefetchScalarGridSpec(
            num_scalar_prefetch=1,
            grid=(nb,),
            in_specs=[head_spec, head_spec, any_spec, any_spec,
                      pl.BlockSpec(uo.shape, lambda b, pt: (0, 0))],
            out_specs=head_spec,
            scratch_shapes=[pltpu.VMEM((2, PAGE_COLS, HEAD_DIM), F32),
                            pltpu.VMEM((2, PAGE_COLS, HEAD_DIM), F32),
                            pltpu.SemaphoreType.DMA((2, 2)),
                            pltpu.VMEM((N_Q_HEADS, LANES), F32),
                            pltpu.VMEM((N_Q_HEADS, HEAD_DIM), F32)]),
        compiler_params=_params("arbitrary"),
        name="sb_decode",
    )(page_table, q, gate, cache_k, cache_v, uo)


def _fox_decode_kernel(pt_ref, q_ref, kn_ref, vn_ref, lfn_ref, gate_ref, sb_ref, ck_hbm, cv_hbm, lf_hbm,
                       us_ref, o_ref, kbuf, vbuf, lbuf, sem, m_ref, acc_ref, d_ref):
    n_seq, n_pages = pt_ref.shape
    n_grp = n_pages // DEC_G
    total = n_seq * n_grp
    us = us_ref[...]
    bias = jnp.where(_own_head((N_Q_HEADS, PAGE_COLS)), 0.0, NEG_BIG)

    def copies(t, slot):
        b = t // n_grp
        newest = n_pages - 1 - (t % n_grp) * DEC_G
        out = []
        for g in range(DEC_G):
            pid = pt_ref[b, newest - g]
            out += [pltpu.make_async_copy(ck_hbm.at[pid], kbuf.at[slot, g], sem.at[0, slot]),
                    pltpu.make_async_copy(cv_hbm.at[pid], vbuf.at[slot, g], sem.at[1, slot]),
                    pltpu.make_async_copy(lf_hbm.at[pid], lbuf.at[slot, g], sem.at[2, slot])]
        return out

    for cp in copies(0, 0):
        cp.start()

    def body(t, carry):
        slot = t % 2
        b = t // n_grp
        gi = t % n_grp

        @pl.when(t + 1 < total)
        def _():
            for cp in copies(t + 1, 1 - slot):
                cp.start()

        q = q_ref[b]

        @pl.when(gi == 0)
        def _():
            z_self = jnp.sum(q.astype(F32) * kn_ref[b].astype(F32), axis=1, keepdims=True)
            m_ref[...] = jnp.broadcast_to(z_self, m_ref.shape)
            acc_ref[...] = jnp.concatenate([vn_ref[b].astype(F32), jnp.ones((N_Q_HEADS, LANES), F32)], axis=1)
            d_ref[...] = jnp.broadcast_to(lfn_ref[b] * LOG2E, d_ref.shape)

        for cp in copies(t, slot):
            cp.wait()

        dt = _split_dot(_split3(lbuf[slot].reshape(DEC_G * N_Q_HEADS, PAGE_SIZE)), us) * LOG2E
        d = d_ref[...]
        zs = []
        for g in range(DEC_G):
            dg = dt[g * N_Q_HEADS:(g + 1) * N_Q_HEADS]
            z = lax.dot_general(q, kbuf[slot, g].astype(BF16), _NT, preferred_element_type=F32)
            zs.append(z + dg[:, :PAGE_COLS] + jnp.tile(d, (1, N_KV_HEADS)) + bias)
            d = d + dg[:, PAGE_COLS:]
        d_ref[...] = d
        z = jnp.concatenate(zs, axis=1)
        m_prev = m_ref[...]
        m_new = jnp.maximum(m_prev, jnp.max(z, axis=1, keepdims=True))
        alpha = jnp.exp2(m_prev - m_new)
        p = jnp.exp2(z - jnp.tile(m_new, (1, z.shape[1] // LANES)))
        lsum = jnp.sum(p, axis=1, keepdims=True)
        pb = p.astype(BF16)
        pv = _dot(pb[:, :PAGE_COLS], vbuf[slot, 0].astype(BF16))
        for g in range(1, DEC_G):
            pv += _dot(pb[:, g * PAGE_COLS:(g + 1) * PAGE_COLS], vbuf[slot, g].astype(BF16))
        acc = acc_ref[...] * jnp.tile(alpha, (1, 2)) + jnp.concatenate(
            [pv, jnp.broadcast_to(lsum, pv.shape)], axis=1)
        acc_ref[...] = acc
        m_ref[...] = m_new

        @pl.when(gi == n_grp - 1)
        def _():
            merged = acc[:, :HEAD_DIM] / acc[:, HEAD_DIM:] * gate_ref[b].astype(F32) + sb_ref[b].astype(F32)
            o_ref[b] = merged.astype(o_ref.dtype)

        return carry

    lax.fori_loop(0, total, body, 0)


def _fox_decode(page_table, q, k_new, v_new, lf_new, gate, sb_part, cache_k, cache_v, cache_lf_t, us):
    nb = q.shape[0]
    assert page_table.shape[1] % DEC_G == 0
    any_spec = pl.BlockSpec(memory_space=pl.ANY)
    head_spec = pl.BlockSpec((nb, N_Q_HEADS, HEAD_DIM), lambda i, pt: (0, 0, 0))
    return pl.pallas_call(
        _fox_decode_kernel,
        out_shape=jax.ShapeDtypeStruct((nb, N_Q_HEADS, HEAD_DIM), BF16),
        grid_spec=pltpu.PrefetchScalarGridSpec(
            num_scalar_prefetch=1,
            grid=(1,),
            in_specs=[head_spec, head_spec, head_spec,
                      pl.BlockSpec((nb, N_Q_HEADS, 1), lambda i, pt: (0, 0, 0)),
                      head_spec, head_spec,
                      any_spec, any_spec, any_spec,
                      pl.BlockSpec(us.shape, lambda i, pt: (0, 0))],
            out_specs=head_spec,
            scratch_shapes=[pltpu.VMEM((2, DEC_G, PAGE_COLS, HEAD_DIM), F32),
                            pltpu.VMEM((2, DEC_G, PAGE_COLS, HEAD_DIM), F32),
                            pltpu.VMEM((2, DEC_G, N_Q_HEADS, PAGE_SIZE), F32),
                            pltpu.SemaphoreType.DMA((3, 2)),
                            pltpu.VMEM((N_Q_HEADS, LANES), F32),
                            pltpu.VMEM((N_Q_HEADS, 2 * LANES), F32),
                            pltpu.VMEM((N_Q_HEADS, LANES), F32)]),
        compiler_params=_params("arbitrary"),
        name="fox_decode",
    )(page_table, q, k_new, v_new, lf_new, gate, sb_part, cache_k, cache_v, cache_lf_t, us)


def _out_proj_kernel(m_ref, w_ref, x_ref, y_ref):
    y_ref[...] = x_ref[...] + _dot(m_ref[...], w_ref[...])


def _out_proj(merged, w, x, tm):
    rows = x.shape[0]
    return pl.pallas_call(
        _out_proj_kernel,
        out_shape=jax.ShapeDtypeStruct((rows, D_MODEL), F32),
        grid=(rows // tm, D_MODEL // TN),
        in_specs=[pl.BlockSpec((tm, W_Q), lambda i, j: (i, 0)),
                  pl.BlockSpec((W_Q, TN), lambda i, j: (0, j)),
                  pl.BlockSpec((tm, TN), lambda i, j: (i, j))],
        out_specs=pl.BlockSpec((tm, TN), lambda i, j: (i, j)),
        compiler_params=_params("parallel", "arbitrary"),
        name="out_proj",
    )(merged, w, x)


def _ffn_kernel(y_ref, g_ref, wu_ref, wd_ref, o_ref, h_ref):
    @pl.when(pl.program_id(1) == 0)
    def _():
        y = y_ref[...]
        h_ref[...] = _rms(y, g_ref[...]).astype(BF16)
        o_ref[...] = y

    u = jnp.square(jnp.maximum(_dot(h_ref[...], wu_ref[...]), 0.0))
    o_ref[...] += _dot(u.astype(BF16), wd_ref[...])


def _ffn(y, g, w_up, w_down, tm):
    rows = y.shape[0]
    return pl.pallas_call(
        _ffn_kernel,
        out_shape=jax.ShapeDtypeStruct((rows, D_MODEL), F32),
        grid=(rows // tm, D_FF // TN),
        in_specs=[pl.BlockSpec((tm, D_MODEL), lambda i, k: (i, 0), pipeline_mode=pl.Buffered(1)),
                  pl.BlockSpec((1, D_MODEL), lambda i, k: (0, 0)),
                  pl.BlockSpec((D_MODEL, TN), lambda i, k: (0, k)),
                  pl.BlockSpec((TN, D_MODEL), lambda i, k: (k, 0))],
        out_specs=pl.BlockSpec((tm, D_MODEL), lambda i, k: (i, 0)),
        scratch_shapes=[pltpu.VMEM((tm, D_MODEL), BF16)],
        compiler_params=pltpu.CompilerParams(dimension_semantics=("parallel", "arbitrary"),
                                             vmem_limit_bytes=FFN_VMEM_LIMIT),
        name="ffn",
    )(y, g, w_up, w_down)


def _row_tile(rows, preferred):
    return preferred if rows % preferred == 0 else rows


def _project_rows(x, wts, prompt_shape=None, kv_meta=None):
    rows = x.shape[0]
    tm = _row_tile(rows, TM_PROJ)
    h, lf = _rms_forget(x, wts["g_mix"], wts["w_fl_hi"], wts["w_fl_lo"], wts["b_f"], _row_tile(rows, TM_MAIN // 2))
    wt = wts["w_in_t"]
    sq = _proj(h, wt, 0, W_Q, wts["g_q"], "scale", tm, "proj_sq")
    fq = _proj(h, wt, W_Q + 2 * W_KV, W_Q, wts["g_q"], "norm_scale", tm, "proj_fq")
    gates = _proj(h, wt, FORGET_OFFSET + N_Q_HEADS, 2 * D_MODEL, wts["g_q"], "sigmoid", tm, "proj_gates")
    if prompt_shape is None:
        kv = _kv_proj(h, wt, wts["g_k"], _row_tile(rows, TM_MAIN))
    else:
        kv = _kv_state_proj(h, wt, wts["g_k"], kv_meta, TM_MAIN, *prompt_shape)
    return sq, fq, gates, kv, lf


def _finish_rows(x, merged, wts):
    rows = x.shape[0]
    y1 = _out_proj(merged, wts["w_out"], x, _row_tile(rows, TM_PROJ))
    return _ffn(y1, wts["g_ffn"], wts["w_up"], wts["w_down"], _row_tile(rows, TM_MAIN))


def kernel(x_prompt, x_sample, cache_sb_k, cache_sb_v, cache_fox_k, cache_fox_v, cache_fox_logf,
           page_table, meta_tokens, g_mix, w_in, b_forget, g_q, g_k, w_out, g_ffn, w_up, w_down):
    batch, seq, _ = x_prompt.shape
    dec = x_sample.shape[0]
    assert w_in.shape[0] == 1 and x_sample.shape[1] == 1 and N_META + dec <= AUX_ROWS
    pool = cache_sb_k.shape[1]

    w_in_t = jnp.swapaxes(w_in[0], 0, 1)
    w_fl = jnp.pad(w_in_t[FORGET_OFFSET:FORGET_OFFSET + N_Q_HEADS], ((0, LANES - N_Q_HEADS), (0, 0)))
    w_fl_hi = w_fl.astype(BF16)
    wts = {
        "g_mix": g_mix, "g_q": g_q, "g_k": g_k, "g_ffn": g_ffn, "b_f": b_forget,
        "w_in_t": w_in_t,
        "w_fl_hi": w_fl_hi, "w_fl_lo": (w_fl - w_fl_hi.astype(F32)).astype(BF16),
        "w_out": w_out[0].astype(BF16), "w_up": w_up[0].astype(BF16), "w_down": w_down[0].astype(BF16),
    }

    x_main = x_prompt.reshape(batch * seq, D_MODEL)
    x_aux = jnp.concatenate([meta_tokens, x_sample.reshape(dec, D_MODEL),
                             jnp.zeros((AUX_ROWS - N_META - dec, D_MODEL), F32)], axis=0)

    sq_a, fq_a, gates_a, kv_a, lf_a = _project_rows(x_aux, wts)
    sk_a, sv_a, fk_a, fv_a, skb_a, svb_a, fkb_a, fvb_a = kv_a
    kv_meta = jnp.stack([a[:N_META].reshape(N_META * N_KV_HEADS, HEAD_DIM) for a in (sk_a, sv_a, fk_a, fv_a)])
    sq_m, fq_m, gates_m, kv_m, lf_m = _project_rows(x_main, wts, (batch, seq), kv_meta)
    sk_st, sv_st, fk_st, fv_st, skb_m, svb_m, fkb_m, fvb_m = kv_m

    lf_meta_t = jnp.pad(lf_a[:N_META].T, ((0, 0), (0, LANES - N_META)))
    lf_main_t = jnp.swapaxes(lf_m.reshape(batch, seq, N_Q_HEADS), 1, 2)
    ck_meta, ck_main = _cum_forget(lf_meta_t, lf_main_t)
    ck_meta = ck_meta.reshape(batch, N_KV_HEADS, GROUP, LANES)
    ck_main = ck_main.reshape(batch, N_KV_HEADS, GROUP, seq)

    pad_keys = lambda a: jnp.pad(a[:N_META], ((0, LANES - N_META), (0, 0)))
    mk_sb, mv_sb, mk_fx, mv_fx = pad_keys(skb_a), pad_keys(svb_a), pad_keys(fkb_a), pad_keys(fvb_a)

    j_idx = lax.broadcasted_iota(jnp.int32, (LANES, LANES), 0)
    s_idx = lax.broadcasted_iota(jnp.int32, (LANES, LANES), 1)
    ones = jnp.ones((LANES, LANES), BF16)
    uo = jnp.concatenate([(j_idx >= s_idx).astype(BF16), ones], axis=1)
    uo = jnp.concatenate([uo] * 2, axis=0)
    c_src = lax.broadcasted_iota(jnp.int32, (PAGE_COLS, PAGE_COLS), 0) // N_KV_HEADS
    c_dst = lax.broadcasted_iota(jnp.int32, (PAGE_COLS, PAGE_COLS), 1) // N_KV_HEADS
    uo_page = jnp.concatenate([(c_src >= c_dst).astype(BF16), jnp.ones((PAGE_COLS, LANES), BF16)], axis=1)
    uo_page = jnp.concatenate([uo_page] * 2, axis=0)
    j_key = lax.broadcasted_iota(jnp.int32, (PAGE_SIZE, PAGE_COLS), 0)
    c_key = lax.broadcasted_iota(jnp.int32, (PAGE_SIZE, PAGE_COLS), 1) // N_KV_HEADS
    us_page = jnp.concatenate([(j_key > c_key).astype(BF16), ones], axis=1)
    us_page = jnp.concatenate([us_page] * 3, axis=0)

    sb_m = _sb_prompt(sq_m, skb_m, svb_m, mk_sb, mv_sb, uo, gates_m, batch, seq)
    merged_m = _fox_prompt(fq_m, fkb_m, fvb_m, ck_main, mk_fx, mv_fx, ck_meta, gates_m, sb_m, batch, seq)
    merged_meta = _meta_attn(sq_a[:N_META], mk_sb, mv_sb, fq_a[:N_META], mk_fx, mv_fx,
                             ck_meta[0], uo, gates_a[:N_META])

    heads = lambda a: a[N_META:N_META + dec].reshape(dec, N_Q_HEADS, HEAD_DIM)
    kv_heads = lambda a: jnp.repeat(a[N_META:N_META + dec].reshape(dec, N_KV_HEADS, HEAD_DIM), GROUP, axis=1)
    pages = lambda c: c[0].reshape(pool, PAGE_COLS, HEAD_DIM)
    sb_dec = _sb_decode(page_table, heads(sq_a), heads(gates_a[:, :W_Q]),
                        pages(cache_sb_k), pages(cache_sb_v), uo_page)
    merged_dec = _fox_decode(page_table, heads(fq_a), kv_heads(fkb_a), kv_heads(fvb_a),
                             lf_a[N_META:N_META + dec].reshape(dec, N_Q_HEADS, 1),
                             heads(gates_a[:, W_Q:]), sb_dec,
                             pages(cache_fox_k), pages(cache_fox_v),
                             jnp.swapaxes(cache_fox_logf[0], 1, 2), us_page)

    tail = jnp.zeros((AUX_ROWS - N_META - dec, W_Q), BF16)
    merged_a = jnp.concatenate([merged_meta, merged_dec.reshape(dec, W_Q), tail], axis=0)

    y_main = _finish_rows(x_main, merged_m, wts)
    y_aux = _finish_rows(x_aux, merged_a, wts)

    def prompt_state(a_aux, a_main, tail_shape):
        meta = jnp.broadcast_to(a_aux[None, :N_META], (batch, N_META, a_aux.shape[1]))
        full = jnp.concatenate([meta, a_main.reshape(batch, seq, a_aux.shape[1])], axis=1)
        return full.reshape((1, batch, seq + N_META) + tail_shape)

    kv_shape = (N_KV_HEADS, HEAD_DIM)
    kv_state = lambda st: st.reshape((1, batch, seq + N_META) + kv_shape)
    sample_state = lambda a, tail_shape: a[N_META:N_META + dec].reshape((1, dec, 1) + tail_shape)
    return (y_main.reshape(batch, seq, D_MODEL), y_aux[N_META:N_META + dec].reshape(dec, 1, D_MODEL),
            kv_state(sk_st), kv_state(sv_st), kv_state(fk_st), kv_state(fv_st),
            prompt_state(lf_a, lf_m, (N_Q_HEADS,)),
            sample_state(sk_a, kv_shape), sample_state(sv_a, kv_shape),
            sample_state(fk_a, kv_shape), sample_state(fv_a, kv_shape),
            sample_state(lf_a, (N_Q_HEADS,)))
```

```python
import functools

import jax
import jax.numpy as jnp
from jax import lax
from jax.experimental import pallas as pl
from jax.experimental.pallas import tpu as pltpu

D_MODEL = 2048
HEAD_DIM = 128
N_Q_HEADS = 16
N_KV_HEADS = 4
GROUP = 4
W_Q = N_Q_HEADS * HEAD_DIM
W_KV = N_KV_HEADS * HEAD_DIM
D_FF = 4 * D_MODEL
N_META = 16
PAGE_SIZE = 128
EPS = 1e-6
LOG2E = 1.4426950408889634
Q_SCALE = HEAD_DIM ** -0.5 * LOG2E
FORGET_OFFSET = 2 * W_Q + 4 * W_KV
AUX_ROWS = 32

BF16 = jnp.bfloat16
F32 = jnp.float32

LANES = 128
SUBLANES = 8
NEG_BIG = -1e30
SB_DEAD = 104.0 * LOG2E
VMEM_LIMIT = 48 * 1024 * 1024
FFN_VMEM_LIMIT = 58 * 1024 * 1024
TM_MAIN = 1024
TM_PROJ = 2048
TN = 512

_NT = (((1,), (1,)), ((), ()))


def _params(*sem):
    return pltpu.CompilerParams(dimension_semantics=sem, vmem_limit_bytes=VMEM_LIMIT)


def _rms(x, g):
    ms = jnp.mean(x * x, axis=-1, keepdims=True)
    return x * lax.rsqrt(ms + EPS) * g


def _softplus(z):
    return jnp.maximum(z, 0.0) + jnp.log1p(jnp.exp(-jnp.abs(z)))


def _softplus2(z2):
    return jnp.maximum(z2, 0.0) + jnp.log2(1.0 + jnp.exp2(-jnp.abs(z2)))


def _split2(x):
    hi = x.astype(BF16)
    lo = (x - hi.astype(F32)).astype(BF16)
    return hi, lo


def _split3(x):
    hi = x.astype(BF16)
    r = x - hi.astype(F32)
    mid = r.astype(BF16)
    lo = (r - mid.astype(F32)).astype(BF16)
    return hi, mid, lo


def _dot(a, b):
    return jnp.dot(a, b, preferred_element_type=F32)


def _dot_t(a, bt):
    return lax.dot_general(a, bt, _NT, preferred_element_type=F32)


def _split_dot(parts, m_stacked):
    return _dot(jnp.concatenate(parts, axis=1), m_stacked)


def _rms_forget_kernel(x_ref, g_ref, whi_ref, wlo_ref, b_ref, h_ref, lf_ref):
    h = _rms(x_ref[...], g_ref[...])
    h_ref[...] = h.astype(h_ref.dtype)
    hh, hl = _split2(h)
    z = _dot_t(hh, whi_ref[...]) + _dot_t(hl, whi_ref[...]) + _dot_t(hh, wlo_ref[...])
    z = z[:, :N_Q_HEADS] + b_ref[...]
    lf_ref[...] = -_softplus(-z)


def _rms_forget(x, g, w_hi, w_lo, b, tm):
    rows = x.shape[0]
    const = lambda shape: pl.BlockSpec(shape, lambda i: (0, 0))
    return pl.pallas_call(
        _rms_forget_kernel,
        out_shape=[jax.ShapeDtypeStruct((rows, D_MODEL), BF16),
                   jax.ShapeDtypeStruct((rows, N_Q_HEADS), F32)],
        grid=(rows // tm,),
        in_specs=[pl.BlockSpec((tm, D_MODEL), lambda i: (i, 0)),
                  const((1, D_MODEL)), const((LANES, D_MODEL)), const((LANES, D_MODEL)),
                  const((1, N_Q_HEADS))],
        out_specs=[pl.BlockSpec((tm, D_MODEL), lambda i: (i, 0)),
                   pl.BlockSpec((tm, N_Q_HEADS), lambda i: (i, 0))],
        compiler_params=_params("parallel"),
        name="rms_forget",
    )(x, g, w_hi, w_lo, b)


def _head_norm(acc, g):
    outs = []
    for hh in range(acc.shape[1] // HEAD_DIM):
        xh = acc[:, hh * HEAD_DIM:(hh + 1) * HEAD_DIM]
        outs.append(_rms(xh, g))
    return jnp.concatenate(outs, axis=1)


def _proj_kernel(h_ref, ha_ref, wt_ref, g_ref, o_ref, oa_ref, *, mode):
    wt = wt_ref[...].astype(BF16)
    for src, dst in ((h_ref, o_ref), (ha_ref, oa_ref)):
        acc = _dot_t(src[...], wt)
        if mode == "scale":
            acc = acc * Q_SCALE
        elif mode == "norm_scale":
            acc = _head_norm(acc, g_ref[...]) * Q_SCALE
        elif mode == "sigmoid":
            acc = jax.nn.sigmoid(acc)
        dst[...] = acc.astype(dst.dtype)


def _wt_rows_spec(row0):
    if row0 % TN == 0:
        return pl.BlockSpec((TN, D_MODEL), lambda i, j: (row0 // TN + j, 0))
    assert row0 % SUBLANES == 0
    return pl.BlockSpec((pl.Element(TN), pl.Element(D_MODEL)),
                        lambda i, j: (pl.multiple_of(row0 + j * TN, SUBLANES), 0))


def _proj(h, h_aux, wt, row0, n, g, mode, tm, name):
    rows = h.shape[0]
    out, out_aux = pl.pallas_call(
        functools.partial(_proj_kernel, mode=mode),
        out_shape=[jax.ShapeDtypeStruct((rows, n), BF16),
                   jax.ShapeDtypeStruct((rows // tm, AUX_ROWS, n), BF16)],
        grid=(rows // tm, n // TN),
        in_specs=[pl.BlockSpec((tm, D_MODEL), lambda i, j: (i, 0)),
                  pl.BlockSpec((AUX_ROWS, D_MODEL), lambda i, j: (0, 0)),
                  _wt_rows_spec(row0),
                  pl.BlockSpec((1, HEAD_DIM), lambda i, j: (0, 0))],
        out_specs=[pl.BlockSpec((tm, TN), lambda i, j: (i, j)),
                   pl.BlockSpec((None, AUX_ROWS, TN), lambda i, j: (i, 0, j))],
        compiler_params=_params("parallel", "arbitrary"),
        name=name,
    )(h, h_aux, wt, g)
    return out, out_aux[0]


def _kv_wt_spec():
    sb_kv, fox_kv = W_Q // W_KV, (2 * W_Q + 2 * W_KV) // W_KV
    return pl.BlockSpec((W_KV, D_MODEL), lambda i, j: (sb_kv + j + (fox_kv - sb_kv - 2) * (j // 2), 0))


def _kv_state_kernel(h_ref, ha_ref, wt_ref, gk_ref, *refs, tiles_per_batch):
    state, bf_refs, aux_f32, aux_bf = refs[:4], refs[4:8], refs[8:12], refs[12:16]
    stage, meta_stage, sem, meta_sem = refs[16:]
    i, j = pl.program_id(0), pl.program_id(1)
    n_i = pl.num_programs(0)
    tm = h_ref.shape[0]
    b, it = i // tiles_per_batch, i % tiles_per_batch
    row0 = pl.multiple_of(N_META * N_KV_HEADS + it * tm * N_KV_HEADS, SUBLANES)
    wt = wt_ref[...].astype(BF16)
    acc = _dot_t(h_ref[...], wt)
    acc_aux = _dot_t(ha_ref[...], wt)

    def tile_copy(jj, slot):
        return pltpu.make_async_copy(stage.at[slot], state[jj].at[b, pl.ds(row0, tm * N_KV_HEADS), :],
                                     sem.at[slot])

    def meta_copy(jj):
        return pltpu.make_async_copy(meta_stage.at[jj], state[jj].at[b, pl.ds(0, N_META * N_KV_HEADS), :],
                                     meta_sem.at[jj])

    def interleave(dst, slot, val, rows):
        for n in range(N_KV_HEADS):
            dst[slot, pl.ds(n, rows, stride=N_KV_HEADS), :] = val[:rows, n * HEAD_DIM:(n + 1) * HEAD_DIM]

    for jj in range(4):
        @pl.when(j == jj)
        def _(jj=jj):
            slot = jj % 2
            finish = (lambda a: _head_norm(a, gk_ref[...])) if jj == 2 else (lambda a: a)
            val, val_aux = finish(acc), finish(acc_aux)
            bf_refs[jj][...] = val.astype(BF16)
            aux_f32[jj][...] = val_aux
            aux_bf[jj][...] = val_aux.astype(BF16)

            @pl.when(jnp.logical_or(i > 0, jj >= 2))
            def _():
                tile_copy(jj, slot).wait()

            interleave(stage, slot, val, tm)
            tile_copy(jj, slot).start()

            @pl.when(it == 0)
            def _():
                @pl.when(b > 0)
                def _():
                    meta_copy(jj).wait()

                interleave(meta_stage, jj, val_aux, N_META)
                meta_copy(jj).start()

    @pl.when(jnp.logical_and(i == n_i - 1, j == 3))
    def _():
        tile_copy(2, 0).wait()
        tile_copy(3, 1).wait()
        for jj in range(4):
            meta_copy(jj).wait()


def _kv_state_proj(h, h_aux, wt, g_k, tm, batch, seq):
    rows = h.shape[0]
    tiles_per_batch = seq // tm
    state_rows = (N_META + seq) * N_KV_HEADS
    any_spec = pl.BlockSpec(memory_space=pl.ANY)
    aux_spec = pl.BlockSpec((AUX_ROWS, W_KV), lambda i, j: (0, 0))
    return pl.pallas_call(
        functools.partial(_kv_state_kernel, tiles_per_batch=tiles_per_batch),
        out_shape=[jax.ShapeDtypeStruct((batch, state_rows, HEAD_DIM), F32)] * 4
        + [jax.ShapeDtypeStruct((rows, W_KV), BF16)] * 4
        + [jax.ShapeDtypeStruct((AUX_ROWS, W_KV), F32)] * 4
        + [jax.ShapeDtypeStruct((AUX_ROWS, W_KV), BF16)] * 4,
        grid=(rows // tm, 4),
        in_specs=[pl.BlockSpec((tm, D_MODEL), lambda i, j: (i, 0)),
                  pl.BlockSpec((AUX_ROWS, D_MODEL), lambda i, j: (0, 0)),
                  _kv_wt_spec(),
                  pl.BlockSpec((1, HEAD_DIM), lambda i, j: (0, 0))],
        out_specs=[any_spec] * 4 + [pl.BlockSpec((tm, W_KV), lambda i, j: (i, 0))] * 4 + [aux_spec] * 8,
        scratch_shapes=[pltpu.VMEM((2, tm * N_KV_HEADS, HEAD_DIM), F32),
                        pltpu.VMEM((4, N_META * N_KV_HEADS, HEAD_DIM), F32),
                        pltpu.SemaphoreType.DMA((2,)),
                        pltpu.SemaphoreType.DMA((4,))],
        compiler_params=_params("arbitrary", "arbitrary"),
        name="kv_state_proj",
    )(h, h_aux, wt, g_k)


def _lane_cumsum(x):
    n = x.shape[-1]
    lane = lax.broadcasted_iota(jnp.int32, x.shape, x.ndim - 1)
    k = 1
    while k < n:
        x = x + jnp.where(lane >= k, pltpu.roll(x, k, axis=x.ndim - 1), 0.0)
        k *= 2
    return x


def _cum_kernel(meta_ref, main_ref, cmeta_ref, cmain_ref):
    cm = _lane_cumsum(meta_ref[...])
    cmeta_ref[...] = cm * LOG2E
    cmain_ref[...] = (_lane_cumsum(main_ref[...]) + cm[:, LANES - 1:LANES]) * LOG2E


def _cum_forget(lf_meta_t, lf_main_t):
    b, _, s = lf_main_t.shape
    return pl.pallas_call(
        _cum_kernel,
        out_shape=[jax.ShapeDtypeStruct((b, N_Q_HEADS, LANES), F32),
                   jax.ShapeDtypeStruct((b, N_Q_HEADS, s), F32)],
        grid=(b,),
        in_specs=[pl.BlockSpec((N_Q_HEADS, LANES), lambda i: (0, 0)),
                  pl.BlockSpec((None, N_Q_HEADS, s), lambda i: (i, 0, 0))],
        out_specs=[pl.BlockSpec((None, N_Q_HEADS, LANES), lambda i: (i, 0, 0)),
                   pl.BlockSpec((None, N_Q_HEADS, s), lambda i: (i, 0, 0))],
        compiler_params=_params("parallel"),
        name="cum_forget",
    )(lf_meta_t, lf_main_t)


def _stack_heads(q):
    return jnp.concatenate([q[:, g * HEAD_DIM:(g + 1) * HEAD_DIM] for g in range(GROUP)], axis=0)


SB_CHUNK = 512
FOX_CHUNK = 512


def _chunk_visible(rel, c, cr, t, tk):
    col = lax.broadcasted_iota(jnp.int32, (cr, tk), 1)
    row = (lax.broadcasted_iota(jnp.int32, (cr, tk), 0) + c * cr) % t
    return rel(col, row)


def _sb_block(qs, kb, vb, uo, rel, r_ref, acc_ref):
    r, tk = qs.shape[0], kb.shape[0]
    t = r // GROUP
    cr = min(r, SB_CHUNK)
    score = lambda c: lax.dot_general(qs[c * cr:(c + 1) * cr], kb, _NT, preferred_element_type=F32)
    z_next = score(0)
    for c in range(r // cr):
        rows = slice(c * cr, (c + 1) * cr)
        z = z_next
        if c + 1 < r // cr:
            z_next = score(c + 1)
        sp = _softplus2(z)
        if rel is not None:
            vis = _chunk_visible(rel, c, cr, t, tk)
            sp = jnp.where(vis, sp, 0.0)
        ct = _split_dot(_split2(sp), uo)
        r_prev = r_ref[rows]
        a = jnp.exp2(z - ct[:, :tk] - r_prev)
        if rel is not None:
            a = jnp.where(vis, a, 0.0)
        acc_ref[rows] += _dot(a.astype(BF16), vb)
        r_ref[rows] = r_prev + ct[:, tk:]


def _sb_window(qs, kb, vb, uo, r_ref, acc_ref):
    r = qs.shape[0]
    t = r // GROUP
    nb = kb.shape[0] // t
    z = _dot_t(qs, kb)
    sp = _softplus2(z)
    vis = _chunk_visible(lambda c, row: c < row, 0, r, t, t)
    newer = None
    weights = [None] * nb
    for b in reversed(range(nb)):
        cols = slice(b * t, (b + 1) * t)
        diag = b == nb - 1
        spb = jnp.where(vis, sp[:, cols], 0.0) if diag else sp[:, cols]
        ct = _split_dot(_split2(spb), uo)
        arg = z[:, cols] - ct[:, :t]
        a = jnp.exp2(arg if newer is None else arg - newer)
        weights[b] = (jnp.where(vis, a, 0.0) if diag else a).astype(BF16)
        newer = ct[:, t:] if newer is None else newer + ct[:, t:]
    acc_ref[...] = _dot(jnp.concatenate(weights, axis=1), vb)
    r_ref[...] = newer


def _fox_block(qs, kb, vb, ckb, rel, m_ref, acc_ref):
    r, tk = qs.shape[0], kb.shape[0]
    t = r // GROUP
    cr = min(r, FOX_CHUNK)
    hc = max(cr // t, 1)
    v1 = jnp.concatenate([vb, jnp.ones((tk, LANES), BF16)], axis=1)
    score = lambda c: lax.dot_general(qs[c * cr:(c + 1) * cr], kb, _NT, preferred_element_type=F32)
    z_next = score(0)
    for c in range(r // cr):
        rows = slice(c * cr, (c + 1) * cr)
        g = c * cr // t
        bias = ckb[g:g + hc]
        z = (z_next.reshape(hc, cr // hc, tk) - bias[:, None, :]).reshape(cr, tk)
        if c + 1 < r // cr:
            z_next = score(c + 1)
        if rel is not None:
            z = jnp.where(_chunk_visible(rel, c, cr, t, tk), z, NEG_BIG)
        m_prev = m_ref[rows]
        m_new = jnp.maximum(m_prev, jnp.max(z, axis=1, keepdims=True))
        alpha = jnp.exp2(m_prev - m_new)
        p = jnp.exp2(z - jnp.tile(m_new, (1, tk // LANES)))
        acc_ref[rows] = acc_ref[rows] * jnp.tile(alpha, (1, 2)) + _dot(p.astype(BF16), v1)
        m_ref[rows] = m_new


SB_T = 128
SB_WIN = 3
SB_SUB = 2


def _store_heads(o_ref, o, t, gate_ref, add_ref=None, row0=0):
    rows = slice(row0, row0 + t)
    for g in range(GROUP):
        cols = slice(g * HEAD_DIM, (g + 1) * HEAD_DIM)
        val = o[g * t:(g + 1) * t] * gate_ref[rows, cols].astype(F32)
        if add_ref is not None:
            val = val + add_ref[rows, cols].astype(F32)
        o_ref[rows, cols] = val.astype(o_ref.dtype)


def _sb_prompt_kernel(q_ref, k_ref, v_ref, mk_ref, mv_ref, uo_ref, gate_ref, o_ref, r_ref, acc_ref):
    i = pl.program_id(2)
    t = SB_T
    uo = uo_ref[...]
    qs = [_stack_heads(q_ref[sub * t:(sub + 1) * t]) for sub in range(SB_SUB)]
    tile = lambda sub: i * SB_SUB + sub

    def first_block_only(sub):
        r_ref[sub] = jnp.zeros(r_ref.shape[1:], F32)
        acc_ref[sub] = jnp.zeros(acc_ref.shape[1:], F32)
        s = pl.multiple_of(tile(sub) * t, t)
        _sb_block(qs[sub], k_ref[pl.ds(s, t), :], v_ref[pl.ds(s, t), :], uo, lambda c, r: c < r,
                  r_ref.at[sub], acc_ref.at[sub])

    assert SB_SUB >= SB_WIN - 1

    @pl.when(i > 0)
    def _():
        for sub in range(SB_SUB):
            s = pl.multiple_of((tile(sub) - (SB_WIN - 1)) * t, t)
            _sb_window(qs[sub], k_ref[pl.ds(s, SB_WIN * t), :], v_ref[pl.ds(s, SB_WIN * t), :], uo,
                       r_ref.at[sub], acc_ref.at[sub])

    @pl.when(i == 0)
    def _():
        for sub in range(SB_SUB):
            if sub >= SB_WIN - 1:
                _sb_window(qs[sub], k_ref[pl.ds((sub - (SB_WIN - 1)) * t, SB_WIN * t), :],
                           v_ref[pl.ds((sub - (SB_WIN - 1)) * t, SB_WIN * t), :], uo,
                           r_ref.at[sub], acc_ref.at[sub])
            else:
                first_block_only(sub)

    for sub in range(SB_SUB):
        def visit(kb, vb, rel, sub=sub):
            _sb_block(qs[sub], kb, vb, uo, rel, r_ref.at[sub], acc_ref.at[sub])
            return jnp.min(r_ref[sub])

        def cond(c):
            return jnp.logical_and(c[0] >= 0, c[1] < SB_DEAD)

        def body(c, visit=visit):
            s = pl.multiple_of(c[0] * t, t)
            return c[0] - 1, visit(k_ref[pl.ds(s, t), :], v_ref[pl.ds(s, t), :], None)

        covered = jnp.where(jnp.logical_or(i > 0, sub >= SB_WIN - 1), SB_WIN, 1)
        _, mn = lax.while_loop(cond, body, (tile(sub) - covered, jnp.min(r_ref[sub])))

        @pl.when(mn < SB_DEAD)
        def _(visit=visit):
            visit(mk_ref[...], mv_ref[...], lambda c, r: c < N_META)

        _store_heads(o_ref, acc_ref[sub], t, gate_ref, row0=sub * t)


def _sb_prompt(q, k, v, mk, mv, uo, gates, batch, seq):
    t = SB_T * SB_SUB
    nb = seq // t
    tile = pl.BlockSpec((t, GROUP * HEAD_DIM), lambda b, n, i: (b * nb + i, n))
    return pl.pallas_call(
        _sb_prompt_kernel,
        out_shape=jax.ShapeDtypeStruct((batch * seq, W_Q), BF16),
        grid=(batch, N_KV_HEADS, nb),
        in_specs=[tile,
                  pl.BlockSpec((seq, HEAD_DIM), lambda b, n, i: (b, n)),
                  pl.BlockSpec((seq, HEAD_DIM), lambda b, n, i: (b, n)),
                  pl.BlockSpec((LANES, HEAD_DIM), lambda b, n, i: (0, n)),
                  pl.BlockSpec((LANES, HEAD_DIM), lambda b, n, i: (0, n)),
                  pl.BlockSpec(uo.shape, lambda b, n, i: (0, 0)),
                  tile],
        out_specs=tile,
        scratch_shapes=[pltpu.VMEM((SB_SUB, GROUP * SB_T, SB_T), F32),
                        pltpu.VMEM((SB_SUB, GROUP * SB_T, HEAD_DIM), F32)],
        compiler_params=_params("parallel", "parallel", "arbitrary"),
        name="sb_prompt",
    )(q, k, v, mk, mv, uo, gates)


FOX_T = 512
FOX_TK = 512


def _fox_prompt_kernel(q_ref, k_ref, v_ref, ck_ref, mk_ref, mv_ref, mck_ref, gate_ref, sb_ref,
                       o_ref, m_ref, acc_ref):
    i = pl.program_id(2)
    t = FOX_T
    qs = _stack_heads(q_ref[...])
    m_ref[...] = jnp.full_like(m_ref, NEG_BIG)
    acc_ref[...] = jnp.zeros_like(acc_ref)

    def visit(s, width, rel):
        _fox_block(qs, k_ref[pl.ds(s, width), :], v_ref[pl.ds(s, width), :], ck_ref[:, pl.ds(s, width)],
                   rel, m_ref, acc_ref)

    visit(pl.multiple_of(i * t, t), t, lambda c, r: c <= r)

    per_visit = FOX_TK // t

    def body(j, carry):
        visit(pl.multiple_of(j * FOX_TK, FOX_TK), FOX_TK, None)
        return carry

    lax.fori_loop(0, i // per_visit, body, 0)

    for left in range(per_visit - 1):
        @pl.when(i % per_visit > left)
        def _(left=left):
            visit(pl.multiple_of((i - i % per_visit + left) * t, t), t, None)

    _fox_block(qs, mk_ref[...], mv_ref[...], mck_ref[...],
               lambda c, r: c < N_META, m_ref, acc_ref)

    acc = acc_ref[...]
    _store_heads(o_ref, acc[:, :HEAD_DIM] / acc[:, HEAD_DIM:], t, gate_ref, sb_ref)


def _fox_prompt(q, k, v, ck, mk, mv, mck, gates, sb_part, batch, seq):
    t = FOX_T
    nb = seq // t
    tile = pl.BlockSpec((t, GROUP * HEAD_DIM), lambda b, n, i: (b * nb + i, n))
    return pl.pallas_call(
        _fox_prompt_kernel,
        out_shape=jax.ShapeDtypeStruct((batch * seq, W_Q), BF16),
        grid=(batch, N_KV_HEADS, nb),
        in_specs=[tile,
                  pl.BlockSpec((seq, HEAD_DIM), lambda b, n, i: (b, n)),
                  pl.BlockSpec((seq, HEAD_DIM), lambda b, n, i: (b, n)),
                  pl.BlockSpec((None, None, GROUP, seq), lambda b, n, i: (b, n, 0, 0)),
                  pl.BlockSpec((LANES, HEAD_DIM), lambda b, n, i: (0, n)),
                  pl.BlockSpec((LANES, HEAD_DIM), lambda b, n, i: (0, n)),
                  pl.BlockSpec((None, None, GROUP, LANES), lambda b, n, i: (b, n, 0, 0)),
                  pl.BlockSpec((t, GROUP * HEAD_DIM), lambda b, n, i: (b * nb + i, N_KV_HEADS + n)),
                  tile],
        out_specs=tile,
        scratch_shapes=[pltpu.VMEM((GROUP * t, LANES), F32), pltpu.VMEM((GROUP * t, 2 * LANES), F32)],
        compiler_params=_params("parallel", "parallel", "arbitrary"),
        name="fox_prompt",
    )(q, k, v, ck, mk, mv, mck, gates, sb_part)


def _meta_attn_kernel(sq_ref, sk_ref, sv_ref, fq_ref, fk_ref, fv_ref, ck_ref, uo_ref, gsb_ref, gfx_ref,
                      o_ref, m_ref, acc_ref, r_ref, sacc_ref):
    t = N_META
    valid = lambda c: c < N_META
    r_ref[...] = jnp.zeros_like(r_ref)
    sacc_ref[...] = jnp.zeros_like(sacc_ref)
    _sb_block(_stack_heads(sq_ref[...]), sk_ref[...], sv_ref[...], uo_ref[...],
              lambda c, r: jnp.logical_and(c < r, valid(c)), r_ref, sacc_ref)
    o = sacc_ref[...]
    m_ref[...] = jnp.full_like(m_ref, NEG_BIG)
    acc_ref[...] = jnp.zeros_like(acc_ref)
    _fox_block(_stack_heads(fq_ref[...]), fk_ref[...], fv_ref[...], ck_ref[...],
               lambda c, r: jnp.logical_and(c <= r, valid(c)), m_ref, acc_ref)
    acc = acc_ref[...]
    f = acc[:, :HEAD_DIM] / acc[:, HEAD_DIM:]
    for g in range(GROUP):
        cols = slice(g * HEAD_DIM, (g + 1) * HEAD_DIM)
        merged = (gsb_ref[:, cols].astype(F32) * o[g * t:(g + 1) * t]
                  + gfx_ref[:, cols].astype(F32) * f[g * t:(g + 1) * t])
        o_ref[:, cols] = merged.astype(o_ref.dtype)


def _meta_attn(sq, sk, sv, fq, fk, fv, ck, uo, gates):
    qspec = pl.BlockSpec((N_META, GROUP * HEAD_DIM), lambda n: (0, n))
    kspec = pl.BlockSpec((LANES, HEAD_DIM), lambda n: (0, n))
    return pl.pallas_call(
        _meta_attn_kernel,
        out_shape=jax.ShapeDtypeStruct((N_META, W_Q), BF16),
        grid=(N_KV_HEADS,),
        in_specs=[qspec, kspec, kspec, qspec, kspec, kspec,
                  pl.BlockSpec((None, GROUP, LANES), lambda n: (n, 0, 0)),
                  pl.BlockSpec(uo.shape, lambda n: (0, 0)),
                  qspec,
                  pl.BlockSpec((N_META, GROUP * HEAD_DIM), lambda n: (0, N_KV_HEADS + n))],
        out_specs=qspec,
        scratch_shapes=[pltpu.VMEM((GROUP * N_META, LANES), F32),
                        pltpu.VMEM((GROUP * N_META, 2 * LANES), F32),
                        pltpu.VMEM((GROUP * N_META, LANES), F32),
                        pltpu.VMEM((GROUP * N_META, HEAD_DIM), F32)],
        compiler_params=_params("parallel"),
        name="meta_attn",
    )(sq, sk, sv, fq, fk, fv, ck, uo, gates, gates)


PAGE_COLS = PAGE_SIZE * N_KV_HEADS
DEC_G = 16


def _own_head(shape):
    row = lax.broadcasted_iota(jnp.int32, shape, 0)
    col = lax.broadcasted_iota(jnp.int32, shape, 1)
    return (col % N_KV_HEADS) == (row // GROUP)


def _sb_decode_kernel(pt_ref, q_ref, gate_ref, ck_hbm, cv_hbm, uo_ref, o_ref, kbuf, vbuf, sem, r_ref, acc_ref):
    b = pl.program_id(0)
    n_pages = pt_ref.shape[1]
    q = q_ref[...]
    uo = uo_ref[...]
    own = _own_head((N_Q_HEADS, PAGE_COLS))

    def copies(p, slot):
        pid = pt_ref[b, p]
        return [pltpu.make_async_copy(ck_hbm.at[pid], kbuf.at[slot], sem.at[0, slot]),
                pltpu.make_async_copy(cv_hbm.at[pid], vbuf.at[slot], sem.at[1, slot])]

    r_ref[...] = jnp.zeros_like(r_ref)
    acc_ref[...] = jnp.zeros_like(acc_ref)
    for c in copies(n_pages - 1, (n_pages - 1) % 2):
        c.start()

    def cond(c):
        return jnp.logical_and(c[0] >= 0, c[1] < SB_DEAD)

    def body(c):
        p = c[0]
        slot = p % 2
        for cp in copies(p, slot):
            cp.wait()

        @pl.when(p > 0)
        def _():
            for cp in copies(p - 1, 1 - slot):
                cp.start()

        z = lax.dot_general(q, kbuf[slot].astype(BF16), _NT, preferred_element_type=F32)
        sp = jnp.where(own, _softplus2(z), 0.0)
        ct = _split_dot(_split2(sp), uo)
        r = r_ref[...]
        a = jnp.where(own, jnp.exp2(z - ct[:, :PAGE_COLS] - jnp.tile(r, (1, N_KV_HEADS))), 0.0)
        acc_ref[...] += _dot(a.astype(BF16), vbuf[slot].astype(BF16))
        r_new = r + ct[:, PAGE_COLS:]
        r_ref[...] = r_new
        return p - 1, jnp.min(r_new)

    p_end, _ = lax.while_loop(cond, body, (jnp.int32(n_pages - 1), jnp.float32(0.0)))

    @pl.when(p_end >= 0)
    def _():
        for cp in copies(p_end, p_end % 2):
            cp.wait()

    o_ref[...] = (acc_ref[...] * gate_ref[...].astype(F32)).astype(o_ref.dtype)


def _sb_decode(page_table, q, gate, cache_k, cache_v, uo):
    nb = q.shape[0]
    any_spec = pl.BlockSpec(memory_space=pl.ANY)
    head_spec = pl.BlockSpec((None, N_Q_HEADS, HEAD_DIM), lambda b, pt: (b, 0, 0))
    return pl.pallas_call(
        _sb_decode_kernel,
        out_shape=jax.ShapeDtypeStruct((nb, N_Q_HEADS, HEAD_DIM), BF16),
        grid_spec=pltpu.PrefetchScalarGridSpec(
            num_scalar_prefetch=1,
            grid=(nb,),
            in_specs=[head_spec, head_spec, any_spec, any_spec,
                      pl.BlockSpec(uo.shape, lambda b, pt: (0, 0))],
            out_specs=head_spec,
            scratch_shapes=[pltpu.VMEM((2, PAGE_COLS, HEAD_DIM), F32),
                            pltpu.VMEM((2, PAGE_COLS, HEAD_DIM), F32),
                            pltpu.SemaphoreType.DMA((2, 2)),
                            pltpu.VMEM((N_Q_HEADS, LANES), F32),
                            pltpu.VMEM((N_Q_HEADS, HEAD_DIM), F32)]),
        compiler_params=_params("arbitrary"),
        name="sb_decode",
    )(page_table, q, gate, cache_k, cache_v, uo)


def _fox_decode_kernel(pt_ref, q_ref, kn_ref, vn_ref, lfn_ref, gate_ref, sb_ref, ck_hbm, cv_hbm, lf_hbm,
                       us_ref, o_ref, kbuf, vbuf, lbuf, sem, m_ref, acc_ref, d_ref):
    n_seq, n_pages = pt_ref.shape
    n_grp = n_pages // DEC_G
    total = n_seq * n_grp
    us = us_ref[...]
    bias = jnp.where(_own_head((N_Q_HEADS, PAGE_COLS)), 0.0, NEG_BIG)

    def copies(t, slot):
        b = t // n_grp
        newest = n_pages - 1 - (t % n_grp) * DEC_G
        out = []
        for g in range(DEC_G):
            pid = pt_ref[b, newest - g]
            out += [pltpu.make_async_copy(ck_hbm.at[pid], kbuf.at[slot, g], sem.at[0, slot]),
                    pltpu.make_async_copy(cv_hbm.at[pid], vbuf.at[slot, g], sem.at[1, slot]),
                    pltpu.make_async_copy(lf_hbm.at[pid], lbuf.at[slot, g], sem.at[2, slot])]
        return out

    for cp in copies(0, 0):
        cp.start()

    def body(t, carry):
        slot = t % 2
        b = t // n_grp
        gi = t % n_grp

        @pl.when(t + 1 < total)
        def _():
            for cp in copies(t + 1, 1 - slot):
                cp.start()

        q = q_ref[b]

        @pl.when(gi == 0)
        def _():
            z_self = jnp.sum(q.astype(F32) * kn_ref[b].astype(F32), axis=1, keepdims=True)
            m_ref[...] = jnp.broadcast_to(z_self, m_ref.shape)
            acc_ref[...] = jnp.concatenate([vn_ref[b].astype(F32), jnp.ones((N_Q_HEADS, LANES), F32)], axis=1)
            d_ref[...] = jnp.broadcast_to(lfn_ref[b] * LOG2E, d_ref.shape)

        for cp in copies(t, slot):
            cp.wait()

        dt = _split_dot(_split3(lbuf[slot].reshape(DEC_G * N_Q_HEADS, PAGE_SIZE)), us) * LOG2E
        d = d_ref[...]
        zs = []
        for g in range(DEC_G):
            dg = dt[g * N_Q_HEADS:(g + 1) * N_Q_HEADS]
            z = lax.dot_general(q, kbuf[slot, g].astype(BF16), _NT, preferred_element_type=F32)
            zs.append(z + dg[:, :PAGE_COLS] + jnp.tile(d, (1, N_KV_HEADS)) + bias)
            d = d + dg[:, PAGE_COLS:]
        d_ref[...] = d
        z = jnp.concatenate(zs, axis=1)
        m_prev = m_ref[...]
        m_new = jnp.maximum(m_prev, jnp.max(z, axis=1, keepdims=True))
        alpha = jnp.exp2(m_prev - m_new)
        p = jnp.exp2(z - jnp.tile(m_new, (1, z.shape[1] // LANES)))
        lsum = jnp.sum(p, axis=1, keepdims=True)
        pb = p.astype(BF16)
        pv = _dot(pb[:, :PAGE_COLS], vbuf[slot, 0].astype(BF16))
        for g in range(1, DEC_G):
            pv += _dot(pb[:, g * PAGE_COLS:(g + 1) * PAGE_COLS], vbuf[slot, g].astype(BF16))
        acc = acc_ref[...] * jnp.tile(alpha, (1, 2)) + jnp.concatenate(
            [pv, jnp.broadcast_to(lsum, pv.shape)], axis=1)
        acc_ref[...] = acc
        m_ref[...] = m_new

        @pl.when(gi == n_grp - 1)
        def _():
            merged = acc[:, :HEAD_DIM] / acc[:, HEAD_DIM:] * gate_ref[b].astype(F32) + sb_ref[b].astype(F32)
            o_ref[b] = merged.astype(o_ref.dtype)

        return carry

    lax.fori_loop(0, total, body, 0)


def _fox_decode(page_table, q, k_new, v_new, lf_new, gate, sb_part, cache_k, cache_v, cache_lf_t, us):
    nb = q.shape[0]
    assert page_table.shape[1] % DEC_G == 0
    any_spec = pl.BlockSpec(memory_space=pl.ANY)
    head_spec = pl.BlockSpec((nb, N_Q_HEADS, HEAD_DIM), lambda i, pt: (0, 0, 0))
    return pl.pallas_call(
        _fox_decode_kernel,
        out_shape=jax.ShapeDtypeStruct((nb, N_Q_HEADS, HEAD_DIM), BF16),
        grid_spec=pltpu.PrefetchScalarGridSpec(
            num_scalar_prefetch=1,
            grid=(1,),
            in_specs=[head_spec, head_spec, head_spec,
                      pl.BlockSpec((nb, N_Q_HEADS, 1), lambda i, pt: (0, 0, 0)),
                      head_spec, head_spec,
                      any_spec, any_spec, any_spec,
                      pl.BlockSpec(us.shape, lambda i, pt: (0, 0))],
            out_specs=head_spec,
            scratch_shapes=[pltpu.VMEM((2, DEC_G, PAGE_COLS, HEAD_DIM), F32),
                            pltpu.VMEM((2, DEC_G, PAGE_COLS, HEAD_DIM), F32),
                            pltpu.VMEM((2, DEC_G, N_Q_HEADS, PAGE_SIZE), F32),
                            pltpu.SemaphoreType.DMA((3, 2)),
                            pltpu.VMEM((N_Q_HEADS, LANES), F32),
                            pltpu.VMEM((N_Q_HEADS, 2 * LANES), F32),
                            pltpu.VMEM((N_Q_HEADS, LANES), F32)]),
        compiler_params=_params("arbitrary"),
        name="fox_decode",
    )(page_table, q, k_new, v_new, lf_new, gate, sb_part, cache_k, cache_v, cache_lf_t, us)


def _out_proj_kernel(m_ref, ma_ref, w_ref, x_ref, xa_ref, y_ref, ya_ref):
    w = w_ref[...]
    y_ref[...] = x_ref[...] + _dot(m_ref[...], w)
    ya_ref[...] = xa_ref[...] + _dot(ma_ref[...], w)


def _out_proj(merged, merged_aux, w, x, x_aux, tm):
    rows = x.shape[0]
    y, y_aux = pl.pallas_call(
        _out_proj_kernel,
        out_shape=[jax.ShapeDtypeStruct((rows, D_MODEL), F32),
                   jax.ShapeDtypeStruct((rows // tm, AUX_ROWS, D_MODEL), F32)],
        grid=(rows // tm, D_MODEL // TN),
        in_specs=[pl.BlockSpec((tm, W_Q), lambda i, j: (i, 0)),
                  pl.BlockSpec((AUX_ROWS, W_Q), lambda i, j: (0, 0)),
                  pl.BlockSpec((W_Q, TN), lambda i, j: (0, j)),
                  pl.BlockSpec((tm, TN), lambda i, j: (i, j)),
                  pl.BlockSpec((AUX_ROWS, TN), lambda i, j: (0, j))],
        out_specs=[pl.BlockSpec((tm, TN), lambda i, j: (i, j)),
                   pl.BlockSpec((None, AUX_ROWS, TN), lambda i, j: (i, 0, j))],
        compiler_params=_params("parallel", "arbitrary"),
        name="out_proj",
    )(merged, merged_aux, w, x, x_aux)
    return y, y_aux[0]


def _ffn_kernel(y_ref, ya_ref, g_ref, wu_ref, wd_ref, o_ref, oa_ref, h_ref, ha_ref):
    i, k = pl.program_id(0), pl.program_id(1)

    def start(src, h_dst, o_dst):
        y = src[...]
        h_dst[...] = _rms(y, g_ref[...]).astype(BF16)
        o_dst[...] = y

    def step(h_src, o_dst):
        u = jnp.square(jnp.maximum(_dot(h_src[...], wu_ref[...]), 0.0))
        o_dst[...] += _dot(u.astype(BF16), wd_ref[...])

    @pl.when(k == 0)
    def _():
        start(y_ref, h_ref, o_ref)

    @pl.when(jnp.logical_and(i == 0, k == 0))
    def _():
        start(ya_ref, ha_ref, oa_ref)

    step(h_ref, o_ref)

    @pl.when(i == 0)
    def _():
        step(ha_ref, oa_ref)


def _ffn(y, y_aux, g, w_up, w_down, tm):
    rows = y.shape[0]
    aux_spec = pl.BlockSpec((AUX_ROWS, D_MODEL), lambda i, k: (0, 0))
    return pl.pallas_call(
        _ffn_kernel,
        out_shape=[jax.ShapeDtypeStruct((rows, D_MODEL), F32), jax.ShapeDtypeStruct((AUX_ROWS, D_MODEL), F32)],
        grid=(rows // tm, D_FF // TN),
        in_specs=[pl.BlockSpec((tm, D_MODEL), lambda i, k: (i, 0), pipeline_mode=pl.Buffered(1)),
                  aux_spec,
                  pl.BlockSpec((1, D_MODEL), lambda i, k: (0, 0)),
                  pl.BlockSpec((D_MODEL, TN), lambda i, k: (0, k)),
                  pl.BlockSpec((TN, D_MODEL), lambda i, k: (k, 0))],
        out_specs=[pl.BlockSpec((tm, D_MODEL), lambda i, k: (i, 0)), aux_spec],
        scratch_shapes=[pltpu.VMEM((tm, D_MODEL), BF16), pltpu.VMEM((AUX_ROWS, D_MODEL), BF16)],
        compiler_params=pltpu.CompilerParams(dimension_semantics=("arbitrary", "arbitrary"),
                                             vmem_limit_bytes=FFN_VMEM_LIMIT),
        name="ffn",
    )(y, y_aux, g, w_up, w_down)


def _project_rows(x, x_aux, wts, batch, seq):
    norm = lambda rows, tm: _rms_forget(rows, wts["g_mix"], wts["w_fl_hi"], wts["w_fl_lo"], wts["b_f"], tm)
    h, lf = norm(x, TM_MAIN // 2)
    h_aux, lf_aux = norm(x_aux, AUX_ROWS)
    wt = wts["w_in_t"]
    sq = _proj(h, h_aux, wt, 0, W_Q, wts["g_q"], "scale", TM_PROJ, "proj_sq")
    fq = _proj(h, h_aux, wt, W_Q + 2 * W_KV, W_Q, wts["g_q"], "norm_scale", TM_PROJ, "proj_fq")
    gates = _proj(h, h_aux, wt, FORGET_OFFSET + N_Q_HEADS, 2 * D_MODEL, wts["g_q"], "sigmoid", TM_PROJ,
                  "proj_gates")
    kv = _kv_state_proj(h, h_aux, wt, wts["g_k"], TM_MAIN, batch, seq)
    return sq, fq, gates, kv, (lf, lf_aux)


def _finish_rows(x, x_aux, merged, merged_aux, wts):
    y1, y1_aux = _out_proj(merged, merged_aux, wts["w_out"], x, x_aux, TM_PROJ)
    return _ffn(y1, y1_aux, wts["g_ffn"], wts["w_up"], wts["w_down"], TM_MAIN)


def kernel(x_prompt, x_sample, cache_sb_k, cache_sb_v, cache_fox_k, cache_fox_v, cache_fox_logf,
           page_table, meta_tokens, g_mix, w_in, b_forget, g_q, g_k, w_out, g_ffn, w_up, w_down):
    batch, seq, _ = x_prompt.shape
    dec = x_sample.shape[0]
    assert w_in.shape[0] == 1 and x_sample.shape[1] == 1 and N_META + dec <= AUX_ROWS
    pool = cache_sb_k.shape[1]

    w_in_t = jnp.swapaxes(w_in[0], 0, 1)
    w_fl = jnp.pad(w_in_t[FORGET_OFFSET:FORGET_OFFSET + N_Q_HEADS], ((0, LANES - N_Q_HEADS), (0, 0)))
    w_fl_hi = w_fl.astype(BF16)
    wts = {
        "g_mix": g_mix, "g_q": g_q, "g_k": g_k, "g_ffn": g_ffn, "b_f": b_forget,
        "w_in_t": w_in_t,
        "w_fl_hi": w_fl_hi, "w_fl_lo": (w_fl - w_fl_hi.astype(F32)).astype(BF16),
        "w_out": w_out[0].astype(BF16), "w_up": w_up[0].astype(BF16), "w_down": w_down[0].astype(BF16),
    }

    x_main = x_prompt.reshape(batch * seq, D_MODEL)
    x_aux = jnp.concatenate([meta_tokens, x_sample.reshape(dec, D_MODEL),
                             jnp.zeros((AUX_ROWS - N_META - dec, D_MODEL), F32)], axis=0)

    (sq_m, sq_a), (fq_m, fq_a), (gates_m, gates_a), kv, (lf_m, lf_a) = _project_rows(
        x_main, x_aux, wts, batch, seq)
    sk_st, sv_st, fk_st, fv_st, skb_m, svb_m, fkb_m, fvb_m = kv[:8]
    sk_a, sv_a, fk_a, fv_a, skb_a, svb_a, fkb_a, fvb_a = kv[8:]

    lf_meta_t = jnp.pad(lf_a[:N_META].T, ((0, 0), (0, LANES - N_META)))
    lf_main_t = jnp.swapaxes(lf_m.reshape(batch, seq, N_Q_HEADS), 1, 2)
    ck_meta, ck_main = _cum_forget(lf_meta_t, lf_main_t)
    ck_meta = ck_meta.reshape(batch, N_KV_HEADS, GROUP, LANES)
    ck_main = ck_main.reshape(batch, N_KV_HEADS, GROUP, seq)

    pad_keys = lambda a: jnp.pad(a[:N_META], ((0, LANES - N_META), (0, 0)))
    mk_sb, mv_sb, mk_fx, mv_fx = pad_keys(skb_a), pad_keys(svb_a), pad_keys(fkb_a), pad_keys(fvb_a)

    j_idx = lax.broadcasted_iota(jnp.int32, (LANES, LANES), 0)
    s_idx = lax.broadcasted_iota(jnp.int32, (LANES, LANES), 1)
    ones = jnp.ones((LANES, LANES), BF16)
    uo = jnp.concatenate([(j_idx >= s_idx).astype(BF16), ones], axis=1)
    uo = jnp.concatenate([uo] * 2, axis=0)
    c_src = lax.broadcasted_iota(jnp.int32, (PAGE_COLS, PAGE_COLS), 0) // N_KV_HEADS
    c_dst = lax.broadcasted_iota(jnp.int32, (PAGE_COLS, PAGE_COLS), 1) // N_KV_HEADS
    uo_page = jnp.concatenate([(c_src >= c_dst).astype(BF16), jnp.ones((PAGE_COLS, LANES), BF16)], axis=1)
    uo_page = jnp.concatenate([uo_page] * 2, axis=0)
    j_key = lax.broadcasted_iota(jnp.int32, (PAGE_SIZE, PAGE_COLS), 0)
    c_key = lax.broadcasted_iota(jnp.int32, (PAGE_SIZE, PAGE_COLS), 1) // N_KV_HEADS
    us_page = jnp.concatenate([(j_key > c_key).astype(BF16), ones], axis=1)
    us_page = jnp.concatenate([us_page] * 3, axis=0)

    sb_m = _sb_prompt(sq_m, skb_m, svb_m, mk_sb, mv_sb, uo, gates_m, batch, seq)
    merged_m = _fox_prompt(fq_m, fkb_m, fvb_m, ck_main, mk_fx, mv_fx, ck_meta, gates_m, sb_m, batch, seq)
    merged_meta = _meta_attn(sq_a[:N_META], mk_sb, mv_sb, fq_a[:N_META], mk_fx, mv_fx,
                             ck_meta[0], uo, gates_a[:N_META])

    heads = lambda a: a[N_META:N_META + dec].reshape(dec, N_Q_HEADS, HEAD_DIM)
    kv_heads = lambda a: jnp.repeat(a[N_META:N_META + dec].reshape(dec, N_KV_HEADS, HEAD_DIM), GROUP, axis=1)
    pages = lambda c: c[0].reshape(pool, PAGE_COLS, HEAD_DIM)
    sb_dec = _sb_decode(page_table, heads(sq_a), heads(gates_a[:, :W_Q]),
                        pages(cache_sb_k), pages(cache_sb_v), uo_page)
    merged_dec = _fox_decode(page_table, heads(fq_a), kv_heads(fkb_a), kv_heads(fvb_a),
                             lf_a[N_META:N_META + dec].reshape(dec, N_Q_HEADS, 1),
                             heads(gates_a[:, W_Q:]), sb_dec,
                             pages(cache_fox_k), pages(cache_fox_v),
                             jnp.swapaxes(cache_fox_logf[0], 1, 2), us_page)

    tail = jnp.zeros((AUX_ROWS - N_META - dec, W_Q), BF16)
    merged_a = jnp.concatenate([merged_meta, merged_dec.reshape(dec, W_Q), tail], axis=0)

    y_main, y_aux = _finish_rows(x_main, x_aux, merged_m, merged_a, wts)

    def prompt_state(a_aux, a_main, tail_shape):
        meta = jnp.broadcast_to(a_aux[None, :N_META], (batch, N_META, a_aux.shape[1]))
        full = jnp.concatenate([meta, a_main.reshape(batch, seq, a_aux.shape[1])], axis=1)
        return full.reshape((1, batch, seq + N_META) + tail_shape)

    kv_shape = (N_KV_HEADS, HEAD_DIM)
    kv_state = lambda st: st.reshape((1, batch, seq + N_META) + kv_shape)
    sample_state = lambda a, tail_shape: a[N_META:N_META + dec].reshape((1, dec, 1) + tail_shape)
    return (y_main.reshape(batch, seq, D_MODEL), y_aux[N_META:N_META + dec].reshape(dec, 1, D_MODEL),
            kv_state(sk_st), kv_state(sv_st), kv_state(fk_st), kv_state(fv_st),
            prompt_state(lf_a, lf_m, (N_Q_HEADS,)),
            sample_state(sk_a, kv_shape), sample_state(sv_a, kv_shape),
            sample_state(fk_a, kv_shape), sample_state(fv_a, kv_shape),
            sample_state(lf_a, (N_Q_HEADS,)))
```

```python
import functools

import jax
import jax.numpy as jnp
from jax import lax
from jax.experimental import pallas as pl
from jax.experimental.pallas import tpu as pltpu

D_MODEL = 2048
HEAD_DIM = 128
N_Q_HEADS = 16
N_KV_HEADS = 4
GROUP = 4
W_Q = N_Q_HEADS * HEAD_DIM
W_KV = N_KV_HEADS * HEAD_DIM
D_FF = 4 * D_MODEL
N_META = 16
PAGE_SIZE = 128
EPS = 1e-6
LOG2E = 1.4426950408889634
Q_SCALE = HEAD_DIM ** -0.5 * LOG2E
FORGET_OFFSET = 2 * W_Q + 4 * W_KV
AUX_ROWS = 32

BF16 = jnp.bfloat16
F32 = jnp.float32

LANES = 128
SUBLANES = 8
NEG_BIG = -1e30
SB_DEAD = 104.0 * LOG2E
VMEM_LIMIT = 48 * 1024 * 1024
FFN_VMEM_LIMIT = 58 * 1024 * 1024
TM_MAIN = 1024
TM_PROJ = 2048
TN = 512
TF = 1024
PROJ_CHUNK = 512

_NT = (((1,), (1,)), ((), ()))


def _params(*sem):
    return pltpu.CompilerParams(dimension_semantics=sem, vmem_limit_bytes=VMEM_LIMIT)


def _rms(x, g):
    ms = jnp.mean(x * x, axis=-1, keepdims=True)
    return x * lax.rsqrt(ms + EPS) * g


def _softplus(z):
    return jnp.maximum(z, 0.0) + jnp.log1p(jnp.exp(-jnp.abs(z)))


def _softplus2(z2):
    return jnp.maximum(z2, 0.0) + jnp.log2(1.0 + jnp.exp2(-jnp.abs(z2)))


def _split2(x):
    hi = x.astype(BF16)
    lo = (x - hi.astype(F32)).astype(BF16)
    return hi, lo


def _split3(x):
    hi = x.astype(BF16)
    r = x - hi.astype(F32)
    mid = r.astype(BF16)
    lo = (r - mid.astype(F32)).astype(BF16)
    return hi, mid, lo


def _dot(a, b):
    return jnp.dot(a, b, preferred_element_type=F32)


def _dot_t(a, bt):
    return lax.dot_general(a, bt, _NT, preferred_element_type=F32)


def _split_dot(parts, m_stacked):
    return _dot(jnp.concatenate(parts, axis=1), m_stacked)


def _rms_forget_kernel(x_ref, g_ref, whi_ref, wlo_ref, b_ref, h_ref, lf_ref):
    h = _rms(x_ref[...], g_ref[...])
    h_ref[...] = h.astype(h_ref.dtype)
    hh, hl = _split2(h)
    z = _dot_t(hh, whi_ref[...]) + _dot_t(hl, whi_ref[...]) + _dot_t(hh, wlo_ref[...])
    z = z[:, :N_Q_HEADS] + b_ref[...]
    lf_ref[...] = -_softplus(-z)


def _rms_forget(x, g, w_hi, w_lo, b, tm):
    rows = x.shape[0]
    const = lambda shape: pl.BlockSpec(shape, lambda i: (0, 0))
    return pl.pallas_call(
        _rms_forget_kernel,
        out_shape=[jax.ShapeDtypeStruct((rows, D_MODEL), BF16),
                   jax.ShapeDtypeStruct((rows, N_Q_HEADS), F32)],
        grid=(rows // tm,),
        in_specs=[pl.BlockSpec((tm, D_MODEL), lambda i: (i, 0)),
                  const((1, D_MODEL)), const((LANES, D_MODEL)), const((LANES, D_MODEL)),
                  const((1, N_Q_HEADS))],
        out_specs=[pl.BlockSpec((tm, D_MODEL), lambda i: (i, 0)),
                   pl.BlockSpec((tm, N_Q_HEADS), lambda i: (i, 0))],
        compiler_params=_params("parallel"),
        name="rms_forget",
    )(x, g, w_hi, w_lo, b)


def _head_norm(acc, g):
    outs = []
    for hh in range(acc.shape[1] // HEAD_DIM):
        xh = acc[:, hh * HEAD_DIM:(hh + 1) * HEAD_DIM]
        outs.append(_rms(xh, g))
    return jnp.concatenate(outs, axis=1)


def _proj_kernel(h_ref, ha_ref, wt_ref, g_ref, o_ref, oa_ref, *, mode):
    wt = wt_ref[...].astype(BF16)
    tm = h_ref.shape[0]
    chunk = min(tm, PROJ_CHUNK)
    pieces = [(h_ref, o_ref, slice(c * chunk, (c + 1) * chunk)) for c in range(tm // chunk)]
    pieces.append((ha_ref, oa_ref, slice(None)))
    acc_next = _dot_t(h_ref[pieces[0][2]], wt)
    for n, (_, dst, rows) in enumerate(pieces):
        acc = acc_next
        if n + 1 < len(pieces):
            src, _, nxt = pieces[n + 1]
            acc_next = _dot_t(src[nxt], wt)
        if mode == "scale":
            acc = acc * Q_SCALE
        elif mode == "norm_scale":
            acc = _head_norm(acc, g_ref[...]) * Q_SCALE
        elif mode == "sigmoid":
            acc = jax.nn.sigmoid(acc)
        dst[rows] = acc.astype(dst.dtype)


def _wt_rows_spec(row0):
    if row0 % TN == 0:
        return pl.BlockSpec((TN, D_MODEL), lambda i, j: (row0 // TN + j, 0))
    assert row0 % SUBLANES == 0
    return pl.BlockSpec((pl.Element(TN), pl.Element(D_MODEL)),
                        lambda i, j: (pl.multiple_of(row0 + j * TN, SUBLANES), 0))


def _proj(h, h_aux, wt, row0, n, g, mode, tm, name):
    rows = h.shape[0]
    out, out_aux = pl.pallas_call(
        functools.partial(_proj_kernel, mode=mode),
        out_shape=[jax.ShapeDtypeStruct((rows, n), BF16),
                   jax.ShapeDtypeStruct((rows // tm, AUX_ROWS, n), BF16)],
        grid=(rows // tm, n // TN),
        in_specs=[pl.BlockSpec((tm, D_MODEL), lambda i, j: (i, 0)),
                  pl.BlockSpec((AUX_ROWS, D_MODEL), lambda i, j: (0, 0)),
                  _wt_rows_spec(row0),
                  pl.BlockSpec((1, HEAD_DIM), lambda i, j: (0, 0))],
        out_specs=[pl.BlockSpec((tm, TN), lambda i, j: (i, j)),
                   pl.BlockSpec((None, AUX_ROWS, TN), lambda i, j: (i, 0, j))],
        compiler_params=_params("parallel", "arbitrary"),
        name=name,
    )(h, h_aux, wt, g)
    return out, out_aux[0]


def _kv_wt_spec():
    sb_kv, fox_kv = W_Q // W_KV, (2 * W_Q + 2 * W_KV) // W_KV
    return pl.BlockSpec((W_KV, D_MODEL), lambda i, j: (sb_kv + j + (fox_kv - sb_kv - 2) * (j // 2), 0))


def _kv_state_kernel(h_ref, ha_ref, wt_ref, gk_ref, *refs, tiles_per_batch):
    state = refs[:4]
    bf_ref, aux_f32_ref, aux_bf_ref, stage, meta_stage, sem, meta_sem = refs[4:]
    i, j = pl.program_id(0), pl.program_id(1)
    n_i = pl.num_programs(0)
    tm = h_ref.shape[0]
    b, it = i // tiles_per_batch, i % tiles_per_batch
    row0 = pl.multiple_of(N_META * N_KV_HEADS + it * tm * N_KV_HEADS, SUBLANES)
    slot = j % 2
    wt = wt_ref[...].astype(BF16)
    finish = lambda a: jnp.where(j == 2, _head_norm(a, gk_ref[...]), a)
    val = finish(_dot_t(h_ref[...], wt))
    val_aux = finish(_dot_t(ha_ref[...], wt))
    bf_ref[...] = val.astype(BF16)
    aux_f32_ref[...] = val_aux
    aux_bf_ref[...] = val_aux.astype(BF16)

    def tile_copy(jj):
        return pltpu.make_async_copy(stage.at[slot], state[jj].at[b, pl.ds(row0, tm * N_KV_HEADS), :],
                                     sem.at[slot])

    def meta_copy(jj):
        return pltpu.make_async_copy(meta_stage.at[jj], state[jj].at[b, pl.ds(0, N_META * N_KV_HEADS), :],
                                     meta_sem.at[jj])

    def per_tensor(fn):
        for jj in range(4):
            pl.when(j == jj)(functools.partial(fn, jj))

    def interleave(dst, val, rows):
        for n in range(N_KV_HEADS):
            dst[pl.ds(n, rows, stride=N_KV_HEADS), :] = val[:rows, n * HEAD_DIM:(n + 1) * HEAD_DIM]

    @pl.when(jnp.logical_or(i > 0, j >= 2))
    def _():
        per_tensor(lambda jj: tile_copy((jj + 2) % 4).wait())

    interleave(stage.at[slot], val, tm)
    per_tensor(lambda jj: tile_copy(jj).start())

    @pl.when(it == 0)
    def _():
        @pl.when(b > 0)
        def _():
            per_tensor(lambda jj: meta_copy(jj).wait())

        interleave(meta_stage.at[j], val_aux, N_META)
        per_tensor(lambda jj: meta_copy(jj).start())

    @pl.when(jnp.logical_and(i == n_i - 1, j == 3))
    def _():
        pltpu.make_async_copy(stage.at[0], state[2].at[b, pl.ds(row0, tm * N_KV_HEADS), :], sem.at[0]).wait()
        tile_copy(3).wait()
        for jj in range(4):
            meta_copy(jj).wait()


def _kv_state_proj(h, h_aux, wt, g_k, tm, batch, seq):
    rows = h.shape[0]
    tiles_per_batch = seq // tm
    state_rows = (N_META + seq) * N_KV_HEADS
    any_spec = pl.BlockSpec(memory_space=pl.ANY)
    aux_spec = pl.BlockSpec((None, AUX_ROWS, W_KV), lambda i, j: (i, 0, j))
    outs = pl.pallas_call(
        functools.partial(_kv_state_kernel, tiles_per_batch=tiles_per_batch),
        out_shape=[jax.ShapeDtypeStruct((batch, state_rows, HEAD_DIM), F32)] * 4
        + [jax.ShapeDtypeStruct((rows, 4 * W_KV), BF16),
           jax.ShapeDtypeStruct((rows // tm, AUX_ROWS, 4 * W_KV), F32),
           jax.ShapeDtypeStruct((rows // tm, AUX_ROWS, 4 * W_KV), BF16)],
        grid=(rows // tm, 4),
        in_specs=[pl.BlockSpec((tm, D_MODEL), lambda i, j: (i, 0)),
                  pl.BlockSpec((AUX_ROWS, D_MODEL), lambda i, j: (0, 0)),
                  _kv_wt_spec(),
                  pl.BlockSpec((1, HEAD_DIM), lambda i, j: (0, 0))],
        out_specs=[any_spec] * 4 + [pl.BlockSpec((tm, W_KV), lambda i, j: (i, j)), aux_spec, aux_spec],
        scratch_shapes=[pltpu.VMEM((2, tm * N_KV_HEADS, HEAD_DIM), F32),
                        pltpu.VMEM((4, N_META * N_KV_HEADS, HEAD_DIM), F32),
                        pltpu.SemaphoreType.DMA((2,)),
                        pltpu.SemaphoreType.DMA((4,))],
        compiler_params=_params("arbitrary", "arbitrary"),
        name="kv_state_proj",
    )(h, h_aux, wt, g_k)
    return outs[:4], outs[4], outs[5][0], outs[6][0]


def _lane_cumsum(x):
    n = x.shape[-1]
    lane = lax.broadcasted_iota(jnp.int32, x.shape, x.ndim - 1)
    k = 1
    while k < n:
        x = x + jnp.where(lane >= k, pltpu.roll(x, k, axis=x.ndim - 1), 0.0)
        k *= 2
    return x


def _cum_kernel(meta_ref, main_ref, cmeta_ref, cmain_ref):
    cm = _lane_cumsum(meta_ref[...])
    cmeta_ref[...] = cm * LOG2E
    cmain_ref[...] = (_lane_cumsum(main_ref[...]) + cm[:, LANES - 1:LANES]) * LOG2E


def _cum_forget(lf_meta_t, lf_main_t):
    b, _, s = lf_main_t.shape
    return pl.pallas_call(
        _cum_kernel,
        out_shape=[jax.ShapeDtypeStruct((b, N_Q_HEADS, LANES), F32),
                   jax.ShapeDtypeStruct((b, N_Q_HEADS, s), F32)],
        grid=(b,),
        in_specs=[pl.BlockSpec((N_Q_HEADS, LANES), lambda i: (0, 0)),
                  pl.BlockSpec((None, N_Q_HEADS, s), lambda i: (i, 0, 0))],
        out_specs=[pl.BlockSpec((None, N_Q_HEADS, LANES), lambda i: (i, 0, 0)),
                   pl.BlockSpec((None, N_Q_HEADS, s), lambda i: (i, 0, 0))],
        compiler_params=_params("parallel"),
        name="cum_forget",
    )(lf_meta_t, lf_main_t)


def _stack_heads(q):
    return jnp.concatenate([q[:, g * HEAD_DIM:(g + 1) * HEAD_DIM] for g in range(GROUP)], axis=0)


SB_CHUNK = 512
FOX_CHUNK = 512


def _chunk_visible(rel, c, cr, t, tk):
    col = lax.broadcasted_iota(jnp.int32, (cr, tk), 1)
    row = (lax.broadcasted_iota(jnp.int32, (cr, tk), 0) + c * cr) % t
    return rel(col, row)


def _sb_block(qs, kb, vb, uo, rel, r_ref, acc_ref):
    r, tk = qs.shape[0], kb.shape[0]
    t = r // GROUP
    cr = min(r, SB_CHUNK)
    score = lambda c: lax.dot_general(qs[c * cr:(c + 1) * cr], kb, _NT, preferred_element_type=F32)
    z_next = score(0)
    for c in range(r // cr):
        rows = slice(c * cr, (c + 1) * cr)
        z = z_next
        if c + 1 < r // cr:
            z_next = score(c + 1)
        sp = _softplus2(z)
        if rel is not None:
            vis = _chunk_visible(rel, c, cr, t, tk)
            sp = jnp.where(vis, sp, 0.0)
        ct = _split_dot(_split2(sp), uo)
        r_prev = r_ref[rows]
        a = jnp.exp2(z - ct[:, :tk] - r_prev)
        if rel is not None:
            a = jnp.where(vis, a, 0.0)
        acc_ref[rows] += _dot(a.astype(BF16), vb)
        r_ref[rows] = r_prev + ct[:, tk:]


def _sb_window(qs, kb, vb, uo, r_ref, acc_ref):
    r = qs.shape[0]
    t = r // GROUP
    nb = kb.shape[0] // t
    z = _dot_t(qs, kb)
    sp = _softplus2(z)
    vis = _chunk_visible(lambda c, row: c < row, 0, r, t, t)
    newer = None
    weights = [None] * nb
    for b in reversed(range(nb)):
        cols = slice(b * t, (b + 1) * t)
        diag = b == nb - 1
        spb = jnp.where(vis, sp[:, cols], 0.0) if diag else sp[:, cols]
        ct = _split_dot(_split2(spb), uo)
        arg = z[:, cols] - ct[:, :t]
        a = jnp.exp2(arg if newer is None else arg - newer)
        weights[b] = (jnp.where(vis, a, 0.0) if diag else a).astype(BF16)
        newer = ct[:, t:] if newer is None else newer + ct[:, t:]
    acc_ref[...] = _dot(jnp.concatenate(weights, axis=1), vb)
    r_ref[...] = newer


def _fox_block(qs, kb, vb, ckb, rel, m_ref, acc_ref):
    r, tk = qs.shape[0], kb.shape[0]
    t = r // GROUP
    cr = min(r, FOX_CHUNK)
    hc = max(cr // t, 1)
    v1 = jnp.concatenate([vb, jnp.ones((tk, LANES), BF16)], axis=1)
    score = lambda c: lax.dot_general(qs[c * cr:(c + 1) * cr], kb, _NT, preferred_element_type=F32)
    z_next = score(0)
    for c in range(r // cr):
        rows = slice(c * cr, (c + 1) * cr)
        g = c * cr // t
        bias = ckb[g:g + hc]
        z = (z_next.reshape(hc, cr // hc, tk) - bias[:, None, :]).reshape(cr, tk)
        if c + 1 < r // cr:
            z_next = score(c + 1)
        if rel is not None:
            z = jnp.where(_chunk_visible(rel, c, cr, t, tk), z, NEG_BIG)
        m_prev = m_ref[rows]
        m_new = jnp.maximum(m_prev, jnp.max(z, axis=1, keepdims=True))
        alpha = jnp.exp2(m_prev - m_new)
        p = jnp.exp2(z - jnp.tile(m_new, (1, tk // LANES)))
        acc_ref[rows] = acc_ref[rows] * jnp.tile(alpha, (1, 2)) + _dot(p.astype(BF16), v1)
        m_ref[rows] = m_new


SB_T = 128
SB_WIN = 3
SB_SUB = 2


def _store_heads(o_ref, o, t, gate_ref, add_ref=None, row0=0):
    rows = slice(row0, row0 + t)
    for g in range(GROUP):
        cols = slice(g * HEAD_DIM, (g + 1) * HEAD_DIM)
        val = o[g * t:(g + 1) * t] * gate_ref[rows, cols].astype(F32)
        if add_ref is not None:
            val = val + add_ref[rows, cols].astype(F32)
        o_ref[rows, cols] = val.astype(o_ref.dtype)


def _sb_prompt_kernel(q_ref, k_ref, v_ref, mk_ref, mv_ref, uo_ref, gate_ref, o_ref, r_ref, acc_ref):
    i = pl.program_id(2)
    t = SB_T
    uo = uo_ref[...]
    qs = [_stack_heads(q_ref[sub * t:(sub + 1) * t]) for sub in range(SB_SUB)]
    tile = lambda sub: i * SB_SUB + sub

    def first_block_only(sub):
        r_ref[sub] = jnp.zeros(r_ref.shape[1:], F32)
        acc_ref[sub] = jnp.zeros(acc_ref.shape[1:], F32)
        s = pl.multiple_of(tile(sub) * t, t)
        _sb_block(qs[sub], k_ref[pl.ds(s, t), :], v_ref[pl.ds(s, t), :], uo, lambda c, r: c < r,
                  r_ref.at[sub], acc_ref.at[sub])

    assert SB_SUB >= SB_WIN - 1

    @pl.when(i > 0)
    def _():
        for sub in range(SB_SUB):
            s = pl.multiple_of((tile(sub) - (SB_WIN - 1)) * t, t)
            _sb_window(qs[sub], k_ref[pl.ds(s, SB_WIN * t), :], v_ref[pl.ds(s, SB_WIN * t), :], uo,
                       r_ref.at[sub], acc_ref.at[sub])

    @pl.when(i == 0)
    def _():
        for sub in range(SB_SUB):
            if sub >= SB_WIN - 1:
                _sb_window(qs[sub], k_ref[pl.ds((sub - (SB_WIN - 1)) * t, SB_WIN * t), :],
                           v_ref[pl.ds((sub - (SB_WIN - 1)) * t, SB_WIN * t), :], uo,
                           r_ref.at[sub], acc_ref.at[sub])
            else:
                first_block_only(sub)

    for sub in range(SB_SUB):
        def visit(kb, vb, rel, sub=sub):
            _sb_block(qs[sub], kb, vb, uo, rel, r_ref.at[sub], acc_ref.at[sub])
            return jnp.min(r_ref[sub])

        def cond(c):
            return jnp.logical_and(c[0] >= 0, c[1] < SB_DEAD)

        def body(c, visit=visit):
            s = pl.multiple_of(c[0] * t, t)
            return c[0] - 1, visit(k_ref[pl.ds(s, t), :], v_ref[pl.ds(s, t), :], None)

        covered = jnp.where(jnp.logical_or(i > 0, sub >= SB_WIN - 1), SB_WIN, 1)
        _, mn = lax.while_loop(cond, body, (tile(sub) - covered, jnp.min(r_ref[sub])))

        @pl.when(mn < SB_DEAD)
        def _(visit=visit):
            visit(mk_ref[...], mv_ref[...], lambda c, r: c < N_META)

        _store_heads(o_ref, acc_ref[sub], t, gate_ref, row0=sub * t)


def _kv_head_spec(seq, tensor):
    return pl.BlockSpec((seq, HEAD_DIM), lambda b, n, i: (b, tensor * N_KV_HEADS + n))


def _sb_prompt(q, kv, mk, mv, uo, gates, batch, seq):
    t = SB_T * SB_SUB
    nb = seq // t
    tile = pl.BlockSpec((t, GROUP * HEAD_DIM), lambda b, n, i: (b * nb + i, n))
    return pl.pallas_call(
        _sb_prompt_kernel,
        out_shape=jax.ShapeDtypeStruct((batch * seq, W_Q), BF16),
        grid=(batch, N_KV_HEADS, nb),
        in_specs=[tile,
                  _kv_head_spec(seq, 0),
                  _kv_head_spec(seq, 1),
                  pl.BlockSpec((LANES, HEAD_DIM), lambda b, n, i: (0, n)),
                  pl.BlockSpec((LANES, HEAD_DIM), lambda b, n, i: (0, n)),
                  pl.BlockSpec(uo.shape, lambda b, n, i: (0, 0)),
                  tile],
        out_specs=tile,
        scratch_shapes=[pltpu.VMEM((SB_SUB, GROUP * SB_T, SB_T), F32),
                        pltpu.VMEM((SB_SUB, GROUP * SB_T, HEAD_DIM), F32)],
        compiler_params=_params("parallel", "parallel", "arbitrary"),
        name="sb_prompt",
    )(q, kv, kv, mk, mv, uo, gates)


FOX_T = 512
FOX_TK = 512


def _fox_prompt_kernel(q_ref, k_ref, v_ref, ck_ref, mk_ref, mv_ref, mck_ref, gate_ref, sb_ref,
                       o_ref, m_ref, acc_ref):
    i = pl.program_id(2)
    t = FOX_T
    qs = _stack_heads(q_ref[...])
    m_ref[...] = jnp.full_like(m_ref, NEG_BIG)
    acc_ref[...] = jnp.zeros_like(acc_ref)

    def visit(s, width, rel):
        _fox_block(qs, k_ref[pl.ds(s, width), :], v_ref[pl.ds(s, width), :], ck_ref[:, pl.ds(s, width)],
                   rel, m_ref, acc_ref)

    visit(pl.multiple_of(i * t, t), t, lambda c, r: c <= r)

    per_visit = FOX_TK // t

    def body(j, carry):
        visit(pl.multiple_of(j * FOX_TK, FOX_TK), FOX_TK, None)
        return carry

    lax.fori_loop(0, i // per_visit, body, 0)

    for left in range(per_visit - 1):
        @pl.when(i % per_visit > left)
        def _(left=left):
            visit(pl.multiple_of((i - i % per_visit + left) * t, t), t, None)

    _fox_block(qs, mk_ref[...], mv_ref[...], mck_ref[...],
               lambda c, r: c < N_META, m_ref, acc_ref)

    acc = acc_ref[...]
    _store_heads(o_ref, acc[:, :HEAD_DIM] / acc[:, HEAD_DIM:], t, gate_ref, sb_ref)


def _fox_prompt(q, kv, ck, mk, mv, mck, gates, sb_part, batch, seq):
    t = FOX_T
    nb = seq // t
    tile = pl.BlockSpec((t, GROUP * HEAD_DIM), lambda b, n, i: (b * nb + i, n))
    return pl.pallas_call(
        _fox_prompt_kernel,
        out_shape=jax.ShapeDtypeStruct((batch * seq, W_Q), BF16),
        grid=(batch, N_KV_HEADS, nb),
        in_specs=[tile,
                  _kv_head_spec(seq, 2),
                  _kv_head_spec(seq, 3),
                  pl.BlockSpec((None, None, GROUP, seq), lambda b, n, i: (b, n, 0, 0)),
                  pl.BlockSpec((LANES, HEAD_DIM), lambda b, n, i: (0, n)),
                  pl.BlockSpec((LANES, HEAD_DIM), lambda b, n, i: (0, n)),
                  pl.BlockSpec((None, None, GROUP, LANES), lambda b, n, i: (b, n, 0, 0)),
                  pl.BlockSpec((t, GROUP * HEAD_DIM), lambda b, n, i: (b * nb + i, N_KV_HEADS + n)),
                  tile],
        out_specs=tile,
        scratch_shapes=[pltpu.VMEM((GROUP * t, LANES), F32), pltpu.VMEM((GROUP * t, 2 * LANES), F32)],
        compiler_params=_params("parallel", "parallel", "arbitrary"),
        name="fox_prompt",
    )(q, kv, kv, ck, mk, mv, mck, gates, sb_part)


def _meta_attn_kernel(sq_ref, sk_ref, sv_ref, fq_ref, fk_ref, fv_ref, ck_ref, uo_ref, gsb_ref, gfx_ref,
                      o_ref, m_ref, acc_ref, r_ref, sacc_ref):
    t = N_META
    valid = lambda c: c < N_META
    r_ref[...] = jnp.zeros_like(r_ref)
    sacc_ref[...] = jnp.zeros_like(sacc_ref)
    _sb_block(_stack_heads(sq_ref[...]), sk_ref[...], sv_ref[...], uo_ref[...],
              lambda c, r: jnp.logical_and(c < r, valid(c)), r_ref, sacc_ref)
    o = sacc_ref[...]
    m_ref[...] = jnp.full_like(m_ref, NEG_BIG)
    acc_ref[...] = jnp.zeros_like(acc_ref)
    _fox_block(_stack_heads(fq_ref[...]), fk_ref[...], fv_ref[...], ck_ref[...],
               lambda c, r: jnp.logical_and(c <= r, valid(c)), m_ref, acc_ref)
    acc = acc_ref[...]
    f = acc[:, :HEAD_DIM] / acc[:, HEAD_DIM:]
    for g in range(GROUP):
        cols = slice(g * HEAD_DIM, (g + 1) * HEAD_DIM)
        merged = (gsb_ref[:, cols].astype(F32) * o[g * t:(g + 1) * t]
                  + gfx_ref[:, cols].astype(F32) * f[g * t:(g + 1) * t])
        o_ref[:, cols] = merged.astype(o_ref.dtype)


def _meta_attn(sq, sk, sv, fq, fk, fv, ck, uo, gates):
    qspec = pl.BlockSpec((N_META, GROUP * HEAD_DIM), lambda n: (0, n))
    kspec = pl.BlockSpec((LANES, HEAD_DIM), lambda n: (0, n))
    return pl.pallas_call(
        _meta_attn_kernel,
        out_shape=jax.ShapeDtypeStruct((N_META, W_Q), BF16),
        grid=(N_KV_HEADS,),
        in_specs=[qspec, kspec, kspec, qspec, kspec, kspec,
                  pl.BlockSpec((None, GROUP, LANES), lambda n: (n, 0, 0)),
                  pl.BlockSpec(uo.shape, lambda n: (0, 0)),
                  qspec,
                  pl.BlockSpec((N_META, GROUP * HEAD_DIM), lambda n: (0, N_KV_HEADS + n))],
        out_specs=qspec,
        scratch_shapes=[pltpu.VMEM((GROUP * N_META, LANES), F32),
                        pltpu.VMEM((GROUP * N_META, 2 * LANES), F32),
                        pltpu.VMEM((GROUP * N_META, LANES), F32),
                        pltpu.VMEM((GROUP * N_META, HEAD_DIM), F32)],
        compiler_params=_params("parallel"),
        name="meta_attn",
    )(sq, sk, sv, fq, fk, fv, ck, uo, gates, gates)


PAGE_COLS = PAGE_SIZE * N_KV_HEADS
DEC_G = 16


def _own_head(shape):
    row = lax.broadcasted_iota(jnp.int32, shape, 0)
    col = lax.broadcasted_iota(jnp.int32, shape, 1)
    return (col % N_KV_HEADS) == (row // GROUP)


def _sb_decode_kernel(pt_ref, q_ref, gate_ref, ck_hbm, cv_hbm, uo_ref, o_ref, kbuf, vbuf, sem, r_ref, acc_ref):
    b = pl.program_id(0)
    n_pages = pt_ref.shape[1]
    q = q_ref[...]
    uo = uo_ref[...]
    own = _own_head((N_Q_HEADS, PAGE_COLS))

    def copies(p, slot):
        pid = pt_ref[b, p]
        return [pltpu.make_async_copy(ck_hbm.at[pid], kbuf.at[slot], sem.at[0, slot]),
                pltpu.make_async_copy(cv_hbm.at[pid], vbuf.at[slot], sem.at[1, slot])]

    r_ref[...] = jnp.zeros_like(r_ref)
    acc_ref[...] = jnp.zeros_like(acc_ref)
    for c in copies(n_pages - 1, (n_pages - 1) % 2):
        c.start()

    def cond(c):
        return jnp.logical_and(c[0] >= 0, c[1] < SB_DEAD)

    def body(c):
        p = c[0]
        slot = p % 2
        for cp in copies(p, slot):
            cp.wait()

        @pl.when(p > 0)
        def _():
            for cp in copies(p - 1, 1 - slot):
                cp.start()

        z = lax.dot_general(q, kbuf[slot].astype(BF16), _NT, preferred_element_type=F32)
        sp = jnp.where(own, _softplus2(z), 0.0)
        ct = _split_dot(_split2(sp), uo)
        r = r_ref[...]
        a = jnp.where(own, jnp.exp2(z - ct[:, :PAGE_COLS] - jnp.tile(r, (1, N_KV_HEADS))), 0.0)
        acc_ref[...] += _dot(a.astype(BF16), vbuf[slot].astype(BF16))
        r_new = r + ct[:, PAGE_COLS:]
        r_ref[...] = r_new
        return p - 1, jnp.min(r_new)

    p_end, _ = lax.while_loop(cond, body, (jnp.int32(n_pages - 1), jnp.float32(0.0)))

    @pl.when(p_end >= 0)
    def _():
        for cp in copies(p_end, p_end % 2):
            cp.wait()

    o_ref[...] = (acc_ref[...] * gate_ref[...].astype(F32)).astype(o_ref.dtype)


def _sb_decode(page_table, q, gate, cache_k, cache_v, uo):
    nb = q.shape[0]
    any_spec = pl.BlockSpec(memory_space=pl.ANY)
    head_spec = pl.BlockSpec((None, N_Q_HEADS, HEAD_DIM), lambda b, pt: (b, 0, 0))
    return pl.pallas_call(
        _sb_decode_kernel,
        out_shape=jax.ShapeDtypeStruct((nb, N_Q_HEADS, HEAD_DIM), BF16),
        grid_spec=pltpu.PrefetchScalarGridSpec(
            num_scalar_prefetch=1,
            grid=(nb,),
            in_specs=[head_spec, head_spec, any_spec, any_spec,
                      pl.BlockSpec(uo.shape, lambda b, pt: (0, 0))],
            out_specs=head_spec,
            scratch_shapes=[pltpu.VMEM((2, PAGE_COLS, HEAD_DIM), F32),
                            pltpu.VMEM((2, PAGE_COLS, HEAD_DIM), F32),
                            pltpu.SemaphoreType.DMA((2, 2)),
                            pltpu.VMEM((N_Q_HEADS, LANES), F32),
                            pltpu.VMEM((N_Q_HEADS, HEAD_DIM), F32)]),
        compiler_params=_params("arbitrary"),
        name="sb_decode",
    )(page_table, q, gate, cache_k, cache_v, uo)


def _fox_decode_kernel(pt_ref, q_ref, kn_ref, vn_ref, lfn_ref, gate_ref, sb_ref, ck_hbm, cv_hbm, lf_hbm,
                       us_ref, o_ref, kbuf, vbuf, lbuf, sem, m_ref, acc_ref, d_ref):
    n_seq, n_pages = pt_ref.shape
    n_grp = n_pages // DEC_G
    total = n_seq * n_grp
    us = us_ref[...]
    bias = jnp.where(_own_head((N_Q_HEADS, PAGE_COLS)), 0.0, NEG_BIG)

    def copies(t, slot):
        b = t // n_grp
        newest = n_pages - 1 - (t % n_grp) * DEC_G
        out = []
        for g in range(DEC_G):
            pid = pt_ref[b, newest - g]
            out += [pltpu.make_async_copy(ck_hbm.at[pid], kbuf.at[slot, g], sem.at[0, slot]),
                    pltpu.make_async_copy(cv_hbm.at[pid], vbuf.at[slot, g], sem.at[1, slot]),
                    pltpu.make_async_copy(lf_hbm.at[pid], lbuf.at[slot, g], sem.at[2, slot])]
        return out

    for cp in copies(0, 0):
        cp.start()

    def body(t, carry):
        slot = t % 2
        b = t // n_grp
        gi = t % n_grp

        @pl.when(t + 1 < total)
        def _():
            for cp in copies(t + 1, 1 - slot):
                cp.start()

        q = q_ref[b]

        @pl.when(gi == 0)
        def _():
            z_self = jnp.sum(q.astype(F32) * kn_ref[b].astype(F32), axis=1, keepdims=True)
            m_ref[...] = jnp.broadcast_to(z_self, m_ref.shape)
            acc_ref[...] = jnp.concatenate([vn_ref[b].astype(F32), jnp.ones((N_Q_HEADS, LANES), F32)], axis=1)
            d_ref[...] = jnp.broadcast_to(lfn_ref[b] * LOG2E, d_ref.shape)

        for cp in copies(t, slot):
            cp.wait()

        dt = _split_dot(_split3(lbuf[slot].reshape(DEC_G * N_Q_HEADS, PAGE_SIZE)), us) * LOG2E
        d = d_ref[...]
        zs = []
        for g in range(DEC_G):
            dg = dt[g * N_Q_HEADS:(g + 1) * N_Q_HEADS]
            z = lax.dot_general(q, kbuf[slot, g].astype(BF16), _NT, preferred_element_type=F32)
            zs.append(z + dg[:, :PAGE_COLS] + jnp.tile(d, (1, N_KV_HEADS)) + bias)
            d = d + dg[:, PAGE_COLS:]
        d_ref[...] = d
        z = jnp.concatenate(zs, axis=1)
        m_prev = m_ref[...]
        m_new = jnp.maximum(m_prev, jnp.max(z, axis=1, keepdims=True))
        alpha = jnp.exp2(m_prev - m_new)
        p = jnp.exp2(z - jnp.tile(m_new, (1, z.shape[1] // LANES)))
        lsum = jnp.sum(p, axis=1, keepdims=True)
        pb = p.astype(BF16)
        pv = _dot(pb[:, :PAGE_COLS], vbuf[slot, 0].astype(BF16))
        for g in range(1, DEC_G):
            pv += _dot(pb[:, g * PAGE_COLS:(g + 1) * PAGE_COLS], vbuf[slot, g].astype(BF16))
        acc = acc_ref[...] * jnp.tile(alpha, (1, 2)) + jnp.concatenate(
            [pv, jnp.broadcast_to(lsum, pv.shape)], axis=1)
        acc_ref[...] = acc
        m_ref[...] = m_new

        @pl.when(gi == n_grp - 1)
        def _():
            merged = acc[:, :HEAD_DIM] / acc[:, HEAD_DIM:] * gate_ref[b].astype(F32) + sb_ref[b].astype(F32)
            o_ref[b] = merged.astype(o_ref.dtype)

        return carry

    lax.fori_loop(0, total, body, 0)


def _fox_decode(page_table, q, k_new, v_new, lf_new, gate, sb_part, cache_k, cache_v, cache_lf_t, us):
    nb = q.shape[0]
    assert page_table.shape[1] % DEC_G == 0
    any_spec = pl.BlockSpec(memory_space=pl.ANY)
    head_spec = pl.BlockSpec((nb, N_Q_HEADS, HEAD_DIM), lambda i, pt: (0, 0, 0))
    return pl.pallas_call(
        _fox_decode_kernel,
        out_shape=jax.ShapeDtypeStruct((nb, N_Q_HEADS, HEAD_DIM), BF16),
        grid_spec=pltpu.PrefetchScalarGridSpec(
            num_scalar_prefetch=1,
            grid=(1,),
            in_specs=[head_spec, head_spec, head_spec,
                      pl.BlockSpec((nb, N_Q_HEADS, 1), lambda i, pt: (0, 0, 0)),
                      head_spec, head_spec,
                      any_spec, any_spec, any_spec,
                      pl.BlockSpec(us.shape, lambda i, pt: (0, 0))],
            out_specs=head_spec,
            scratch_shapes=[pltpu.VMEM((2, DEC_G, PAGE_COLS, HEAD_DIM), F32),
                            pltpu.VMEM((2, DEC_G, PAGE_COLS, HEAD_DIM), F32),
                            pltpu.VMEM((2, DEC_G, N_Q_HEADS, PAGE_SIZE), F32),
                            pltpu.SemaphoreType.DMA((3, 2)),
                            pltpu.VMEM((N_Q_HEADS, LANES), F32),
                            pltpu.VMEM((N_Q_HEADS, 2 * LANES), F32),
                            pltpu.VMEM((N_Q_HEADS, LANES), F32)]),
        compiler_params=_params("arbitrary"),
        name="fox_decode",
    )(page_table, q, k_new, v_new, lf_new, gate, sb_part, cache_k, cache_v, cache_lf_t, us)


def _out_proj_kernel(m_ref, ma_ref, w_ref, x_ref, xa_ref, y_ref, ya_ref):
    w = w_ref[...]
    y_ref[...] = x_ref[...] + _dot(m_ref[...], w)
    ya_ref[...] = xa_ref[...] + _dot(ma_ref[...], w)


def _out_proj(merged, merged_aux, w, x, x_aux, tm):
    rows = x.shape[0]
    y, y_aux = pl.pallas_call(
        _out_proj_kernel,
        out_shape=[jax.ShapeDtypeStruct((rows, D_MODEL), F32),
                   jax.ShapeDtypeStruct((rows // tm, AUX_ROWS, D_MODEL), F32)],
        grid=(rows // tm, D_MODEL // TN),
        in_specs=[pl.BlockSpec((tm, W_Q), lambda i, j: (i, 0)),
                  pl.BlockSpec((AUX_ROWS, W_Q), lambda i, j: (0, 0)),
                  pl.BlockSpec((W_Q, TN), lambda i, j: (0, j)),
                  pl.BlockSpec((tm, TN), lambda i, j: (i, j)),
                  pl.BlockSpec((AUX_ROWS, TN), lambda i, j: (0, j))],
        out_specs=[pl.BlockSpec((tm, TN), lambda i, j: (i, j)),
                   pl.BlockSpec((None, AUX_ROWS, TN), lambda i, j: (i, 0, j))],
        compiler_params=_params("parallel", "arbitrary"),
        name="out_proj",
    )(merged, merged_aux, w, x, x_aux)
    return y, y_aux[0]


def _ffn_kernel(y_ref, ya_ref, g_ref, wu_ref, wd_ref, o_ref, oa_ref, h_ref, ha_ref):
    i, k = pl.program_id(0), pl.program_id(1)

    def start(src, h_dst, o_dst):
        y = src[...]
        h_dst[...] = _rms(y, g_ref[...]).astype(BF16)
        o_dst[...] = y

    def step(h_src, o_dst):
        u = jnp.square(jnp.maximum(_dot(h_src[...], wu_ref[...]), 0.0))
        o_dst[...] += _dot(u.astype(BF16), wd_ref[...])

    @pl.when(k == 0)
    def _():
        start(y_ref, h_ref, o_ref)

    @pl.when(jnp.logical_and(i == 0, k == 0))
    def _():
        start(ya_ref, ha_ref, oa_ref)

    step(h_ref, o_ref)

    @pl.when(i == 0)
    def _():
        step(ha_ref, oa_ref)


def _ffn(y, y_aux, g, w_up, w_down, tm):
    rows = y.shape[0]
    aux_spec = pl.BlockSpec((AUX_ROWS, D_MODEL), lambda i, k: (0, 0))
    return pl.pallas_call(
        _ffn_kernel,
        out_shape=[jax.ShapeDtypeStruct((rows, D_MODEL), F32), jax.ShapeDtypeStruct((AUX_ROWS, D_MODEL), F32)],
        grid=(rows // tm, D_FF // TF),
        in_specs=[pl.BlockSpec((tm, D_MODEL), lambda i, k: (i, 0), pipeline_mode=pl.Buffered(1)),
                  aux_spec,
                  pl.BlockSpec((1, D_MODEL), lambda i, k: (0, 0)),
                  pl.BlockSpec((D_MODEL, TF), lambda i, k: (0, k)),
                  pl.BlockSpec((TF, D_MODEL), lambda i, k: (k, 0))],
        out_specs=[pl.BlockSpec((tm, D_MODEL), lambda i, k: (i, 0)), aux_spec],
        scratch_shapes=[pltpu.VMEM((tm, D_MODEL), BF16), pltpu.VMEM((AUX_ROWS, D_MODEL), BF16)],
        compiler_params=pltpu.CompilerParams(dimension_semantics=("arbitrary", "arbitrary"),
                                             vmem_limit_bytes=FFN_VMEM_LIMIT),
        name="ffn",
    )(y, y_aux, g, w_up, w_down)


def _project_rows(x, x_aux, wts, batch, seq):
    norm = lambda rows, tm: _rms_forget(rows, wts["g_mix"], wts["w_fl_hi"], wts["w_fl_lo"], wts["b_f"], tm)
    h, lf = norm(x, TM_MAIN // 2)
    h_aux, lf_aux = norm(x_aux, AUX_ROWS)
    wt = wts["w_in_t"]
    sq = _proj(h, h_aux, wt, 0, W_Q, wts["g_q"], "scale", TM_PROJ, "proj_sq")
    fq = _proj(h, h_aux, wt, W_Q + 2 * W_KV, W_Q, wts["g_q"], "norm_scale", TM_PROJ, "proj_fq")
    gates = _proj(h, h_aux, wt, FORGET_OFFSET + N_Q_HEADS, 2 * D_MODEL, wts["g_q"], "sigmoid", TM_PROJ,
                  "proj_gates")
    kv = _kv_state_proj(h, h_aux, wt, wts["g_k"], TM_MAIN, batch, seq)
    return sq, fq, gates, kv, (lf, lf_aux)


def _finish_rows(x, x_aux, merged, merged_aux, wts):
    y1, y1_aux = _out_proj(merged, merged_aux, wts["w_out"], x, x_aux, TM_PROJ)
    return _ffn(y1, y1_aux, wts["g_ffn"], wts["w_up"], wts["w_down"], TM_MAIN)


def kernel(x_prompt, x_sample, cache_sb_k, cache_sb_v, cache_fox_k, cache_fox_v, cache_fox_logf,
           page_table, meta_tokens, g_mix, w_in, b_forget, g_q, g_k, w_out, g_ffn, w_up, w_down):
    batch, seq, _ = x_prompt.shape
    dec = x_sample.shape[0]
    assert w_in.shape[0] == 1 and x_sample.shape[1] == 1 and N_META + dec <= AUX_ROWS
    pool = cache_sb_k.shape[1]

    w_in_t = jnp.swapaxes(w_in[0], 0, 1)
    w_fl = jnp.pad(w_in_t[FORGET_OFFSET:FORGET_OFFSET + N_Q_HEADS], ((0, LANES - N_Q_HEADS), (0, 0)))
    w_fl_hi = w_fl.astype(BF16)
    wts = {
        "g_mix": g_mix, "g_q": g_q, "g_k": g_k, "g_ffn": g_ffn, "b_f": b_forget,
        "w_in_t": w_in_t,
        "w_fl_hi": w_fl_hi, "w_fl_lo": (w_fl - w_fl_hi.astype(F32)).astype(BF16),
        "w_out": w_out[0].astype(BF16), "w_up": w_up[0].astype(BF16), "w_down": w_down[0].astype(BF16),
    }

    x_main = x_prompt.reshape(batch * seq, D_MODEL)
    x_aux = jnp.concatenate([meta_tokens, x_sample.reshape(dec, D_MODEL),
                             jnp.zeros((AUX_ROWS - N_META - dec, D_MODEL), F32)], axis=0)

    (sq_m, sq_a), (fq_m, fq_a), (gates_m, gates_a), kv, (lf_m, lf_a) = _project_rows(
        x_main, x_aux, wts, batch, seq)
    (sk_st, sv_st, fk_st, fv_st), kvb_m, kv_a, kvb_a = kv
    four = lambda a: [a[:, t * W_KV:(t + 1) * W_KV] for t in range(4)]
    sk_a, sv_a, fk_a, fv_a = four(kv_a)
    skb_a, svb_a, fkb_a, fvb_a = four(kvb_a)

    lf_meta_t = jnp.pad(lf_a[:N_META].T, ((0, 0), (0, LANES - N_META)))
    lf_main_t = jnp.swapaxes(lf_m.reshape(batch, seq, N_Q_HEADS), 1, 2)
    ck_meta, ck_main = _cum_forget(lf_meta_t, lf_main_t)
    ck_meta = ck_meta.reshape(batch, N_KV_HEADS, GROUP, LANES)
    ck_main = ck_main.reshape(batch, N_KV_HEADS, GROUP, seq)

    pad_keys = lambda a: jnp.pad(a[:N_META], ((0, LANES - N_META), (0, 0)))
    mk_sb, mv_sb, mk_fx, mv_fx = pad_keys(skb_a), pad_keys(svb_a), pad_keys(fkb_a), pad_keys(fvb_a)

    j_idx = lax.broadcasted_iota(jnp.int32, (LANES, LANES), 0)
    s_idx = lax.broadcasted_iota(jnp.int32, (LANES, LANES), 1)
    ones = jnp.ones((LANES, LANES), BF16)
    uo = jnp.concatenate([(j_idx >= s_idx).astype(BF16), ones], axis=1)
    uo = jnp.concatenate([uo] * 2, axis=0)
    c_src = lax.broadcasted_iota(jnp.int32, (PAGE_COLS, PAGE_COLS), 0) // N_KV_HEADS
    c_dst = lax.broadcasted_iota(jnp.int32, (PAGE_COLS, PAGE_COLS), 1) // N_KV_HEADS
    uo_page = jnp.concatenate([(c_src >= c_dst).astype(BF16), jnp.ones((PAGE_COLS, LANES), BF16)], axis=1)
    uo_page = jnp.concatenate([uo_page] * 2, axis=0)
    j_key = lax.broadcasted_iota(jnp.int32, (PAGE_SIZE, PAGE_COLS), 0)
    c_key = lax.broadcasted_iota(jnp.int32, (PAGE_SIZE, PAGE_COLS), 1) // N_KV_HEADS
    us_page = jnp.concatenate([(j_key > c_key).astype(BF16), ones], axis=1)
    us_page = jnp.concatenate([us_page] * 3, axis=0)

    sb_m = _sb_prompt(sq_m, kvb_m, mk_sb, mv_sb, uo, gates_m, batch, seq)
    merged_m = _fox_prompt(fq_m, kvb_m, ck_main, mk_fx, mv_fx, ck_meta, gates_m, sb_m, batch, seq)
    merged_meta = _meta_attn(sq_a[:N_META], mk_sb, mv_sb, fq_a[:N_META], mk_fx, mv_fx,
                             ck_meta[0], uo, gates_a[:N_META])

    heads = lambda a: a[N_META:N_META + dec].reshape(dec, N_Q_HEADS, HEAD_DIM)
    kv_heads = lambda a: jnp.repeat(a[N_META:N_META + dec].reshape(dec, N_KV_HEADS, HEAD_DIM), GROUP, axis=1)
    pages = lambda c: c[0].reshape(pool, PAGE_COLS, HEAD_DIM)
    sb_dec = _sb_decode(page_table, heads(sq_a), heads(gates_a[:, :W_Q]),
                        pages(cache_sb_k), pages(cache_sb_v), uo_page)
    merged_dec = _fox_decode(page_table, heads(fq_a), kv_heads(fkb_a), kv_heads(fvb_a),
                             lf_a[N_META:N_META + dec].reshape(dec, N_Q_HEADS, 1),
                             heads(gates_a[:, W_Q:]), sb_dec,
                             pages(cache_fox_k), pages(cache_fox_v),
                             jnp.swapaxes(cache_fox_logf[0], 1, 2), us_page)

    tail = jnp.zeros((AUX_ROWS - N_META - dec, W_Q), BF16)
    merged_a = jnp.concatenate([merged_meta, merged_dec.reshape(dec, W_Q), tail], axis=0)

    y_main, y_aux = _finish_rows(x_main, x_aux, merged_m, merged_a, wts)

    def prompt_state(a_aux, a_main, tail_shape):
        meta = jnp.broadcast_to(a_aux[None, :N_META], (batch, N_META, a_aux.shape[1]))
        full = jnp.concatenate([meta, a_main.reshape(batch, seq, a_aux.shape[1])], axis=1)
        return full.reshape((1, batch, seq + N_META) + tail_shape)

    kv_shape = (N_KV_HEADS, HEAD_DIM)
    kv_state = lambda st: st.reshape((1, batch, seq + N_META) + kv_shape)
    sample_state = lambda a, tail_shape: a[N_META:N_META + dec].reshape((1, dec, 1) + tail_shape)
    return (y_main.reshape(batch, seq, D_MODEL), y_aux[N_META:N_META + dec].reshape(dec, 1, D_MODEL),
            kv_state(sk_st), kv_state(sv_st), kv_state(fk_st), kv_state(fv_st),
            prompt_state(lf_a, lf_m, (N_Q_HEADS,)),
            sample_state(sk_a, kv_shape), sample_state(sv_a, kv_shape),
            sample_state(fk_a, kv_shape), sample_state(fv_a, kv_shape),
            sample_state(lf_a, (N_Q_HEADS,)))
```

```python
import functools

import jax
import jax.numpy as jnp
from jax import lax
from jax.experimental import pallas as pl
from jax.experimental.pallas import tpu as pltpu

D_MODEL = 2048
HEAD_DIM = 128
N_Q_HEADS = 16
N_KV_HEADS = 4
GROUP = 4
W_Q = N_Q_HEADS * HEAD_DIM
W_KV = N_KV_HEADS * HEAD_DIM
D_FF = 4 * D_MODEL
N_META = 16
PAGE_SIZE = 128
EPS = 1e-6
LOG2E = 1.4426950408889634
Q_SCALE = HEAD_DIM ** -0.5 * LOG2E
FORGET_OFFSET = 2 * W_Q + 4 * W_KV
AUX_ROWS = 32

BF16 = jnp.bfloat16
F32 = jnp.float32

LANES = 128
SUBLANES = 8
NEG_BIG = -1e30
SB_DEAD = 104.0 * LOG2E
VMEM_LIMIT = 48 * 1024 * 1024
FFN_VMEM_LIMIT = 58 * 1024 * 1024
TM_MAIN = 1024
TM_PROJ = 2048
TN = 512
TF = 1024
PROJ_CHUNK = 512

_NT = (((1,), (1,)), ((), ()))


def _params(*sem):
    return pltpu.CompilerParams(dimension_semantics=sem, vmem_limit_bytes=VMEM_LIMIT)


def _rms(x, g):
    ms = jnp.mean(x * x, axis=-1, keepdims=True)
    return x * lax.rsqrt(ms + EPS) * g


def _softplus(z):
    return jnp.maximum(z, 0.0) + jnp.log1p(jnp.exp(-jnp.abs(z)))


def _softplus2(z2):
    return jnp.maximum(z2, 0.0) + jnp.log2(1.0 + jnp.exp2(-jnp.abs(z2)))


def _split2(x):
    hi = x.astype(BF16)
    lo = (x - hi.astype(F32)).astype(BF16)
    return hi, lo


def _split3(x):
    hi = x.astype(BF16)
    r = x - hi.astype(F32)
    mid = r.astype(BF16)
    lo = (r - mid.astype(F32)).astype(BF16)
    return hi, mid, lo


def _dot(a, b):
    return jnp.dot(a, b, preferred_element_type=F32)


def _dot_t(a, bt):
    return lax.dot_general(a, bt, _NT, preferred_element_type=F32)


def _split_dot(parts, m_stacked):
    return _dot(jnp.concatenate(parts, axis=1), m_stacked)


def _rms_forget_kernel(x_ref, g_ref, whi_ref, wlo_ref, b_ref, h_ref, lf_ref):
    h = _rms(x_ref[...], g_ref[...])
    h_ref[...] = h.astype(h_ref.dtype)
    hh, hl = _split2(h)
    z = _dot_t(hh, whi_ref[...]) + _dot_t(hl, whi_ref[...]) + _dot_t(hh, wlo_ref[...])
    z = z[:, :N_Q_HEADS] + b_ref[...]
    lf_ref[...] = -_softplus(-z)


def _rms_forget(x, g, w_hi, w_lo, b, tm):
    rows = x.shape[0]
    const = lambda shape: pl.BlockSpec(shape, lambda i: (0, 0))
    return pl.pallas_call(
        _rms_forget_kernel,
        out_shape=[jax.ShapeDtypeStruct((rows, D_MODEL), BF16),
                   jax.ShapeDtypeStruct((rows, N_Q_HEADS), F32)],
        grid=(rows // tm,),
        in_specs=[pl.BlockSpec((tm, D_MODEL), lambda i: (i, 0)),
                  const((1, D_MODEL)), const((LANES, D_MODEL)), const((LANES, D_MODEL)),
                  const((1, N_Q_HEADS))],
        out_specs=[pl.BlockSpec((tm, D_MODEL), lambda i: (i, 0)),
                   pl.BlockSpec((tm, N_Q_HEADS), lambda i: (i, 0))],
        compiler_params=_params("parallel"),
        name="rms_forget",
    )(x, g, w_hi, w_lo, b)


def _head_norm(acc, g):
    outs = []
    for hh in range(acc.shape[1] // HEAD_DIM):
        xh = acc[:, hh * HEAD_DIM:(hh + 1) * HEAD_DIM]
        outs.append(_rms(xh, g))
    return jnp.concatenate(outs, axis=1)


def _proj_kernel(h_ref, ha_ref, wt_ref, g_ref, o_ref, oa_ref, *, mode):
    wt = wt_ref[...].astype(BF16)
    tm = h_ref.shape[0]
    chunk = min(tm, PROJ_CHUNK)
    pieces = [(h_ref, o_ref, slice(c * chunk, (c + 1) * chunk)) for c in range(tm // chunk)]
    pieces.append((ha_ref, oa_ref, slice(None)))
    acc_next = _dot_t(h_ref[pieces[0][2]], wt)
    for n, (_, dst, rows) in enumerate(pieces):
        acc = acc_next
        if n + 1 < len(pieces):
            src, _, nxt = pieces[n + 1]
            acc_next = _dot_t(src[nxt], wt)
        if mode == "scale":
            acc = acc * Q_SCALE
        elif mode == "norm_scale":
            acc = _head_norm(acc, g_ref[...]) * Q_SCALE
        elif mode == "sigmoid":
            acc = jax.nn.sigmoid(acc)
        dst[rows] = acc.astype(dst.dtype)


def _wt_rows_spec(row0):
    if row0 % TN == 0:
        return pl.BlockSpec((TN, D_MODEL), lambda i, j: (row0 // TN + j, 0))
    assert row0 % SUBLANES == 0
    return pl.BlockSpec((pl.Element(TN), pl.Element(D_MODEL)),
                        lambda i, j: (pl.multiple_of(row0 + j * TN, SUBLANES), 0))


def _proj(h, h_aux, wt, row0, n, g, mode, tm, name):
    rows = h.shape[0]
    out, out_aux = pl.pallas_call(
        functools.partial(_proj_kernel, mode=mode),
        out_shape=[jax.ShapeDtypeStruct((rows, n), BF16),
                   jax.ShapeDtypeStruct((rows // tm, AUX_ROWS, n), BF16)],
        grid=(rows // tm, n // TN),
        in_specs=[pl.BlockSpec((tm, D_MODEL), lambda i, j: (i, 0)),
                  pl.BlockSpec((AUX_ROWS, D_MODEL), lambda i, j: (0, 0)),
                  _wt_rows_spec(row0),
                  pl.BlockSpec((1, HEAD_DIM), lambda i, j: (0, 0))],
        out_specs=[pl.BlockSpec((tm, TN), lambda i, j: (i, j)),
                   pl.BlockSpec((None, AUX_ROWS, TN), lambda i, j: (i, 0, j))],
        compiler_params=_params("parallel", "arbitrary"),
        name=name,
    )(h, h_aux, wt, g)
    return out, out_aux[0]


def _kv_wt_spec():
    sb_kv, fox_kv = W_Q // W_KV, (2 * W_Q + 2 * W_KV) // W_KV
    return pl.BlockSpec((W_KV, D_MODEL), lambda i, j: (sb_kv + j + (fox_kv - sb_kv - 2) * (j // 2), 0))


def _kv_state_kernel(h_ref, ha_ref, wt_ref, gk_ref, *refs, tiles_per_batch):
    state = refs[:4]
    bf_ref, aux_f32_ref, aux_bf_ref, stage, meta_stage, sem, meta_sem = refs[4:]
    i, j = pl.program_id(0), pl.program_id(1)
    n_i = pl.num_programs(0)
    tm = h_ref.shape[0]
    b, it = i // tiles_per_batch, i % tiles_per_batch
    row0 = pl.multiple_of(N_META * N_KV_HEADS + it * tm * N_KV_HEADS, SUBLANES)
    slot = j % 2
    wt = wt_ref[...].astype(BF16)
    finish = lambda a: jnp.where(j == 2, _head_norm(a, gk_ref[...]), a)
    val = finish(_dot_t(h_ref[...], wt))
    val_aux = finish(_dot_t(ha_ref[...], wt))
    bf_ref[...] = val.astype(BF16)
    aux_f32_ref[...] = val_aux
    aux_bf_ref[...] = val_aux.astype(BF16)

    def tile_copy(jj):
        return pltpu.make_async_copy(stage.at[slot], state[jj].at[b, pl.ds(row0, tm * N_KV_HEADS), :],
                                     sem.at[slot])

    def meta_copy(jj):
        return pltpu.make_async_copy(meta_stage.at[jj], state[jj].at[b, pl.ds(0, N_META * N_KV_HEADS), :],
                                     meta_sem.at[jj])

    def per_tensor(fn):
        for jj in range(4):
            pl.when(j == jj)(functools.partial(fn, jj))

    def interleave(dst, val, rows):
        for n in range(N_KV_HEADS):
            dst[pl.ds(n, rows, stride=N_KV_HEADS), :] = val[:rows, n * HEAD_DIM:(n + 1) * HEAD_DIM]

    @pl.when(jnp.logical_or(i > 0, j >= 2))
    def _():
        per_tensor(lambda jj: tile_copy((jj + 2) % 4).wait())

    interleave(stage.at[slot], val, tm)
    per_tensor(lambda jj: tile_copy(jj).start())

    @pl.when(it == 0)
    def _():
        @pl.when(b > 0)
        def _():
            per_tensor(lambda jj: meta_copy(jj).wait())

        interleave(meta_stage.at[j], val_aux, N_META)
        per_tensor(lambda jj: meta_copy(jj).start())

    @pl.when(jnp.logical_and(i == n_i - 1, j == 3))
    def _():
        pltpu.make_async_copy(stage.at[0], state[2].at[b, pl.ds(row0, tm * N_KV_HEADS), :], sem.at[0]).wait()
        tile_copy(3).wait()
        for jj in range(4):
            meta_copy(jj).wait()


def _kv_state_proj(h, h_aux, wt, g_k, tm, batch, seq):
    rows = h.shape[0]
    tiles_per_batch = seq // tm
    state_rows = (N_META + seq) * N_KV_HEADS
    any_spec = pl.BlockSpec(memory_space=pl.ANY)
    aux_spec = pl.BlockSpec((None, AUX_ROWS, W_KV), lambda i, j: (i, 0, j))
    outs = pl.pallas_call(
        functools.partial(_kv_state_kernel, tiles_per_batch=tiles_per_batch),
        out_shape=[jax.ShapeDtypeStruct((batch, state_rows, HEAD_DIM), F32)] * 4
        + [jax.ShapeDtypeStruct((rows, 4 * W_KV), BF16),
           jax.ShapeDtypeStruct((rows // tm, AUX_ROWS, 4 * W_KV), F32),
           jax.ShapeDtypeStruct((rows // tm, AUX_ROWS, 4 * W_KV), BF16)],
        grid=(rows // tm, 4),
        in_specs=[pl.BlockSpec((tm, D_MODEL), lambda i, j: (i, 0)),
                  pl.BlockSpec((AUX_ROWS, D_MODEL), lambda i, j: (0, 0)),
                  _kv_wt_spec(),
                  pl.BlockSpec((1, HEAD_DIM), lambda i, j: (0, 0))],
        out_specs=[any_spec] * 4 + [pl.BlockSpec((tm, W_KV), lambda i, j: (i, j)), aux_spec, aux_spec],
        scratch_shapes=[pltpu.VMEM((2, tm * N_KV_HEADS, HEAD_DIM), F32),
                        pltpu.VMEM((4, N_META * N_KV_HEADS, HEAD_DIM), F32),
                        pltpu.SemaphoreType.DMA((2,)),
                        pltpu.SemaphoreType.DMA((4,))],
        compiler_params=_params("arbitrary", "arbitrary"),
        name="kv_state_proj",
    )(h, h_aux, wt, g_k)
    return outs[:4], outs[4], outs[5][0], outs[6][0]


def _lane_cumsum(x):
    n = x.shape[-1]
    lane = lax.broadcasted_iota(jnp.int32, x.shape, x.ndim - 1)
    k = 1
    while k < n:
        x = x + jnp.where(lane >= k, pltpu.roll(x, k, axis=x.ndim - 1), 0.0)
        k *= 2
    return x


def _cum_kernel(meta_ref, main_ref, cmeta_ref, cmain_ref):
    cm = _lane_cumsum(meta_ref[...])
    cmeta_ref[...] = cm * LOG2E
    cmain_ref[...] = (_lane_cumsum(main_ref[...]) + cm[:, LANES - 1:LANES]) * LOG2E


def _cum_forget(lf_meta_t, lf_main_t):
    b, _, s = lf_main_t.shape
    return pl.pallas_call(
        _cum_kernel,
        out_shape=[jax.ShapeDtypeStruct((b, N_Q_HEADS, LANES), F32),
                   jax.ShapeDtypeStruct((b, N_Q_HEADS, s), F32)],
        grid=(b,),
        in_specs=[pl.BlockSpec((N_Q_HEADS, LANES), lambda i: (0, 0)),
                  pl.BlockSpec((None, N_Q_HEADS, s), lambda i: (i, 0, 0))],
        out_specs=[pl.BlockSpec((None, N_Q_HEADS, LANES), lambda i: (i, 0, 0)),
                   pl.BlockSpec((None, N_Q_HEADS, s), lambda i: (i, 0, 0))],
        compiler_params=_params("parallel"),
        name="cum_forget",
    )(lf_meta_t, lf_main_t)


def _stack_heads(q):
    return jnp.concatenate([q[:, g * HEAD_DIM:(g + 1) * HEAD_DIM] for g in range(GROUP)], axis=0)


SB_CHUNK = 512
FOX_CHUNK = 512


def _chunk_visible(rel, c, cr, t, tk):
    col = lax.broadcasted_iota(jnp.int32, (cr, tk), 1)
    row = (lax.broadcasted_iota(jnp.int32, (cr, tk), 0) + c * cr) % t
    return rel(col, row)


def _sb_block(qs, kb, vb, uo, rel, r_ref, acc_ref):
    r, tk = qs.shape[0], kb.shape[0]
    t = r // GROUP
    cr = min(r, SB_CHUNK)
    score = lambda c: lax.dot_general(qs[c * cr:(c + 1) * cr], kb, _NT, preferred_element_type=F32)
    z_next = score(0)
    for c in range(r // cr):
        rows = slice(c * cr, (c + 1) * cr)
        z = z_next
        if c + 1 < r // cr:
            z_next = score(c + 1)
        sp = _softplus2(z)
        if rel is not None:
            vis = _chunk_visible(rel, c, cr, t, tk)
            sp = jnp.where(vis, sp, 0.0)
        ct = _split_dot(_split2(sp), uo)
        r_prev = r_ref[rows]
        a = jnp.exp2(z - ct[:, :tk] - r_prev)
        if rel is not None:
            a = jnp.where(vis, a, 0.0)
        acc_ref[rows] += _dot(a.astype(BF16), vb)
        r_ref[rows] = r_prev + ct[:, tk:]


def _sb_window(qs, kb, vb, uo, r_ref, acc_ref):
    r = qs.shape[0]
    t = r // GROUP
    nb = kb.shape[0] // t
    z = _dot_t(qs, kb)
    sp = _softplus2(z)
    vis = _chunk_visible(lambda c, row: c < row, 0, r, t, t)
    newer = None
    weights = [None] * nb
    for b in reversed(range(nb)):
        cols = slice(b * t, (b + 1) * t)
        diag = b == nb - 1
        spb = jnp.where(vis, sp[:, cols], 0.0) if diag else sp[:, cols]
        ct = _split_dot(_split2(spb), uo)
        arg = z[:, cols] - ct[:, :t]
        a = jnp.exp2(arg if newer is None else arg - newer)
        weights[b] = (jnp.where(vis, a, 0.0) if diag else a).astype(BF16)
        newer = ct[:, t:] if newer is None else newer + ct[:, t:]
    acc_ref[...] = _dot(jnp.concatenate(weights, axis=1), vb)
    r_ref[...] = newer


def _fox_block(qs, kb, vb, ckb, rel, m_ref, acc_ref):
    r, tk = qs.shape[0], kb.shape[0]
    t = r // GROUP
    cr = min(r, FOX_CHUNK)
    hc = max(cr // t, 1)
    v1 = jnp.concatenate([vb, jnp.ones((tk, LANES), BF16)], axis=1)
    score = lambda c: lax.dot_general(qs[c * cr:(c + 1) * cr], kb, _NT, preferred_element_type=F32)
    z_next = score(0)
    for c in range(r // cr):
        rows = slice(c * cr, (c + 1) * cr)
        g = c * cr // t
        bias = ckb[g:g + hc]
        z = (z_next.reshape(hc, cr // hc, tk) - bias[:, None, :]).reshape(cr, tk)
        if c + 1 < r // cr:
            z_next = score(c + 1)
        if rel is not None:
            z = jnp.where(_chunk_visible(rel, c, cr, t, tk), z, NEG_BIG)
        m_prev = m_ref[rows]
        m_new = jnp.maximum(m_prev, jnp.max(z, axis=1, keepdims=True))
        alpha = jnp.exp2(m_prev - m_new)
        p = jnp.exp2(z - jnp.tile(m_new, (1, tk // LANES)))
        acc_ref[rows] = acc_ref[rows] * jnp.tile(alpha, (1, 2)) + _dot(p.astype(BF16), v1)
        m_ref[rows] = m_new


SB_T = 128
SB_WIN = 3
SB_SUB = 4


def _store_heads(o_ref, o, t, gate_ref, add_ref=None, row0=0):
    rows = slice(row0, row0 + t)
    for g in range(GROUP):
        cols = slice(g * HEAD_DIM, (g + 1) * HEAD_DIM)
        val = o[g * t:(g + 1) * t] * gate_ref[rows, cols].astype(F32)
        if add_ref is not None:
            val = val + add_ref[rows, cols].astype(F32)
        o_ref[rows, cols] = val.astype(o_ref.dtype)


def _sb_prompt_kernel(q_ref, k_ref, v_ref, mk_ref, mv_ref, uo_ref, gate_ref, o_ref, r_ref, acc_ref):
    i = pl.program_id(2)
    t = SB_T
    uo = uo_ref[...]
    qs = [_stack_heads(q_ref[sub * t:(sub + 1) * t]) for sub in range(SB_SUB)]
    tile = lambda sub: i * SB_SUB + sub

    def first_block_only(sub):
        r_ref[sub] = jnp.zeros(r_ref.shape[1:], F32)
        acc_ref[sub] = jnp.zeros(acc_ref.shape[1:], F32)
        s = pl.multiple_of(tile(sub) * t, t)
        _sb_block(qs[sub], k_ref[pl.ds(s, t), :], v_ref[pl.ds(s, t), :], uo, lambda c, r: c < r,
                  r_ref.at[sub], acc_ref.at[sub])

    assert SB_SUB >= SB_WIN - 1

    @pl.when(i > 0)
    def _():
        for sub in range(SB_SUB):
            s = pl.multiple_of((tile(sub) - (SB_WIN - 1)) * t, t)
            _sb_window(qs[sub], k_ref[pl.ds(s, SB_WIN * t), :], v_ref[pl.ds(s, SB_WIN * t), :], uo,
                       r_ref.at[sub], acc_ref.at[sub])

    @pl.when(i == 0)
    def _():
        for sub in range(SB_SUB):
            if sub >= SB_WIN - 1:
                _sb_window(qs[sub], k_ref[pl.ds((sub - (SB_WIN - 1)) * t, SB_WIN * t), :],
                           v_ref[pl.ds((sub - (SB_WIN - 1)) * t, SB_WIN * t), :], uo,
                           r_ref.at[sub], acc_ref.at[sub])
            else:
                first_block_only(sub)

    for sub in range(SB_SUB):
        def visit(kb, vb, rel, sub=sub):
            _sb_block(qs[sub], kb, vb, uo, rel, r_ref.at[sub], acc_ref.at[sub])
            return jnp.min(r_ref[sub])

        def cond(c):
            return jnp.logical_and(c[0] >= 0, c[1] < SB_DEAD)

        def body(c, visit=visit):
            s = pl.multiple_of(c[0] * t, t)
            return c[0] - 1, visit(k_ref[pl.ds(s, t), :], v_ref[pl.ds(s, t), :], None)

        covered = jnp.where(jnp.logical_or(i > 0, sub >= SB_WIN - 1), SB_WIN, 1)
        _, mn = lax.while_loop(cond, body, (tile(sub) - covered, jnp.min(r_ref[sub])))

        @pl.when(mn < SB_DEAD)
        def _(visit=visit):
            visit(mk_ref[...], mv_ref[...], lambda c, r: c < N_META)

        _store_heads(o_ref, acc_ref[sub], t, gate_ref, row0=sub * t)


def _kv_head_spec(seq, tensor):
    return pl.BlockSpec((seq, HEAD_DIM), lambda b, n, i: (b, tensor * N_KV_HEADS + n))


def _sb_prompt(q, kv, mk, mv, uo, gates, batch, seq):
    t = SB_T * SB_SUB
    nb = seq // t
    tile = pl.BlockSpec((t, GROUP * HEAD_DIM), lambda b, n, i: (b * nb + i, n))
    return pl.pallas_call(
        _sb_prompt_kernel,
        out_shape=jax.ShapeDtypeStruct((batch * seq, W_Q), BF16),
        grid=(batch, N_KV_HEADS, nb),
        in_specs=[tile,
                  _kv_head_spec(seq, 0),
                  _kv_head_spec(seq, 1),
                  pl.BlockSpec((LANES, HEAD_DIM), lambda b, n, i: (0, n)),
                  pl.BlockSpec((LANES, HEAD_DIM), lambda b, n, i: (0, n)),
                  pl.BlockSpec(uo.shape, lambda b, n, i: (0, 0)),
                  tile],
        out_specs=tile,
        scratch_shapes=[pltpu.VMEM((SB_SUB, GROUP * SB_T, SB_T), F32),
                        pltpu.VMEM((SB_SUB, GROUP * SB_T, HEAD_DIM), F32)],
        compiler_params=_params("parallel", "parallel", "arbitrary"),
        name="sb_prompt",
    )(q, kv, kv, mk, mv, uo, gates)


FOX_T = 512
FOX_TK = 512


def _fox_prompt_kernel(q_ref, k_ref, v_ref, ck_ref, mk_ref, mv_ref, mck_ref, gate_ref, sb_ref,
                       o_ref, m_ref, acc_ref):
    i = pl.program_id(2)
    t = FOX_T
    qs = _stack_heads(q_ref[...])
    m_ref[...] = jnp.full_like(m_ref, NEG_BIG)
    acc_ref[...] = jnp.zeros_like(acc_ref)

    def visit(s, width, rel):
        _fox_block(qs, k_ref[pl.ds(s, width), :], v_ref[pl.ds(s, width), :], ck_ref[:, pl.ds(s, width)],
                   rel, m_ref, acc_ref)

    visit(pl.multiple_of(i * t, t), t, lambda c, r: c <= r)

    per_visit = FOX_TK // t

    def body(j, carry):
        visit(pl.multiple_of(j * FOX_TK, FOX_TK), FOX_TK, None)
        return carry

    lax.fori_loop(0, i // per_visit, body, 0)

    for left in range(per_visit - 1):
        @pl.when(i % per_visit > left)
        def _(left=left):
            visit(pl.multiple_of((i - i % per_visit + left) * t, t), t, None)

    _fox_block(qs, mk_ref[...], mv_ref[...], mck_ref[...],
               lambda c, r: c < N_META, m_ref, acc_ref)

    acc = acc_ref[...]
    _store_heads(o_ref, acc[:, :HEAD_DIM] / acc[:, HEAD_DIM:], t, gate_ref, sb_ref)


def _fox_prompt(q, kv, ck, mk, mv, mck, gates, sb_part, batch, seq):
    t = FOX_T
    nb = seq // t
    tile = pl.BlockSpec((t, GROUP * HEAD_DIM), lambda b, n, i: (b * nb + i, n))
    return pl.pallas_call(
        _fox_prompt_kernel,
        out_shape=jax.ShapeDtypeStruct((batch * seq, W_Q), BF16),
        grid=(batch, N_KV_HEADS, nb),
        in_specs=[tile,
                  _kv_head_spec(seq, 2),
                  _kv_head_spec(seq, 3),
                  pl.BlockSpec((None, None, GROUP, seq), lambda b, n, i: (b, n, 0, 0)),
                  pl.BlockSpec((LANES, HEAD_DIM), lambda b, n, i: (0, n)),
                  pl.BlockSpec((LANES, HEAD_DIM), lambda b, n, i: (0, n)),
                  pl.BlockSpec((None, None, GROUP, LANES), lambda b, n, i: (b, n, 0, 0)),
                  pl.BlockSpec((t, GROUP * HEAD_DIM), lambda b, n, i: (b * nb + i, N_KV_HEADS + n)),
                  tile],
        out_specs=tile,
        scratch_shapes=[pltpu.VMEM((GROUP * t, LANES), F32), pltpu.VMEM((GROUP * t, 2 * LANES), F32)],
        compiler_params=_params("parallel", "parallel", "arbitrary"),
        name="fox_prompt",
    )(q, kv, kv, ck, mk, mv, mck, gates, sb_part)


def _meta_attn_kernel(sq_ref, sk_ref, sv_ref, fq_ref, fk_ref, fv_ref, ck_ref, uo_ref, gsb_ref, gfx_ref,
                      o_ref, m_ref, acc_ref, r_ref, sacc_ref):
    t = N_META
    valid = lambda c: c < N_META
    r_ref[...] = jnp.zeros_like(r_ref)
    sacc_ref[...] = jnp.zeros_like(sacc_ref)
    _sb_block(_stack_heads(sq_ref[...]), sk_ref[...], sv_ref[...], uo_ref[...],
              lambda c, r: jnp.logical_and(c < r, valid(c)), r_ref, sacc_ref)
    o = sacc_ref[...]
    m_ref[...] = jnp.full_like(m_ref, NEG_BIG)
    acc_ref[...] = jnp.zeros_like(acc_ref)
    _fox_block(_stack_heads(fq_ref[...]), fk_ref[...], fv_ref[...], ck_ref[...],
               lambda c, r: jnp.logical_and(c <= r, valid(c)), m_ref, acc_ref)
    acc = acc_ref[...]
    f = acc[:, :HEAD_DIM] / acc[:, HEAD_DIM:]
    for g in range(GROUP):
        cols = slice(g * HEAD_DIM, (g + 1) * HEAD_DIM)
        merged = (gsb_ref[:, cols].astype(F32) * o[g * t:(g + 1) * t]
                  + gfx_ref[:, cols].astype(F32) * f[g * t:(g + 1) * t])
        o_ref[:, cols] = merged.astype(o_ref.dtype)


def _meta_attn(sq, sk, sv, fq, fk, fv, ck, uo, gates):
    qspec = pl.BlockSpec((N_META, GROUP * HEAD_DIM), lambda n: (0, n))
    kspec = pl.BlockSpec((LANES, HEAD_DIM), lambda n: (0, n))
    return pl.pallas_call(
        _meta_attn_kernel,
        out_shape=jax.ShapeDtypeStruct((N_META, W_Q), BF16),
        grid=(N_KV_HEADS,),
        in_specs=[qspec, kspec, kspec, qspec, kspec, kspec,
                  pl.BlockSpec((None, GROUP, LANES), lambda n: (n, 0, 0)),
                  pl.BlockSpec(uo.shape, lambda n: (0, 0)),
                  qspec,
                  pl.BlockSpec((N_META, GROUP * HEAD_DIM), lambda n: (0, N_KV_HEADS + n))],
        out_specs=qspec,
        scratch_shapes=[pltpu.VMEM((GROUP * N_META, LANES), F32),
                        pltpu.VMEM((GROUP * N_META, 2 * LANES), F32),
                        pltpu.VMEM((GROUP * N_META, LANES), F32),
                        pltpu.VMEM((GROUP * N_META, HEAD_DIM), F32)],
        compiler_params=_params("parallel"),
        name="meta_attn",
    )(sq, sk, sv, fq, fk, fv, ck, uo, gates, gates)


PAGE_COLS = PAGE_SIZE * N_KV_HEADS
DEC_G = 16


def _own_head(shape):
    row = lax.broadcasted_iota(jnp.int32, shape, 0)
    col = lax.broadcasted_iota(jnp.int32, shape, 1)
    return (col % N_KV_HEADS) == (row // GROUP)


def _sb_decode_kernel(pt_ref, q_ref, gate_ref, ck_hbm, cv_hbm, uo_ref, o_ref, kbuf, vbuf, sem, r_ref, acc_ref):
    b = pl.program_id(0)
    n_pages = pt_ref.shape[1]
    q = q_ref[...]
    uo = uo_ref[...]
    own = _own_head((N_Q_HEADS, PAGE_COLS))

    def copies(p, slot):
        pid = pt_ref[b, p]
        return [pltpu.make_async_copy(ck_hbm.at[pid], kbuf.at[slot], sem.at[0, slot]),
                pltpu.make_async_copy(cv_hbm.at[pid], vbuf.at[slot], sem.at[1, slot])]

    r_ref[...] = jnp.zeros_like(r_ref)
    acc_ref[...] = jnp.zeros_like(acc_ref)
    for c in copies(n_pages - 1, (n_pages - 1) % 2):
        c.start()

    def cond(c):
        return jnp.logical_and(c[0] >= 0, c[1] < SB_DEAD)

    def body(c):
        p = c[0]
        slot = p % 2
        for cp in copies(p, slot):
            cp.wait()

        @pl.when(p > 0)
        def _():
            for cp in copies(p - 1, 1 - slot):
                cp.start()

        z = lax.dot_general(q, kbuf[slot].astype(BF16), _NT, preferred_element_type=F32)
        sp = jnp.where(own, _softplus2(z), 0.0)
        ct = _split_dot(_split2(sp), uo)
        r = r_ref[...]
        a = jnp.where(own, jnp.exp2(z - ct[:, :PAGE_COLS] - jnp.tile(r, (1, N_KV_HEADS))), 0.0)
        acc_ref[...] += _dot(a.astype(BF16), vbuf[slot].astype(BF16))
        r_new = r + ct[:, PAGE_COLS:]
        r_ref[...] = r_new
        return p - 1, jnp.min(r_new)

    p_end, _ = lax.while_loop(cond, body, (jnp.int32(n_pages - 1), jnp.float32(0.0)))

    @pl.when(p_end >= 0)
    def _():
        for cp in copies(p_end, p_end % 2):
            cp.wait()

    o_ref[...] = (acc_ref[...] * gate_ref[...].astype(F32)).astype(o_ref.dtype)


def _sb_decode(page_table, q, gate, cache_k, cache_v, uo):
    nb = q.shape[0]
    any_spec = pl.BlockSpec(memory_space=pl.ANY)
    head_spec = pl.BlockSpec((None, N_Q_HEADS, HEAD_DIM), lambda b, pt: (b, 0, 0))
    return pl.pallas_call(
        _sb_decode_kernel,
        out_shape=jax.ShapeDtypeStruct((nb, N_Q_HEADS, HEAD_DIM), BF16),
        grid_spec=pltpu.PrefetchScalarGridSpec(
            num_scalar_prefetch=1,
            grid=(nb,),
            in_specs=[head_spec, head_spec, any_spec, any_spec,
                      pl.BlockSpec(uo.shape, lambda b, pt: (0, 0))],
            out_specs=head_spec,
            scratch_shapes=[pltpu.VMEM((2, PAGE_COLS, HEAD_DIM), F32),
                            pltpu.VMEM((2, PAGE_COLS, HEAD_DIM), F32),
                            pltpu.SemaphoreType.DMA((2, 2)),
                            pltpu.VMEM((N_Q_HEADS, LANES), F32),
                            pltpu.VMEM((N_Q_HEADS, HEAD_DIM), F32)]),
        compiler_params=_params("arbitrary"),
        name="sb_decode",
    )(page_table, q, gate, cache_k, cache_v, uo)


def _fox_decode_kernel(pt_ref, q_ref, kn_ref, vn_ref, lfn_ref, gate_ref, sb_ref, ck_hbm, cv_hbm, lf_hbm,
                       us_ref, o_ref, kbuf, vbuf, lbuf, sem, m_ref, acc_ref, d_ref):
    n_seq, n_pages = pt_ref.shape
    n_grp = n_pages // DEC_G
    total = n_seq * n_grp
    us = us_ref[...]
    bias = jnp.where(_own_head((N_Q_HEADS, PAGE_COLS)), 0.0, NEG_BIG)

    def copies(t, slot):
        b = t // n_grp
        newest = n_pages - 1 - (t % n_grp) * DEC_G
        out = []
        for g in range(DEC_G):
            pid = pt_ref[b, newest - g]
            out += [pltpu.make_async_copy(ck_hbm.at[pid], kbuf.at[slot, g], sem.at[0, slot]),
                    pltpu.make_async_copy(cv_hbm.at[pid], vbuf.at[slot, g], sem.at[1, slot]),
                    pltpu.make_async_copy(lf_hbm.at[pid], lbuf.at[slot, g], sem.at[2, slot])]
        return out

    for cp in copies(0, 0):
        cp.start()

    def body(t, carry):
        slot = t % 2
        b = t // n_grp
        gi = t % n_grp

        @pl.when(t + 1 < total)
        def _():
            for cp in copies(t + 1, 1 - slot):
                cp.start()

        q = q_ref[b]

        @pl.when(gi == 0)
        def _():
            z_self = jnp.sum(q.astype(F32) * kn_ref[b].astype(F32), axis=1, keepdims=True)
            m_ref[...] = jnp.broadcast_to(z_self, m_ref.shape)
            acc_ref[...] = jnp.concatenate([vn_ref[b].astype(F32), jnp.ones((N_Q_HEADS, LANES), F32)], axis=1)
            d_ref[...] = jnp.broadcast_to(lfn_ref[b] * LOG2E, d_ref.shape)

        for cp in copies(t, slot):
            cp.wait()

        dt = _split_dot(_split3(lbuf[slot].reshape(DEC_G * N_Q_HEADS, PAGE_SIZE)), us) * LOG2E
        d = d_ref[...]
        zs = []
        for g in range(DEC_G):
            dg = dt[g * N_Q_HEADS:(g + 1) * N_Q_HEADS]
            z = lax.dot_general(q, kbuf[slot, g].astype(BF16), _NT, preferred_element_type=F32)
            zs.append(z + dg[:, :PAGE_COLS] + jnp.tile(d, (1, N_KV_HEADS)) + bias)
            d = d + dg[:, PAGE_COLS:]
        d_ref[...] = d
        z = jnp.concatenate(zs, axis=1)
        m_prev = m_ref[...]
        m_new = jnp.maximum(m_prev, jnp.max(z, axis=1, keepdims=True))
        alpha = jnp.exp2(m_prev - m_new)
        p = jnp.exp2(z - jnp.tile(m_new, (1, z.shape[1] // LANES)))
        lsum = jnp.sum(p, axis=1, keepdims=True)
        pb = p.astype(BF16)
        pv = _dot(pb[:, :PAGE_COLS], vbuf[slot, 0].astype(BF16))
        for g in range(1, DEC_G):
            pv += _dot(pb[:, g * PAGE_COLS:(g + 1) * PAGE_COLS], vbuf[slot, g].astype(BF16))
        acc = acc_ref[...] * jnp.tile(alpha, (1, 2)) + jnp.concatenate(
            [pv, jnp.broadcast_to(lsum, pv.shape)], axis=1)
        acc_ref[...] = acc
        m_ref[...] = m_new

        @pl.when(gi == n_grp - 1)
        def _():
            merged = acc[:, :HEAD_DIM] / acc[:, HEAD_DIM:] * gate_ref[b].astype(F32) + sb_ref[b].astype(F32)
            o_ref[b] = merged.astype(o_ref.dtype)

        return carry

    lax.fori_loop(0, total, body, 0)


def _fox_decode(page_table, q, k_new, v_new, lf_new, gate, sb_part, cache_k, cache_v, cache_lf_t, us):
    nb = q.shape[0]
    assert page_table.shape[1] % DEC_G == 0
    any_spec = pl.BlockSpec(memory_space=pl.ANY)
    head_spec = pl.BlockSpec((nb, N_Q_HEADS, HEAD_DIM), lambda i, pt: (0, 0, 0))
    return pl.pallas_call(
        _fox_decode_kernel,
        out_shape=jax.ShapeDtypeStruct((nb, N_Q_HEADS, HEAD_DIM), BF16),
        grid_spec=pltpu.PrefetchScalarGridSpec(
            num_scalar_prefetch=1,
            grid=(1,),
            in_specs=[head_spec, head_spec, head_spec,
                      pl.BlockSpec((nb, N_Q_HEADS, 1), lambda i, pt: (0, 0, 0)),
                      head_spec, head_spec,
                      any_spec, any_spec, any_spec,
                      pl.BlockSpec(us.shape, lambda i, pt: (0, 0))],
            out_specs=head_spec,
            scratch_shapes=[pltpu.VMEM((2, DEC_G, PAGE_COLS, HEAD_DIM), F32),
                            pltpu.VMEM((2, DEC_G, PAGE_COLS, HEAD_DIM), F32),
                            pltpu.VMEM((2, DEC_G, N_Q_HEADS, PAGE_SIZE), F32),
                            pltpu.SemaphoreType.DMA((3, 2)),
                            pltpu.VMEM((N_Q_HEADS, LANES), F32),
                            pltpu.VMEM((N_Q_HEADS, 2 * LANES), F32),
                            pltpu.VMEM((N_Q_HEADS, LANES), F32)]),
        compiler_params=_params("arbitrary"),
        name="fox_decode",
    )(page_table, q, k_new, v_new, lf_new, gate, sb_part, cache_k, cache_v, cache_lf_t, us)


def _out_proj_kernel(m_ref, ma_ref, w_ref, x_ref, xa_ref, y_ref, ya_ref):
    w = w_ref[...]
    y_ref[...] = x_ref[...] + _dot(m_ref[...], w)
    ya_ref[...] = xa_ref[...] + _dot(ma_ref[...], w)


def _out_proj(merged, merged_aux, w, x, x_aux, tm):
    rows = x.shape[0]
    y, y_aux = pl.pallas_call(
        _out_proj_kernel,
        out_shape=[jax.ShapeDtypeStruct((rows, D_MODEL), F32),
                   jax.ShapeDtypeStruct((rows // tm, AUX_ROWS, D_MODEL), F32)],
        grid=(rows // tm, D_MODEL // TN),
        in_specs=[pl.BlockSpec((tm, W_Q), lambda i, j: (i, 0)),
                  pl.BlockSpec((AUX_ROWS, W_Q), lambda i, j: (0, 0)),
                  pl.BlockSpec((W_Q, TN), lambda i, j: (0, j)),
                  pl.BlockSpec((tm, TN), lambda i, j: (i, j)),
                  pl.BlockSpec((AUX_ROWS, TN), lambda i, j: (0, j))],
        out_specs=[pl.BlockSpec((tm, TN), lambda i, j: (i, j)),
                   pl.BlockSpec((None, AUX_ROWS, TN), lambda i, j: (i, 0, j))],
        compiler_params=_params("parallel", "arbitrary"),
        name="out_proj",
    )(merged, merged_aux, w, x, x_aux)
    return y, y_aux[0]


def _ffn_kernel(y_ref, ya_ref, g_ref, wu_ref, wd_ref, o_ref, oa_ref, h_ref, ha_ref):
    i, k = pl.program_id(0), pl.program_id(1)

    def start(src, h_dst, o_dst):
        y = src[...]
        h_dst[...] = _rms(y, g_ref[...]).astype(BF16)
        o_dst[...] = y

    def step(h_src, o_dst):
        u = jnp.square(jnp.maximum(_dot(h_src[...], wu_ref[...]), 0.0))
        o_dst[...] += _dot(u.astype(BF16), wd_ref[...])

    @pl.when(k == 0)
    def _():
        start(y_ref, h_ref, o_ref)

    @pl.when(jnp.logical_and(i == 0, k == 0))
    def _():
        start(ya_ref, ha_ref, oa_ref)

    step(h_ref, o_ref)

    @pl.when(i == 0)
    def _():
        step(ha_ref, oa_ref)


def _ffn(y, y_aux, g, w_up, w_down, tm):
    rows = y.shape[0]
    aux_spec = pl.BlockSpec((AUX_ROWS, D_MODEL), lambda i, k: (0, 0))
    return pl.pallas_call(
        _ffn_kernel,
        out_shape=[jax.ShapeDtypeStruct((rows, D_MODEL), F32), jax.ShapeDtypeStruct((AUX_ROWS, D_MODEL), F32)],
        grid=(rows // tm, D_FF // TF),
        in_specs=[pl.BlockSpec((tm, D_MODEL), lambda i, k: (i, 0)),
                  aux_spec,
                  pl.BlockSpec((1, D_MODEL), lambda i, k: (0, 0)),
                  pl.BlockSpec((D_MODEL, TF), lambda i, k: (0, k)),
                  pl.BlockSpec((TF, D_MODEL), lambda i, k: (k, 0))],
        out_specs=[pl.BlockSpec((tm, D_MODEL), lambda i, k: (i, 0)), aux_spec],
        scratch_shapes=[pltpu.VMEM((tm, D_MODEL), BF16), pltpu.VMEM((AUX_ROWS, D_MODEL), BF16)],
        compiler_params=pltpu.CompilerParams(dimension_semantics=("arbitrary", "arbitrary"),
                                             vmem_limit_bytes=FFN_VMEM_LIMIT),
        name="ffn",
    )(y, y_aux, g, w_up, w_down)


def _project_rows(x, x_aux, wts, batch, seq):
    norm = lambda rows, tm: _rms_forget(rows, wts["g_mix"], wts["w_fl_hi"], wts["w_fl_lo"], wts["b_f"], tm)
    h, lf = norm(x, TM_MAIN // 2)
    h_aux, lf_aux = norm(x_aux, AUX_ROWS)
    wt = wts["w_in_t"]
    sq = _proj(h, h_aux, wt, 0, W_Q, wts["g_q"], "scale", TM_PROJ, "proj_sq")
    fq = _proj(h, h_aux, wt, W_Q + 2 * W_KV, W_Q, wts["g_q"], "norm_scale", TM_PROJ, "proj_fq")
    gates = _proj(h, h_aux, wt, FORGET_OFFSET + N_Q_HEADS, 2 * D_MODEL, wts["g_q"], "sigmoid", TM_PROJ,
                  "proj_gates")
    kv = _kv_state_proj(h, h_aux, wt, wts["g_k"], TM_PROJ, batch, seq)
    return sq, fq, gates, kv, (lf, lf_aux)


def _finish_rows(x, x_aux, merged, merged_aux, wts):
    y1, y1_aux = _out_proj(merged, merged_aux, wts["w_out"], x, x_aux, TM_PROJ)
    return _ffn(y1, y1_aux, wts["g_ffn"], wts["w_up"], wts["w_down"], TM_MAIN)


def kernel(x_prompt, x_sample, cache_sb_k, cache_sb_v, cache_fox_k, cache_fox_v, cache_fox_logf,
           page_table, meta_tokens, g_mix, w_in, b_forget, g_q, g_k, w_out, g_ffn, w_up, w_down):
    batch, seq, _ = x_prompt.shape
    dec = x_sample.shape[0]
    assert w_in.shape[0] == 1 and x_sample.shape[1] == 1 and N_META + dec <= AUX_ROWS
    pool = cache_sb_k.shape[1]

    w_in_t = jnp.swapaxes(w_in[0], 0, 1)
    w_fl = jnp.pad(w_in_t[FORGET_OFFSET:FORGET_OFFSET + N_Q_HEADS], ((0, LANES - N_Q_HEADS), (0, 0)))
    w_fl_hi = w_fl.astype(BF16)
    wts = {
        "g_mix": g_mix, "g_q": g_q, "g_k": g_k, "g_ffn": g_ffn, "b_f": b_forget,
        "w_in_t": w_in_t,
        "w_fl_hi": w_fl_hi, "w_fl_lo": (w_fl - w_fl_hi.astype(F32)).astype(BF16),
        "w_out": w_out[0].astype(BF16), "w_up": w_up[0].astype(BF16), "w_down": w_down[0].astype(BF16),
    }

    x_main = x_prompt.reshape(batch * seq, D_MODEL)
    x_aux = jnp.concatenate([meta_tokens, x_sample.reshape(dec, D_MODEL),
                             jnp.zeros((AUX_ROWS - N_META - dec, D_MODEL), F32)], axis=0)

    (sq_m, sq_a), (fq_m, fq_a), (gates_m, gates_a), kv, (lf_m, lf_a) = _project_rows(
        x_main, x_aux, wts, batch, seq)
    (sk_st, sv_st, fk_st, fv_st), kvb_m, kv_a, kvb_a = kv
    four = lambda a: [a[:, t * W_KV:(t + 1) * W_KV] for t in range(4)]
    sk_a, sv_a, fk_a, fv_a = four(kv_a)
    skb_a, svb_a, fkb_a, fvb_a = four(kvb_a)

    lf_meta_t = jnp.pad(lf_a[:N_META].T, ((0, 0), (0, LANES - N_META)))
    lf_main_t = jnp.swapaxes(lf_m.reshape(batch, seq, N_Q_HEADS), 1, 2)
    ck_meta, ck_main = _cum_forget(lf_meta_t, lf_main_t)
    ck_meta = ck_meta.reshape(batch, N_KV_HEADS, GROUP, LANES)
    ck_main = ck_main.reshape(batch, N_KV_HEADS, GROUP, seq)

    pad_keys = lambda a: jnp.pad(a[:N_META], ((0, LANES - N_META), (0, 0)))
    mk_sb, mv_sb, mk_fx, mv_fx = pad_keys(skb_a), pad_keys(svb_a), pad_keys(fkb_a), pad_keys(fvb_a)

    j_idx = lax.broadcasted_iota(jnp.int32, (LANES, LANES), 0)
    s_idx = lax.broadcasted_iota(jnp.int32, (LANES, LANES), 1)
    ones = jnp.ones((LANES, LANES), BF16)
    uo = jnp.concatenate([(j_idx >= s_idx).astype(BF16), ones], axis=1)
    uo = jnp.concatenate([uo] * 2, axis=0)
    c_src = lax.broadcasted_iota(jnp.int32, (PAGE_COLS, PAGE_COLS), 0) // N_KV_HEADS
    c_dst = lax.broadcasted_iota(jnp.int32, (PAGE_COLS, PAGE_COLS), 1) // N_KV_HEADS
    uo_page = jnp.concatenate([(c_src >= c_dst).astype(BF16), jnp.ones((PAGE_COLS, LANES), BF16)], axis=1)
    uo_page = jnp.concatenate([uo_page] * 2, axis=0)
    j_key = lax.broadcasted_iota(jnp.int32, (PAGE_SIZE, PAGE_COLS), 0)
    c_key = lax.broadcasted_iota(jnp.int32, (PAGE_SIZE, PAGE_COLS), 1) // N_KV_HEADS
    us_page = jnp.concatenate([(j_key > c_key).astype(BF16), ones], axis=1)
    us_page = jnp.concatenate([us_page] * 3, axis=0)

    sb_m = _sb_prompt(sq_m, kvb_m, mk_sb, mv_sb, uo, gates_m, batch, seq)
    merged_m = _fox_prompt(fq_m, kvb_m, ck_main, mk_fx, mv_fx, ck_meta, gates_m, sb_m, batch, seq)
    merged_meta = _meta_attn(sq_a[:N_META], mk_sb, mv_sb, fq_a[:N_META], mk_fx, mv_fx,
                             ck_meta[0], uo, gates_a[:N_META])

    heads = lambda a: a[N_META:N_META + dec].reshape(dec, N_Q_HEADS, HEAD_DIM)
    kv_heads = lambda a: jnp.repeat(a[N_META:N_META + dec].reshape(dec, N_KV_HEADS, HEAD_DIM), GROUP, axis=1)
    pages = lambda c: c[0].reshape(pool, PAGE_COLS, HEAD_DIM)
    sb_dec = _sb_decode(page_table, heads(sq_a), heads(gates_a[:, :W_Q]),
                        pages(cache_sb_k), pages(cache_sb_v), uo_page)
    merged_dec = _fox_decode(page_table, heads(fq_a), kv_heads(fkb_a), kv_heads(fvb_a),
                             lf_a[N_META:N_META + dec].reshape(dec, N_Q_HEADS, 1),
                             heads(gates_a[:, W_Q:]), sb_dec,
                             pages(cache_fox_k), pages(cache_fox_v),
                             jnp.swapaxes(cache_fox_logf[0], 1, 2), us_page)

    tail = jnp.zeros((AUX_ROWS - N_META - dec, W_Q), BF16)
    merged_a = jnp.concatenate([merged_meta, merged_dec.reshape(dec, W_Q), tail], axis=0)

    y_main, y_aux = _finish_rows(x_main, x_aux, merged_m, merged_a, wts)

    def prompt_state(a_aux, a_main, tail_shape):
        meta = jnp.broadcast_to(a_aux[None, :N_META], (batch, N_META, a_aux.shape[1]))
        full = jnp.concatenate([meta, a_main.reshape(batch, seq, a_aux.shape[1])], axis=1)
        return full.reshape((1, batch, seq + N_META) + tail_shape)

    kv_shape = (N_KV_HEADS, HEAD_DIM)
    kv_state = lambda st: st.reshape((1, batch, seq + N_META) + kv_shape)
    sample_state = lambda a, tail_shape: a[N_META:N_META + dec].reshape((1, dec, 1) + tail_shape)
    return (y_main.reshape(batch, seq, D_MODEL), y_aux[N_META:N_META + dec].reshape(dec, 1, D_MODEL),
            kv_state(sk_st), kv_state(sv_st), kv_state(fk_st), kv_state(fv_st),
            prompt_state(lf_a, lf_m, (N_Q_HEADS,)),
            sample_state(sk_a, kv_shape), sample_state(sv_a, kv_shape),
            sample_state(fk_a, kv_shape), sample_state(fv_a, kv_shape),
            sample_state(lf_a, (N_Q_HEADS,)))
```

```python
import functools

import jax
import jax.numpy as jnp
from jax import lax
from jax.experimental import pallas as pl
from jax.experimental.pallas import tpu as pltpu

D_MODEL = 2048
HEAD_DIM = 128
N_Q_HEADS = 16
N_KV_HEADS = 4
GROUP = 4
W_Q = N_Q_HEADS * HEAD_DIM
W_KV = N_KV_HEADS * HEAD_DIM
D_FF = 4 * D_MODEL
N_META = 16
PAGE_SIZE = 128
EPS = 1e-6
LOG2E = 1.4426950408889634
Q_SCALE = HEAD_DIM ** -0.5 * LOG2E
FORGET_OFFSET = 2 * W_Q + 4 * W_KV
AUX_ROWS = 32

BF16 = jnp.bfloat16
F32 = jnp.float32

LANES = 128
SUBLANES = 8
NEG_BIG = -1e30
SB_DEAD = 104.0 * LOG2E
VMEM_LIMIT = 48 * 1024 * 1024
FFN_VMEM_LIMIT = 58 * 1024 * 1024
TM_MAIN = 1024
TM_PROJ = 2048
TN = 512
TF = 1024
PROJ_CHUNK = 512

_NT = (((1,), (1,)), ((), ()))


def _params(*sem):
    return pltpu.CompilerParams(dimension_semantics=sem, vmem_limit_bytes=VMEM_LIMIT)


def _rms(x, g):
    ms = jnp.mean(x * x, axis=-1, keepdims=True)
    return x * lax.rsqrt(ms + EPS) * g


def _softplus(z):
    return jnp.maximum(z, 0.0) + jnp.log1p(jnp.exp(-jnp.abs(z)))


def _softplus2(z2):
    return jnp.maximum(z2, 0.0) + jnp.log2(1.0 + jnp.exp2(-jnp.abs(z2)))


def _split2(x):
    hi = x.astype(BF16)
    lo = (x - hi.astype(F32)).astype(BF16)
    return hi, lo


def _split3(x):
    hi = x.astype(BF16)
    r = x - hi.astype(F32)
    mid = r.astype(BF16)
    lo = (r - mid.astype(F32)).astype(BF16)
    return hi, mid, lo


def _dot(a, b):
    return jnp.dot(a, b, preferred_element_type=F32)


def _dot_t(a, bt):
    return lax.dot_general(a, bt, _NT, preferred_element_type=F32)


def _split_dot(parts, m_stacked):
    return _dot(jnp.concatenate(parts, axis=1), m_stacked)


def _rms_forget_kernel(x_ref, g_ref, whi_ref, wlo_ref, b_ref, h_ref, lf_ref):
    h = _rms(x_ref[...], g_ref[...])
    h_ref[...] = h.astype(h_ref.dtype)
    hh, hl = _split2(h)
    z = _dot_t(hh, whi_ref[...]) + _dot_t(hl, whi_ref[...]) + _dot_t(hh, wlo_ref[...])
    z = z[:, :N_Q_HEADS] + b_ref[...]
    lf_ref[...] = -_softplus(-z)


def _rms_forget(x, g, w_hi, w_lo, b, tm):
    rows = x.shape[0]
    const = lambda shape: pl.BlockSpec(shape, lambda i: (0, 0))
    return pl.pallas_call(
        _rms_forget_kernel,
        out_shape=[jax.ShapeDtypeStruct((rows, D_MODEL), BF16),
                   jax.ShapeDtypeStruct((rows, N_Q_HEADS), F32)],
        grid=(rows // tm,),
        in_specs=[pl.BlockSpec((tm, D_MODEL), lambda i: (i, 0)),
                  const((1, D_MODEL)), const((LANES, D_MODEL)), const((LANES, D_MODEL)),
                  const((1, N_Q_HEADS))],
        out_specs=[pl.BlockSpec((tm, D_MODEL), lambda i: (i, 0)),
                   pl.BlockSpec((tm, N_Q_HEADS), lambda i: (i, 0))],
        compiler_params=_params("parallel"),
        name="rms_forget",
    )(x, g, w_hi, w_lo, b)


def _head_norm(acc, g):
    outs = []
    for hh in range(acc.shape[1] // HEAD_DIM):
        xh = acc[:, hh * HEAD_DIM:(hh + 1) * HEAD_DIM]
        outs.append(_rms(xh, g))
    return jnp.concatenate(outs, axis=1)


def _proj_kernel(h_ref, ha_ref, wt_ref, g_ref, o_ref, oa_ref, *, mode):
    wt = wt_ref[...].astype(BF16)
    tm = h_ref.shape[0]
    chunk = min(tm, PROJ_CHUNK)
    pieces = [(h_ref, o_ref, slice(c * chunk, (c + 1) * chunk)) for c in range(tm // chunk)]
    pieces.append((ha_ref, oa_ref, slice(None)))
    acc_next = _dot_t(h_ref[pieces[0][2]], wt)
    for n, (_, dst, rows) in enumerate(pieces):
        acc = acc_next
        if n + 1 < len(pieces):
            src, _, nxt = pieces[n + 1]
            acc_next = _dot_t(src[nxt], wt)
        if mode == "scale":
            acc = acc * Q_SCALE
        elif mode == "norm_scale":
            acc = _head_norm(acc, g_ref[...]) * Q_SCALE
        elif mode == "sigmoid":
            acc = jax.nn.sigmoid(acc)
        dst[rows] = acc.astype(dst.dtype)


def _wt_rows_spec(row0):
    if row0 % TN == 0:
        return pl.BlockSpec((TN, D_MODEL), lambda i, j: (row0 // TN + j, 0))
    assert row0 % SUBLANES == 0
    return pl.BlockSpec((pl.Element(TN), pl.Element(D_MODEL)),
                        lambda i, j: (pl.multiple_of(row0 + j * TN, SUBLANES), 0))


def _proj(h, h_aux, wt, row0, n, g, mode, tm, name):
    rows = h.shape[0]
    out, out_aux = pl.pallas_call(
        functools.partial(_proj_kernel, mode=mode),
        out_shape=[jax.ShapeDtypeStruct((rows, n), BF16),
                   jax.ShapeDtypeStruct((rows // tm, AUX_ROWS, n), BF16)],
        grid=(rows // tm, n // TN),
        in_specs=[pl.BlockSpec((tm, D_MODEL), lambda i, j: (i, 0)),
                  pl.BlockSpec((AUX_ROWS, D_MODEL), lambda i, j: (0, 0)),
                  _wt_rows_spec(row0),
                  pl.BlockSpec((1, HEAD_DIM), lambda i, j: (0, 0))],
        out_specs=[pl.BlockSpec((tm, TN), lambda i, j: (i, j)),
                   pl.BlockSpec((None, AUX_ROWS, TN), lambda i, j: (i, 0, j))],
        compiler_params=_params("parallel", "arbitrary"),
        name=name,
    )(h, h_aux, wt, g)
    return out, out_aux[0]


def _kv_wt_spec():
    sb_kv, fox_kv = W_Q // W_KV, (2 * W_Q + 2 * W_KV) // W_KV
    return pl.BlockSpec((W_KV, D_MODEL), lambda i, j: (sb_kv + j + (fox_kv - sb_kv - 2) * (j // 2), 0))


def _kv_state_kernel(h_ref, ha_ref, wt_ref, gk_ref, *refs, tiles_per_batch):
    state = refs[:4]
    bf_ref, aux_f32_ref, aux_bf_ref, stage, meta_stage, sem, meta_sem = refs[4:]
    i, j = pl.program_id(0), pl.program_id(1)
    n_i = pl.num_programs(0)
    tm = h_ref.shape[0]
    b, it = i // tiles_per_batch, i % tiles_per_batch
    row0 = pl.multiple_of(N_META * N_KV_HEADS + it * tm * N_KV_HEADS, SUBLANES)
    slot = j % 2
    wt = wt_ref[...].astype(BF16)
    finish = lambda a: jnp.where(j == 2, _head_norm(a, gk_ref[...]), a)
    val = finish(_dot_t(h_ref[...], wt))
    val_aux = finish(_dot_t(ha_ref[...], wt))
    bf_ref[...] = val.astype(BF16)
    aux_f32_ref[...] = val_aux
    aux_bf_ref[...] = val_aux.astype(BF16)

    def tile_copy(jj):
        return pltpu.make_async_copy(stage.at[slot], state[jj].at[b, pl.ds(row0, tm * N_KV_HEADS), :],
                                     sem.at[slot])

    def meta_copy(jj):
        return pltpu.make_async_copy(meta_stage.at[jj], state[jj].at[b, pl.ds(0, N_META * N_KV_HEADS), :],
                                     meta_sem.at[jj])

    def per_tensor(fn):
        for jj in range(4):
            pl.when(j == jj)(functools.partial(fn, jj))

    def interleave(dst, val, rows):
        for n in range(N_KV_HEADS):
            dst[pl.ds(n, rows, stride=N_KV_HEADS), :] = val[:rows, n * HEAD_DIM:(n + 1) * HEAD_DIM]

    @pl.when(jnp.logical_or(i > 0, j >= 2))
    def _():
        per_tensor(lambda jj: tile_copy((jj + 2) % 4).wait())

    interleave(stage.at[slot], val, tm)
    per_tensor(lambda jj: tile_copy(jj).start())

    @pl.when(it == 0)
    def _():
        @pl.when(b > 0)
        def _():
            per_tensor(lambda jj: meta_copy(jj).wait())

        interleave(meta_stage.at[j], val_aux, N_META)
        per_tensor(lambda jj: meta_copy(jj).start())

    @pl.when(jnp.logical_and(i == n_i - 1, j == 3))
    def _():
        pltpu.make_async_copy(stage.at[0], state[2].at[b, pl.ds(row0, tm * N_KV_HEADS), :], sem.at[0]).wait()
        tile_copy(3).wait()
        for jj in range(4):
            meta_copy(jj).wait()


def _kv_state_proj(h, h_aux, wt, g_k, tm, batch, seq):
    rows = h.shape[0]
    tiles_per_batch = seq // tm
    state_rows = (N_META + seq) * N_KV_HEADS
    any_spec = pl.BlockSpec(memory_space=pl.ANY)
    aux_spec = pl.BlockSpec((None, AUX_ROWS, W_KV), lambda i, j: (i, 0, j))
    outs = pl.pallas_call(
        functools.partial(_kv_state_kernel, tiles_per_batch=tiles_per_batch),
        out_shape=[jax.ShapeDtypeStruct((batch, state_rows, HEAD_DIM), F32)] * 4
        + [jax.ShapeDtypeStruct((rows, 4 * W_KV), BF16),
           jax.ShapeDtypeStruct((rows // tm, AUX_ROWS, 4 * W_KV), F32),
           jax.ShapeDtypeStruct((rows // tm, AUX_ROWS, 4 * W_KV), BF16)],
        grid=(rows // tm, 4),
        in_specs=[pl.BlockSpec((tm, D_MODEL), lambda i, j: (i, 0)),
                  pl.BlockSpec((AUX_ROWS, D_MODEL), lambda i, j: (0, 0)),
                  _kv_wt_spec(),
                  pl.BlockSpec((1, HEAD_DIM), lambda i, j: (0, 0))],
        out_specs=[any_spec] * 4 + [pl.BlockSpec((tm, W_KV), lambda i, j: (i, j)), aux_spec, aux_spec],
        scratch_shapes=[pltpu.VMEM((2, tm * N_KV_HEADS, HEAD_DIM), F32),
                        pltpu.VMEM((4, N_META * N_KV_HEADS, HEAD_DIM), F32),
                        pltpu.SemaphoreType.DMA((2,)),
                        pltpu.SemaphoreType.DMA((4,))],
        compiler_params=_params("arbitrary", "arbitrary"),
        name="kv_state_proj",
    )(h, h_aux, wt, g_k)
    return outs[:4], outs[4], outs[5][0], outs[6][0]


def _lane_cumsum(x):
    n = x.shape[-1]
    lane = lax.broadcasted_iota(jnp.int32, x.shape, x.ndim - 1)
    k = 1
    while k < n:
        x = x + jnp.where(lane >= k, pltpu.roll(x, k, axis=x.ndim - 1), 0.0)
        k *= 2
    return x


def _cum_kernel(meta_ref, main_ref, cmeta_ref, cmain_ref):
    cm = _lane_cumsum(meta_ref[...])
    cmeta_ref[...] = cm * LOG2E
    cmain_ref[...] = (_lane_cumsum(main_ref[...]) + cm[:, LANES - 1:LANES]) * LOG2E


def _cum_forget(lf_meta_t, lf_main_t):
    b, _, s = lf_main_t.shape
    return pl.pallas_call(
        _cum_kernel,
        out_shape=[jax.ShapeDtypeStruct((b, N_Q_HEADS, LANES), F32),
                   jax.ShapeDtypeStruct((b, N_Q_HEADS, s), F32)],
        grid=(b,),
        in_specs=[pl.BlockSpec((N_Q_HEADS, LANES), lambda i: (0, 0)),
                  pl.BlockSpec((None, N_Q_HEADS, s), lambda i: (i, 0, 0))],
        out_specs=[pl.BlockSpec((None, N_Q_HEADS, LANES), lambda i: (i, 0, 0)),
                   pl.BlockSpec((None, N_Q_HEADS, s), lambda i: (i, 0, 0))],
        compiler_params=_params("parallel"),
        name="cum_forget",
    )(lf_meta_t, lf_main_t)


def _stack_heads(q):
    return jnp.concatenate([q[:, g * HEAD_DIM:(g + 1) * HEAD_DIM] for g in range(GROUP)], axis=0)


SB_CHUNK = 512
FOX_CHUNK = 512


def _chunk_visible(rel, c, cr, t, tk):
    col = lax.broadcasted_iota(jnp.int32, (cr, tk), 1)
    row = (lax.broadcasted_iota(jnp.int32, (cr, tk), 0) + c * cr) % t
    return rel(col, row)


def _sb_block(qs, kb, vb, uo, rel, r_ref, acc_ref):
    r, tk = qs.shape[0], kb.shape[0]
    t = r // GROUP
    cr = min(r, SB_CHUNK)
    score = lambda c: lax.dot_general(qs[c * cr:(c + 1) * cr], kb, _NT, preferred_element_type=F32)
    z_next = score(0)
    for c in range(r // cr):
        rows = slice(c * cr, (c + 1) * cr)
        z = z_next
        if c + 1 < r // cr:
            z_next = score(c + 1)
        sp = _softplus2(z)
        if rel is not None:
            vis = _chunk_visible(rel, c, cr, t, tk)
            sp = jnp.where(vis, sp, 0.0)
        ct = _split_dot(_split2(sp), uo)
        r_prev = r_ref[rows]
        a = jnp.exp2(z - ct[:, :tk] - r_prev)
        if rel is not None:
            a = jnp.where(vis, a, 0.0)
        acc_ref[rows] += _dot(a.astype(BF16), vb)
        r_ref[rows] = r_prev + ct[:, tk:]


def _sb_window(qs, kb, vb, uo, r_ref, acc_ref):
    r = qs.shape[0]
    t = r // GROUP
    nb = kb.shape[0] // t
    z = _dot_t(qs, kb)
    sp = _softplus2(z)
    vis = _chunk_visible(lambda c, row: c < row, 0, r, t, t)
    newer = None
    weights = [None] * nb
    for b in reversed(range(nb)):
        cols = slice(b * t, (b + 1) * t)
        diag = b == nb - 1
        spb = jnp.where(vis, sp[:, cols], 0.0) if diag else sp[:, cols]
        ct = _split_dot(_split2(spb), uo)
        arg = z[:, cols] - ct[:, :t]
        a = jnp.exp2(arg if newer is None else arg - newer)
        weights[b] = (jnp.where(vis, a, 0.0) if diag else a).astype(BF16)
        newer = ct[:, t:] if newer is None else newer + ct[:, t:]
    acc_ref[...] = _dot(jnp.concatenate(weights, axis=1), vb)
    r_ref[...] = newer


def _fox_block(qs, kb, vb, ckb, rel, m_ref, acc_ref):
    r, tk = qs.shape[0], kb.shape[0]
    t = r // GROUP
    cr = min(r, FOX_CHUNK)
    hc = max(cr // t, 1)
    v1 = jnp.concatenate([vb, jnp.ones((tk, LANES), BF16)], axis=1)
    score = lambda c: lax.dot_general(qs[c * cr:(c + 1) * cr], kb, _NT, preferred_element_type=F32)
    z_next = score(0)
    for c in range(r // cr):
        rows = slice(c * cr, (c + 1) * cr)
        g = c * cr // t
        bias = ckb[g:g + hc]
        z = (z_next.reshape(hc, cr // hc, tk) - bias[:, None, :]).reshape(cr, tk)
        if c + 1 < r // cr:
            z_next = score(c + 1)
        if rel is not None:
            z = jnp.where(_chunk_visible(rel, c, cr, t, tk), z, NEG_BIG)
        m_prev = m_ref[rows]
        m_new = jnp.maximum(m_prev, jnp.max(z, axis=1, keepdims=True))
        alpha = jnp.exp2(m_prev - m_new)
        p = jnp.exp2(z - jnp.tile(m_new, (1, tk // LANES)))
        acc_ref[rows] = acc_ref[rows] * jnp.tile(alpha, (1, 2)) + _dot(p.astype(BF16), v1)
        m_ref[rows] = m_new


SB_T = 128
SB_WIN = 3
SB_SUB = 8


def _store_heads(o_ref, o, t, gate_ref, add_ref=None, row0=0):
    rows = slice(row0, row0 + t)
    for g in range(GROUP):
        cols = slice(g * HEAD_DIM, (g + 1) * HEAD_DIM)
        val = o[g * t:(g + 1) * t] * gate_ref[rows, cols].astype(F32)
        if add_ref is not None:
            val = val + add_ref[rows, cols].astype(F32)
        o_ref[rows, cols] = val.astype(o_ref.dtype)


def _sb_prompt_kernel(q_ref, k_ref, v_ref, mk_ref, mv_ref, uo_ref, gate_ref, o_ref, r_ref, acc_ref):
    i = pl.program_id(2)
    t = SB_T
    uo = uo_ref[...]
    qs = [_stack_heads(q_ref[sub * t:(sub + 1) * t]) for sub in range(SB_SUB)]
    tile = lambda sub: i * SB_SUB + sub

    def first_block_only(sub):
        r_ref[sub] = jnp.zeros(r_ref.shape[1:], F32)
        acc_ref[sub] = jnp.zeros(acc_ref.shape[1:], F32)
        s = pl.multiple_of(tile(sub) * t, t)
        _sb_block(qs[sub], k_ref[pl.ds(s, t), :], v_ref[pl.ds(s, t), :], uo, lambda c, r: c < r,
                  r_ref.at[sub], acc_ref.at[sub])

    assert SB_SUB >= SB_WIN - 1

    @pl.when(i > 0)
    def _():
        for sub in range(SB_SUB):
            s = pl.multiple_of((tile(sub) - (SB_WIN - 1)) * t, t)
            _sb_window(qs[sub], k_ref[pl.ds(s, SB_WIN * t), :], v_ref[pl.ds(s, SB_WIN * t), :], uo,
                       r_ref.at[sub], acc_ref.at[sub])

    @pl.when(i == 0)
    def _():
        for sub in range(SB_SUB):
            if sub >= SB_WIN - 1:
                _sb_window(qs[sub], k_ref[pl.ds((sub - (SB_WIN - 1)) * t, SB_WIN * t), :],
                           v_ref[pl.ds((sub - (SB_WIN - 1)) * t, SB_WIN * t), :], uo,
                           r_ref.at[sub], acc_ref.at[sub])
            else:
                first_block_only(sub)

    for sub in range(SB_SUB):
        def visit(kb, vb, rel, sub=sub):
            _sb_block(qs[sub], kb, vb, uo, rel, r_ref.at[sub], acc_ref.at[sub])
            return jnp.min(r_ref[sub])

        def cond(c):
            return jnp.logical_and(c[0] >= 0, c[1] < SB_DEAD)

        def body(c, visit=visit):
            s = pl.multiple_of(c[0] * t, t)
            return c[0] - 1, visit(k_ref[pl.ds(s, t), :], v_ref[pl.ds(s, t), :], None)

        covered = jnp.where(jnp.logical_or(i > 0, sub >= SB_WIN - 1), SB_WIN, 1)
        _, mn = lax.while_loop(cond, body, (tile(sub) - covered, jnp.min(r_ref[sub])))

        @pl.when(mn < SB_DEAD)
        def _(visit=visit):
            visit(mk_ref[...], mv_ref[...], lambda c, r: c < N_META)

        _store_heads(o_ref, acc_ref[sub], t, gate_ref, row0=sub * t)


def _kv_head_spec(seq, tensor):
    return pl.BlockSpec((seq, HEAD_DIM), lambda b, n, i: (b, tensor * N_KV_HEADS + n))


def _sb_prompt(q, kv, mk, mv, uo, gates, batch, seq):
    t = SB_T * SB_SUB
    nb = seq // t
    tile = pl.BlockSpec((t, GROUP * HEAD_DIM), lambda b, n, i: (b * nb + i, n))
    return pl.pallas_call(
        _sb_prompt_kernel,
        out_shape=jax.ShapeDtypeStruct((batch * seq, W_Q), BF16),
        grid=(batch, N_KV_HEADS, nb),
        in_specs=[tile,
                  _kv_head_spec(seq, 0),
                  _kv_head_spec(seq, 1),
                  pl.BlockSpec((LANES, HEAD_DIM), lambda b, n, i: (0, n)),
                  pl.BlockSpec((LANES, HEAD_DIM), lambda b, n, i: (0, n)),
                  pl.BlockSpec(uo.shape, lambda b, n, i: (0, 0)),
                  tile],
        out_specs=tile,
        scratch_shapes=[pltpu.VMEM((SB_SUB, GROUP * SB_T, SB_T), F32),
                        pltpu.VMEM((SB_SUB, GROUP * SB_T, HEAD_DIM), F32)],
        compiler_params=_params("parallel", "parallel", "arbitrary"),
        name="sb_prompt",
    )(q, kv, kv, mk, mv, uo, gates)


FOX_T = 512
FOX_TK = 1024


def _fox_prompt_kernel(q_ref, k_ref, v_ref, ck_ref, mk_ref, mv_ref, mck_ref, gate_ref, sb_ref,
                       o_ref, m_ref, acc_ref):
    i = pl.program_id(2)
    t = FOX_T
    qs = _stack_heads(q_ref[...])
    m_ref[...] = jnp.full_like(m_ref, NEG_BIG)
    acc_ref[...] = jnp.zeros_like(acc_ref)

    def visit(s, width, rel):
        _fox_block(qs, k_ref[pl.ds(s, width), :], v_ref[pl.ds(s, width), :], ck_ref[:, pl.ds(s, width)],
                   rel, m_ref, acc_ref)

    visit(pl.multiple_of(i * t, t), t, lambda c, r: c <= r)

    per_visit = FOX_TK // t

    def body(j, carry):
        visit(pl.multiple_of(j * FOX_TK, FOX_TK), FOX_TK, None)
        return carry

    lax.fori_loop(0, i // per_visit, body, 0)

    for left in range(per_visit - 1):
        @pl.when(i % per_visit > left)
        def _(left=left):
            visit(pl.multiple_of((i - i % per_visit + left) * t, t), t, None)

    _fox_block(qs, mk_ref[...], mv_ref[...], mck_ref[...],
               lambda c, r: c < N_META, m_ref, acc_ref)

    acc = acc_ref[...]
    _store_heads(o_ref, acc[:, :HEAD_DIM] / acc[:, HEAD_DIM:], t, gate_ref, sb_ref)


def _fox_prompt(q, kv, ck, mk, mv, mck, gates, sb_part, batch, seq):
    t = FOX_T
    nb = seq // t
    tile = pl.BlockSpec((t, GROUP * HEAD_DIM), lambda b, n, i: (b * nb + i, n))
    return pl.pallas_call(
        _fox_prompt_kernel,
        out_shape=jax.ShapeDtypeStruct((batch * seq, W_Q), BF16),
        grid=(batch, N_KV_HEADS, nb),
        in_specs=[tile,
                  _kv_head_spec(seq, 2),
                  _kv_head_spec(seq, 3),
                  pl.BlockSpec((None, None, GROUP, seq), lambda b, n, i: (b, n, 0, 0)),
                  pl.BlockSpec((LANES, HEAD_DIM), lambda b, n, i: (0, n)),
                  pl.BlockSpec((LANES, HEAD_DIM), lambda b, n, i: (0, n)),
                  pl.BlockSpec((None, None, GROUP, LANES), lambda b, n, i: (b, n, 0, 0)),
                  pl.BlockSpec((t, GROUP * HEAD_DIM), lambda b, n, i: (b * nb + i, N_KV_HEADS + n)),
                  tile],
        out_specs=tile,
        scratch_shapes=[pltpu.VMEM((GROUP * t, LANES), F32), pltpu.VMEM((GROUP * t, 2 * LANES), F32)],
        compiler_params=_params("parallel", "parallel", "arbitrary"),
        name="fox_prompt",
    )(q, kv, kv, ck, mk, mv, mck, gates, sb_part)


def _meta_attn_kernel(sq_ref, sk_ref, sv_ref, fq_ref, fk_ref, fv_ref, ck_ref, uo_ref, gsb_ref, gfx_ref,
                      o_ref, m_ref, acc_ref, r_ref, sacc_ref):
    t = N_META
    valid = lambda c: c < N_META
    r_ref[...] = jnp.zeros_like(r_ref)
    sacc_ref[...] = jnp.zeros_like(sacc_ref)
    _sb_block(_stack_heads(sq_ref[...]), sk_ref[...], sv_ref[...], uo_ref[...],
              lambda c, r: jnp.logical_and(c < r, valid(c)), r_ref, sacc_ref)
    o = sacc_ref[...]
    m_ref[...] = jnp.full_like(m_ref, NEG_BIG)
    acc_ref[...] = jnp.zeros_like(acc_ref)
    _fox_block(_stack_heads(fq_ref[...]), fk_ref[...], fv_ref[...], ck_ref[...],
               lambda c, r: jnp.logical_and(c <= r, valid(c)), m_ref, acc_ref)
    acc = acc_ref[...]
    f = acc[:, :HEAD_DIM] / acc[:, HEAD_DIM:]
    for g in range(GROUP):
        cols = slice(g * HEAD_DIM, (g + 1) * HEAD_DIM)
        merged = (gsb_ref[:, cols].astype(F32) * o[g * t:(g + 1) * t]
                  + gfx_ref[:, cols].astype(F32) * f[g * t:(g + 1) * t])
        o_ref[:, cols] = merged.astype(o_ref.dtype)


def _meta_attn(sq, sk, sv, fq, fk, fv, ck, uo, gates):
    qspec = pl.BlockSpec((N_META, GROUP * HEAD_DIM), lambda n: (0, n))
    kspec = pl.BlockSpec((LANES, HEAD_DIM), lambda n: (0, n))
    return pl.pallas_call(
        _meta_attn_kernel,
        out_shape=jax.ShapeDtypeStruct((N_META, W_Q), BF16),
        grid=(N_KV_HEADS,),
        in_specs=[qspec, kspec, kspec, qspec, kspec, kspec,
                  pl.BlockSpec((None, GROUP, LANES), lambda n: (n, 0, 0)),
                  pl.BlockSpec(uo.shape, lambda n: (0, 0)),
                  qspec,
                  pl.BlockSpec((N_META, GROUP * HEAD_DIM), lambda n: (0, N_KV_HEADS + n))],
        out_specs=qspec,
        scratch_shapes=[pltpu.VMEM((GROUP * N_META, LANES), F32),
                        pltpu.VMEM((GROUP * N_META, 2 * LANES), F32),
                        pltpu.VMEM((GROUP * N_META, LANES), F32),
                        pltpu.VMEM((GROUP * N_META, HEAD_DIM), F32)],
        compiler_params=_params("parallel"),
        name="meta_attn",
    )(sq, sk, sv, fq, fk, fv, ck, uo, gates, gates)


PAGE_COLS = PAGE_SIZE * N_KV_HEADS
DEC_G = 16


def _own_head(shape):
    row = lax.broadcasted_iota(jnp.int32, shape, 0)
    col = lax.broadcasted_iota(jnp.int32, shape, 1)
    return (col % N_KV_HEADS) == (row // GROUP)


def _sb_decode_kernel(pt_ref, q_ref, gate_ref, ck_hbm, cv_hbm, uo_ref, o_ref, kbuf, vbuf, sem, r_ref, acc_ref):
    b = pl.program_id(0)
    n_pages = pt_ref.shape[1]
    q = q_ref[...]
    uo = uo_ref[...]
    own = _own_head((N_Q_HEADS, PAGE_COLS))

    def copies(p, slot):
        pid = pt_ref[b, p]
        return [pltpu.make_async_copy(ck_hbm.at[pid], kbuf.at[slot], sem.at[0, slot]),
                pltpu.make_async_copy(cv_hbm.at[pid], vbuf.at[slot], sem.at[1, slot])]

    r_ref[...] = jnp.zeros_like(r_ref)
    acc_ref[...] = jnp.zeros_like(acc_ref)
    for c in copies(n_pages - 1, (n_pages - 1) % 2):
        c.start()

    def cond(c):
        return jnp.logical_and(c[0] >= 0, c[1] < SB_DEAD)

    def body(c):
        p = c[0]
        slot = p % 2
        for cp in copies(p, slot):
            cp.wait()

        @pl.when(p > 0)
        def _():
            for cp in copies(p - 1, 1 - slot):
                cp.start()

        z = lax.dot_general(q, kbuf[slot].astype(BF16), _NT, preferred_element_type=F32)
        sp = jnp.where(own, _softplus2(z), 0.0)
        ct = _split_dot(_split2(sp), uo)
        r = r_ref[...]
        a = jnp.where(own, jnp.exp2(z - ct[:, :PAGE_COLS] - jnp.tile(r, (1, N_KV_HEADS))), 0.0)
        acc_ref[...] += _dot(a.astype(BF16), vbuf[slot].astype(BF16))
        r_new = r + ct[:, PAGE_COLS:]
        r_ref[...] = r_new
        return p - 1, jnp.min(r_new)

    p_end, _ = lax.while_loop(cond, body, (jnp.int32(n_pages - 1), jnp.float32(0.0)))

    @pl.when(p_end >= 0)
    def _():
        for cp in copies(p_end, p_end % 2):
            cp.wait()

    o_ref[...] = (acc_ref[...] * gate_ref[...].astype(F32)).astype(o_ref.dtype)


def _sb_decode(page_table, q, gate, cache_k, cache_v, uo):
    nb = q.shape[0]
    any_spec = pl.BlockSpec(memory_space=pl.ANY)
    head_spec = pl.BlockSpec((None, N_Q_HEADS, HEAD_DIM), lambda b, pt: (b, 0, 0))
    return pl.pallas_call(
        _sb_decode_kernel,
        out_shape=jax.ShapeDtypeStruct((nb, N_Q_HEADS, HEAD_DIM), BF16),
        grid_spec=pltpu.PrefetchScalarGridSpec(
            num_scalar_prefetch=1,
            grid=(nb,),
            in_specs=[head_spec, head_spec, any_spec, any_spec,
                      pl.BlockSpec(uo.shape, lambda b, pt: (0, 0))],
            out_specs=head_spec,
            scratch_shapes=[pltpu.VMEM((2, PAGE_COLS, HEAD_DIM), F32),
                            pltpu.VMEM((2, PAGE_COLS, HEAD_DIM), F32),
                            pltpu.SemaphoreType.DMA((2, 2)),
                            pltpu.VMEM((N_Q_HEADS, LANES), F32),
                            pltpu.VMEM((N_Q_HEADS, HEAD_DIM), F32)]),
        compiler_params=_params("arbitrary"),
        name="sb_decode",
    )(page_table, q, gate, cache_k, cache_v, uo)


def _fox_decode_kernel(pt_ref, q_ref, kn_ref, vn_ref, lfn_ref, gate_ref, sb_ref, ck_hbm, cv_hbm, lf_hbm,
                       us_ref, o_ref, kbuf, vbuf, lbuf, sem, m_ref, acc_ref, d_ref):
    n_seq, n_pages = pt_ref.shape
    n_grp = n_pages // DEC_G
    total = n_seq * n_grp
    us = us_ref[...]
    bias = jnp.where(_own_head((N_Q_HEADS, PAGE_COLS)), 0.0, NEG_BIG)

    def copies(t, slot):
        b = t // n_grp
        newest = n_pages - 1 - (t % n_grp) * DEC_G
        out = []
        for g in range(DEC_G):
            pid = pt_ref[b, newest - g]
            out += [pltpu.make_async_copy(ck_hbm.at[pid], kbuf.at[slot, g], sem.at[0, slot]),
                    pltpu.make_async_copy(cv_hbm.at[pid], vbuf.at[slot, g], sem.at[1, slot]),
                    pltpu.make_async_copy(lf_hbm.at[pid], lbuf.at[slot, g], sem.at[2, slot])]
        return out

    for cp in copies(0, 0):
        cp.start()

    def body(t, carry):
        slot = t % 2
        b = t // n_grp
        gi = t % n_grp

        @pl.when(t + 1 < total)
        def _():
            for cp in copies(t + 1, 1 - slot):
                cp.start()

        q = q_ref[b]

        @pl.when(gi == 0)
        def _():
            z_self = jnp.sum(q.astype(F32) * kn_ref[b].astype(F32), axis=1, keepdims=True)
            m_ref[...] = jnp.broadcast_to(z_self, m_ref.shape)
            acc_ref[...] = jnp.concatenate([vn_ref[b].astype(F32), jnp.ones((N_Q_HEADS, LANES), F32)], axis=1)
            d_ref[...] = jnp.broadcast_to(lfn_ref[b] * LOG2E, d_ref.shape)

        for cp in copies(t, slot):
            cp.wait()

        dt = _split_dot(_split3(lbuf[slot].reshape(DEC_G * N_Q_HEADS, PAGE_SIZE)), us) * LOG2E
        d = d_ref[...]
        zs = []
        for g in range(DEC_G):
            dg = dt[g * N_Q_HEADS:(g + 1) * N_Q_HEADS]
            z = lax.dot_general(q, kbuf[slot, g].astype(BF16), _NT, preferred_element_type=F32)
            zs.append(z + dg[:, :PAGE_COLS] + jnp.tile(d, (1, N_KV_HEADS)) + bias)
            d = d + dg[:, PAGE_COLS:]
        d_ref[...] = d
        z = jnp.concatenate(zs, axis=1)
        m_prev = m_ref[...]
        m_new = jnp.maximum(m_prev, jnp.max(z, axis=1, keepdims=True))
        alpha = jnp.exp2(m_prev - m_new)
        p = jnp.exp2(z - jnp.tile(m_new, (1, z.shape[1] // LANES)))
        lsum = jnp.sum(p, axis=1, keepdims=True)
        pb = p.astype(BF16)
        pv = _dot(pb[:, :PAGE_COLS], vbuf[slot, 0].astype(BF16))
        for g in range(1, DEC_G):
            pv += _dot(pb[:, g * PAGE_COLS:(g + 1) * PAGE_COLS], vbuf[slot, g].astype(BF16))
        acc = acc_ref[...] * jnp.tile(alpha, (1, 2)) + jnp.concatenate(
            [pv, jnp.broadcast_to(lsum, pv.shape)], axis=1)
        acc_ref[...] = acc
        m_ref[...] = m_new

        @pl.when(gi == n_grp - 1)
        def _():
            merged = acc[:, :HEAD_DIM] / acc[:, HEAD_DIM:] * gate_ref[b].astype(F32) + sb_ref[b].astype(F32)
            o_ref[b] = merged.astype(o_ref.dtype)

        return carry

    lax.fori_loop(0, total, body, 0)


def _fox_decode(page_table, q, k_new, v_new, lf_new, gate, sb_part, cache_k, cache_v, cache_lf_t, us):
    nb = q.shape[0]
    assert page_table.shape[1] % DEC_G == 0
    any_spec = pl.BlockSpec(memory_space=pl.ANY)
    head_spec = pl.BlockSpec((nb, N_Q_HEADS, HEAD_DIM), lambda i, pt: (0, 0, 0))
    return pl.pallas_call(
        _fox_decode_kernel,
        out_shape=jax.ShapeDtypeStruct((nb, N_Q_HEADS, HEAD_DIM), BF16),
        grid_spec=pltpu.PrefetchScalarGridSpec(
            num_scalar_prefetch=1,
            grid=(1,),
            in_specs=[head_spec, head_spec, head_spec,
                      pl.BlockSpec((nb, N_Q_HEADS, 1), lambda i, pt: (0, 0, 0)),
                      head_spec, head_spec,
                      any_spec, any_spec, any_spec,
                      pl.BlockSpec(us.shape, lambda i, pt: (0, 0))],
            out_specs=head_spec,
            scratch_shapes=[pltpu.VMEM((2, DEC_G, PAGE_COLS, HEAD_DIM), F32),
                            pltpu.VMEM((2, DEC_G, PAGE_COLS, HEAD_DIM), F32),
                            pltpu.VMEM((2, DEC_G, N_Q_HEADS, PAGE_SIZE), F32),
                            pltpu.SemaphoreType.DMA((3, 2)),
                            pltpu.VMEM((N_Q_HEADS, LANES), F32),
                            pltpu.VMEM((N_Q_HEADS, 2 * LANES), F32),
                            pltpu.VMEM((N_Q_HEADS, LANES), F32)]),
        compiler_params=_params("arbitrary"),
        name="fox_decode",
    )(page_table, q, k_new, v_new, lf_new, gate, sb_part, cache_k, cache_v, cache_lf_t, us)


def _out_proj_kernel(m_ref, ma_ref, w_ref, x_ref, xa_ref, y_ref, ya_ref):
    w = w_ref[...]
    y_ref[...] = x_ref[...] + _dot(m_ref[...], w)
    ya_ref[...] = xa_ref[...] + _dot(ma_ref[...], w)


def _out_proj(merged, merged_aux, w, x, x_aux, tm):
    rows = x.shape[0]
    y, y_aux = pl.pallas_call(
        _out_proj_kernel,
        out_shape=[jax.ShapeDtypeStruct((rows, D_MODEL), F32),
                   jax.ShapeDtypeStruct((rows // tm, AUX_ROWS, D_MODEL), F32)],
        grid=(rows // tm, D_MODEL // TN),
        in_specs=[pl.BlockSpec((tm, W_Q), lambda i, j: (i, 0)),
                  pl.BlockSpec((AUX_ROWS, W_Q), lambda i, j: (0, 0)),
                  pl.BlockSpec((W_Q, TN), lambda i, j: (0, j)),
                  pl.BlockSpec((tm, TN), lambda i, j: (i, j)),
                  pl.BlockSpec((AUX_ROWS, TN), lambda i, j: (0, j))],
        out_specs=[pl.BlockSpec((tm, TN), lambda i, j: (i, j)),
                   pl.BlockSpec((None, AUX_ROWS, TN), lambda i, j: (i, 0, j))],
        compiler_params=_params("parallel", "arbitrary"),
        name="out_proj",
    )(merged, merged_aux, w, x, x_aux)
    return y, y_aux[0]


def _ffn_kernel(y_ref, ya_ref, g_ref, wu_ref, wd_ref, o_ref, oa_ref, h_ref, ha_ref):
    i, k = pl.program_id(0), pl.program_id(1)

    def start(src, h_dst, o_dst):
        y = src[...]
        h_dst[...] = _rms(y, g_ref[...]).astype(BF16)
        o_dst[...] = y

    def step(h_src, o_dst):
        u = jnp.square(jnp.maximum(_dot(h_src[...], wu_ref[...]), 0.0))
        o_dst[...] += _dot(u.astype(BF16), wd_ref[...])

    @pl.when(k == 0)
    def _():
        start(y_ref, h_ref, o_ref)

    @pl.when(jnp.logical_and(i == 0, k == 0))
    def _():
        start(ya_ref, ha_ref, oa_ref)

    step(h_ref, o_ref)

    @pl.when(i == 0)
    def _():
        step(ha_ref, oa_ref)


def _ffn(y, y_aux, g, w_up, w_down, tm):
    rows = y.shape[0]
    aux_spec = pl.BlockSpec((AUX_ROWS, D_MODEL), lambda i, k: (0, 0))
    return pl.pallas_call(
        _ffn_kernel,
        out_shape=[jax.ShapeDtypeStruct((rows, D_MODEL), F32), jax.ShapeDtypeStruct((AUX_ROWS, D_MODEL), F32)],
        grid=(rows // tm, D_FF // TF),
        in_specs=[pl.BlockSpec((tm, D_MODEL), lambda i, k: (i, 0)),
                  aux_spec,
                  pl.BlockSpec((1, D_MODEL), lambda i, k: (0, 0)),
                  pl.BlockSpec((D_MODEL, TF), lambda i, k: (0, k)),
                  pl.BlockSpec((TF, D_MODEL), lambda i, k: (k, 0))],
        out_specs=[pl.BlockSpec((tm, D_MODEL), lambda i, k: (i, 0)), aux_spec],
        scratch_shapes=[pltpu.VMEM((tm, D_MODEL), BF16), pltpu.VMEM((AUX_ROWS, D_MODEL), BF16)],
        compiler_params=pltpu.CompilerParams(dimension_semantics=("arbitrary", "arbitrary"),
                                             vmem_limit_bytes=FFN_VMEM_LIMIT),
        name="ffn",
    )(y, y_aux, g, w_up, w_down)


def _project_rows(x, x_aux, wts, batch, seq):
    norm = lambda rows, tm: _rms_forget(rows, wts["g_mix"], wts["w_fl_hi"], wts["w_fl_lo"], wts["b_f"], tm)
    h, lf = norm(x, TM_MAIN // 2)
    h_aux, lf_aux = norm(x_aux, AUX_ROWS)
    wt = wts["w_in_t"]
    sq = _proj(h, h_aux, wt, 0, W_Q, wts["g_q"], "scale", TM_PROJ, "proj_sq")
    fq = _proj(h, h_aux, wt, W_Q + 2 * W_KV, W_Q, wts["g_q"], "norm_scale", TM_PROJ, "proj_fq")
    gates = _proj(h, h_aux, wt, FORGET_OFFSET + N_Q_HEADS, 2 * D_MODEL, wts["g_q"], "sigmoid", TM_PROJ,
                  "proj_gates")
    kv = _kv_state_proj(h, h_aux, wt, wts["g_k"], TM_PROJ, batch, seq)
    return sq, fq, gates, kv, (lf, lf_aux)


def _finish_rows(x, x_aux, merged, merged_aux, wts):
    y1, y1_aux = _out_proj(merged, merged_aux, wts["w_out"], x, x_aux, TM_PROJ)
    return _ffn(y1, y1_aux, wts["g_ffn"], wts["w_up"], wts["w_down"], TM_MAIN)


def kernel(x_prompt, x_sample, cache_sb_k, cache_sb_v, cache_fox_k, cache_fox_v, cache_fox_logf,
           page_table, meta_tokens, g_mix, w_in, b_forget, g_q, g_k, w_out, g_ffn, w_up, w_down):
    batch, seq, _ = x_prompt.shape
    dec = x_sample.shape[0]
    assert w_in.shape[0] == 1 and x_sample.shape[1] == 1 and N_META + dec <= AUX_ROWS
    pool = cache_sb_k.shape[1]

    w_in_t = jnp.swapaxes(w_in[0], 0, 1)
    w_fl = jnp.pad(w_in_t[FORGET_OFFSET:FORGET_OFFSET + N_Q_HEADS], ((0, LANES - N_Q_HEADS), (0, 0)))
    w_fl_hi = w_fl.astype(BF16)
    wts = {
        "g_mix": g_mix, "g_q": g_q, "g_k": g_k, "g_ffn": g_ffn, "b_f": b_forget,
        "w_in_t": w_in_t,
        "w_fl_hi": w_fl_hi, "w_fl_lo": (w_fl - w_fl_hi.astype(F32)).astype(BF16),
        "w_out": w_out[0].astype(BF16), "w_up": w_up[0].astype(BF16), "w_down": w_down[0].astype(BF16),
    }

    x_main = x_prompt.reshape(batch * seq, D_MODEL)
    x_aux = jnp.concatenate([meta_tokens, x_sample.reshape(dec, D_MODEL),
                             jnp.zeros((AUX_ROWS - N_META - dec, D_MODEL), F32)], axis=0)

    (sq_m, sq_a), (fq_m, fq_a), (gates_m, gates_a), kv, (lf_m, lf_a) = _project_rows(
        x_main, x_aux, wts, batch, seq)
    (sk_st, sv_st, fk_st, fv_st), kvb_m, kv_a, kvb_a = kv
    four = lambda a: [a[:, t * W_KV:(t + 1) * W_KV] for t in range(4)]
    sk_a, sv_a, fk_a, fv_a = four(kv_a)
    skb_a, svb_a, fkb_a, fvb_a = four(kvb_a)

    lf_meta_t = jnp.pad(lf_a[:N_META].T, ((0, 0), (0, LANES - N_META)))
    lf_main_t = jnp.swapaxes(lf_m.reshape(batch, seq, N_Q_HEADS), 1, 2)
    ck_meta, ck_main = _cum_forget(lf_meta_t, lf_main_t)
    ck_meta = ck_meta.reshape(batch, N_KV_HEADS, GROUP, LANES)
    ck_main = ck_main.reshape(batch, N_KV_HEADS, GROUP, seq)

    pad_keys = lambda a: jnp.pad(a[:N_META], ((0, LANES - N_META), (0, 0)))
    mk_sb, mv_sb, mk_fx, mv_fx = pad_keys(skb_a), pad_keys(svb_a), pad_keys(fkb_a), pad_keys(fvb_a)

    j_idx = lax.broadcasted_iota(jnp.int32, (LANES, LANES), 0)
    s_idx = lax.broadcasted_iota(jnp.int32, (LANES, LANES), 1)
    ones = jnp.ones((LANES, LANES), BF16)
    uo = jnp.concatenate([(j_idx >= s_idx).astype(BF16), ones], axis=1)
    uo = jnp.concatenate([uo] * 2, axis=0)
    c_src = lax.broadcasted_iota(jnp.int32, (PAGE_COLS, PAGE_COLS), 0) // N_KV_HEADS
    c_dst = lax.broadcasted_iota(jnp.int32, (PAGE_COLS, PAGE_COLS), 1) // N_KV_HEADS
    uo_page = jnp.concatenate([(c_src >= c_dst).astype(BF16), jnp.ones((PAGE_COLS, LANES), BF16)], axis=1)
    uo_page = jnp.concatenate([uo_page] * 2, axis=0)
    j_key = lax.broadcasted_iota(jnp.int32, (PAGE_SIZE, PAGE_COLS), 0)
    c_key = lax.broadcasted_iota(jnp.int32, (PAGE_SIZE, PAGE_COLS), 1) // N_KV_HEADS
    us_page = jnp.concatenate([(j_key > c_key).astype(BF16), ones], axis=1)
    us_page = jnp.concatenate([us_page] * 3, axis=0)

    sb_m = _sb_prompt(sq_m, kvb_m, mk_sb, mv_sb, uo, gates_m, batch, seq)
    merged_m = _fox_prompt(fq_m, kvb_m, ck_main, mk_fx, mv_fx, ck_meta, gates_m, sb_m, batch, seq)
    merged_meta = _meta_attn(sq_a[:N_META], mk_sb, mv_sb, fq_a[:N_META], mk_fx, mv_fx,
                             ck_meta[0], uo, gates_a[:N_META])

    heads = lambda a: a[N_META:N_META + dec].reshape(dec, N_Q_HEADS, HEAD_DIM)
    kv_heads = lambda a: jnp.repeat(a[N_META:N_META + dec].reshape(dec, N_KV_HEADS, HEAD_DIM), GROUP, axis=1)
    pages = lambda c: c[0].reshape(pool, PAGE_COLS, HEAD_DIM)
    sb_dec = _sb_decode(page_table, heads(sq_a), heads(gates_a[:, :W_Q]),
                        pages(cache_sb_k), pages(cache_sb_v), uo_page)
    merged_dec = _fox_decode(page_table, heads(fq_a), kv_heads(fkb_a), kv_heads(fvb_a),
                             lf_a[N_META:N_META + dec].reshape(dec, N_Q_HEADS, 1),
                             heads(gates_a[:, W_Q:]), sb_dec,
                             pages(cache_fox_k), pages(cache_fox_v),
                             jnp.swapaxes(cache_fox_logf[0], 1, 2), us_page)

    tail = jnp.zeros((AUX_ROWS - N_META - dec, W_Q), BF16)
    merged_a = jnp.concatenate([merged_meta, merged_dec.reshape(dec, W_Q), tail], axis=0)

    y_main, y_aux = _finish_rows(x_main, x_aux, merged_m, merged_a, wts)

    def prompt_state(a_aux, a_main, tail_shape):
        meta = jnp.broadcast_to(a_aux[None, :N_META], (batch, N_META, a_aux.shape[1]))
        full = jnp.concatenate([meta, a_main.reshape(batch, seq, a_aux.shape[1])], axis=1)
        return full.reshape((1, batch, seq + N_META) + tail_shape)

    kv_shape = (N_KV_HEADS, HEAD_DIM)
    kv_state = lambda st: st.reshape((1, batch, seq + N_META) + kv_shape)
    sample_state = lambda a, tail_shape: a[N_META:N_META + dec].reshape((1, dec, 1) + tail_shape)
    return (y_main.reshape(batch, seq, D_MODEL), y_aux[N_META:N_META + dec].reshape(dec, 1, D_MODEL),
            kv_state(sk_st), kv_state(sv_st), kv_state(fk_st), kv_state(fv_st),
            prompt_state(lf_a, lf_m, (N_Q_HEADS,)),
            sample_state(sk_a, kv_shape), sample_state(sv_a, kv_shape),
            sample_state(fk_a, kv_shape), sample_state(fv_a, kv_shape),
            sample_state(lf_a, (N_Q_HEADS,)))
```

```python
import functools

import jax
import jax.numpy as jnp
from jax import lax
from jax.experimental import pallas as pl
from jax.experimental.pallas import tpu as pltpu

D_MODEL = 2048
HEAD_DIM = 128
N_Q_HEADS = 16
N_KV_HEADS = 4
GROUP = 4
W_Q = N_Q_HEADS * HEAD_DIM
W_KV = N_KV_HEADS * HEAD_DIM
D_FF = 4 * D_MODEL
N_META = 16
PAGE_SIZE = 128
EPS = 1e-6
LOG2E = 1.4426950408889634
Q_SCALE = HEAD_DIM ** -0.5 * LOG2E
FORGET_OFFSET = 2 * W_Q + 4 * W_KV
AUX_ROWS = 32

BF16 = jnp.bfloat16
F32 = jnp.float32

LANES = 128
SUBLANES = 8
NEG_BIG = -1e30
SB_DEAD = 104.0 * LOG2E
VMEM_LIMIT = 48 * 1024 * 1024
FFN_VMEM_LIMIT = 58 * 1024 * 1024
TM_MAIN = 1024
TM_PROJ = 2048
TN = 512
TF = 1024
PROJ_CHUNK = 512

_NT = (((1,), (1,)), ((), ()))


def _params(*sem):
    return pltpu.CompilerParams(dimension_semantics=sem, vmem_limit_bytes=VMEM_LIMIT)


def _rms(x, g):
    ms = jnp.mean(x * x, axis=-1, keepdims=True)
    return x * lax.rsqrt(ms + EPS) * g


def _softplus(z):
    return jnp.maximum(z, 0.0) + jnp.log1p(jnp.exp(-jnp.abs(z)))


def _softplus2(z2):
    return jnp.maximum(z2, 0.0) + jnp.log2(1.0 + jnp.exp2(-jnp.abs(z2)))


def _split2(x):
    hi = x.astype(BF16)
    lo = (x - hi.astype(F32)).astype(BF16)
    return hi, lo


def _split3(x):
    hi = x.astype(BF16)
    r = x - hi.astype(F32)
    mid = r.astype(BF16)
    lo = (r - mid.astype(F32)).astype(BF16)
    return hi, mid, lo


def _dot(a, b):
    return jnp.dot(a, b, preferred_element_type=F32)


def _dot_t(a, bt):
    return lax.dot_general(a, bt, _NT, preferred_element_type=F32)


def _split_dot(parts, m_stacked):
    return _dot(jnp.concatenate(parts, axis=1), m_stacked)


def _rms_forget_kernel(x_ref, g_ref, whi_ref, wlo_ref, b_ref, h_ref, lf_ref):
    h = _rms(x_ref[...], g_ref[...])
    h_ref[...] = h.astype(h_ref.dtype)
    hh, hl = _split2(h)
    z = _dot_t(hh, whi_ref[...]) + _dot_t(hl, whi_ref[...]) + _dot_t(hh, wlo_ref[...])
    z = z[:, :N_Q_HEADS] + b_ref[...]
    lf_ref[...] = -_softplus(-z)


def _rms_forget(x, g, w_hi, w_lo, b, tm):
    rows = x.shape[0]
    const = lambda shape: pl.BlockSpec(shape, lambda i: (0, 0))
    return pl.pallas_call(
        _rms_forget_kernel,
        out_shape=[jax.ShapeDtypeStruct((rows, D_MODEL), BF16),
                   jax.ShapeDtypeStruct((rows, N_Q_HEADS), F32)],
        grid=(rows // tm,),
        in_specs=[pl.BlockSpec((tm, D_MODEL), lambda i: (i, 0)),
                  const((1, D_MODEL)), const((LANES, D_MODEL)), const((LANES, D_MODEL)),
                  const((1, N_Q_HEADS))],
        out_specs=[pl.BlockSpec((tm, D_MODEL), lambda i: (i, 0)),
                   pl.BlockSpec((tm, N_Q_HEADS), lambda i: (i, 0))],
        compiler_params=_params("parallel"),
        name="rms_forget",
    )(x, g, w_hi, w_lo, b)


def _head_norm(acc, g):
    outs = []
    for hh in range(acc.shape[1] // HEAD_DIM):
        xh = acc[:, hh * HEAD_DIM:(hh + 1) * HEAD_DIM]
        outs.append(_rms(xh, g))
    return jnp.concatenate(outs, axis=1)


def _proj_kernel(h_ref, ha_ref, wt_ref, g_ref, o_ref, oa_ref, *, mode):
    wt = wt_ref[...].astype(BF16)
    tm = h_ref.shape[0]
    chunk = min(tm, PROJ_CHUNK)
    pieces = [(h_ref, o_ref, slice(c * chunk, (c + 1) * chunk)) for c in range(tm // chunk)]
    pieces.append((ha_ref, oa_ref, slice(None)))
    acc_next = _dot_t(h_ref[pieces[0][2]], wt)
    for n, (_, dst, rows) in enumerate(pieces):
        acc = acc_next
        if n + 1 < len(pieces):
            src, _, nxt = pieces[n + 1]
            acc_next = _dot_t(src[nxt], wt)
        if mode == "scale":
            acc = acc * Q_SCALE
        elif mode == "norm_scale":
            acc = _head_norm(acc, g_ref[...]) * Q_SCALE
        elif mode == "sigmoid":
            acc = jax.nn.sigmoid(acc)
        dst[rows] = acc.astype(dst.dtype)


def _wt_rows_spec(row0):
    if row0 % TN == 0:
        return pl.BlockSpec((TN, D_MODEL), lambda i, j: (row0 // TN + j, 0))
    assert row0 % SUBLANES == 0
    return pl.BlockSpec((pl.Element(TN), pl.Element(D_MODEL)),
                        lambda i, j: (pl.multiple_of(row0 + j * TN, SUBLANES), 0))


def _proj(h, h_aux, wt, row0, n, g, mode, tm, name):
    rows = h.shape[0]
    out, out_aux = pl.pallas_call(
        functools.partial(_proj_kernel, mode=mode),
        out_shape=[jax.ShapeDtypeStruct((rows, n), BF16),
                   jax.ShapeDtypeStruct((rows // tm, AUX_ROWS, n), BF16)],
        grid=(rows // tm, n // TN),
        in_specs=[pl.BlockSpec((tm, D_MODEL), lambda i, j: (i, 0)),
                  pl.BlockSpec((AUX_ROWS, D_MODEL), lambda i, j: (0, 0)),
                  _wt_rows_spec(row0),
                  pl.BlockSpec((1, HEAD_DIM), lambda i, j: (0, 0))],
        out_specs=[pl.BlockSpec((tm, TN), lambda i, j: (i, j)),
                   pl.BlockSpec((None, AUX_ROWS, TN), lambda i, j: (i, 0, j))],
        compiler_params=_params("parallel", "arbitrary"),
        name=name,
    )(h, h_aux, wt, g)
    return out, out_aux[0]


def _kv_wt_spec():
    sb_kv, fox_kv = W_Q // W_KV, (2 * W_Q + 2 * W_KV) // W_KV
    return pl.BlockSpec((W_KV, D_MODEL), lambda i, j: (sb_kv + j + (fox_kv - sb_kv - 2) * (j // 2), 0))


def _kv_state_kernel(h_ref, ha_ref, wt_ref, gk_ref, *refs, tiles_per_batch):
    state = refs[:4]
    bf_ref, aux_f32_ref, aux_bf_ref, stage, meta_stage, sem, meta_sem = refs[4:]
    i, j = pl.program_id(0), pl.program_id(1)
    n_i = pl.num_programs(0)
    tm = h_ref.shape[0]
    b, it = i // tiles_per_batch, i % tiles_per_batch
    row0 = pl.multiple_of(N_META * N_KV_HEADS + it * tm * N_KV_HEADS, SUBLANES)
    slot = j % 2
    wt = wt_ref[...].astype(BF16)
    finish = lambda a: jnp.where(j == 2, _head_norm(a, gk_ref[...]), a)
    val = finish(_dot_t(h_ref[...], wt))
    val_aux = finish(_dot_t(ha_ref[...], wt))
    bf_ref[...] = val.astype(BF16)
    aux_f32_ref[...] = val_aux
    aux_bf_ref[...] = val_aux.astype(BF16)

    def tile_copy(jj):
        return pltpu.make_async_copy(stage.at[slot], state[jj].at[b, pl.ds(row0, tm * N_KV_HEADS), :],
                                     sem.at[slot])

    def meta_copy(jj):
        return pltpu.make_async_copy(meta_stage.at[jj], state[jj].at[b, pl.ds(0, N_META * N_KV_HEADS), :],
                                     meta_sem.at[jj])

    def per_tensor(fn):
        for jj in range(4):
            pl.when(j == jj)(functools.partial(fn, jj))

    def interleave(dst, val, rows):
        for n in range(N_KV_HEADS):
            dst[pl.ds(n, rows, stride=N_KV_HEADS), :] = val[:rows, n * HEAD_DIM:(n + 1) * HEAD_DIM]

    @pl.when(jnp.logical_or(i > 0, j >= 2))
    def _():
        per_tensor(lambda jj: tile_copy((jj + 2) % 4).wait())

    interleave(stage.at[slot], val, tm)
    per_tensor(lambda jj: tile_copy(jj).start())

    @pl.when(it == 0)
    def _():
        @pl.when(b > 0)
        def _():
            per_tensor(lambda jj: meta_copy(jj).wait())

        interleave(meta_stage.at[j], val_aux, N_META)
        per_tensor(lambda jj: meta_copy(jj).start())

    @pl.when(jnp.logical_and(i == n_i - 1, j == 3))
    def _():
        pltpu.make_async_copy(stage.at[0], state[2].at[b, pl.ds(row0, tm * N_KV_HEADS), :], sem.at[0]).wait()
        tile_copy(3).wait()
        for jj in range(4):
            meta_copy(jj).wait()


def _kv_state_proj(h, h_aux, wt, g_k, tm, batch, seq):
    rows = h.shape[0]
    tiles_per_batch = seq // tm
    state_rows = (N_META + seq) * N_KV_HEADS
    any_spec = pl.BlockSpec(memory_space=pl.ANY)
    aux_spec = pl.BlockSpec((None, AUX_ROWS, W_KV), lambda i, j: (i, 0, j))
    outs = pl.pallas_call(
        functools.partial(_kv_state_kernel, tiles_per_batch=tiles_per_batch),
        out_shape=[jax.ShapeDtypeStruct((batch, state_rows, HEAD_DIM), F32)] * 4
        + [jax.ShapeDtypeStruct((rows, 4 * W_KV), BF16),
           jax.ShapeDtypeStruct((rows // tm, AUX_ROWS, 4 * W_KV), F32),
           jax.ShapeDtypeStruct((rows // tm, AUX_ROWS, 4 * W_KV), BF16)],
        grid=(rows // tm, 4),
        in_specs=[pl.BlockSpec((tm, D_MODEL), lambda i, j: (i, 0)),
                  pl.BlockSpec((AUX_ROWS, D_MODEL), lambda i, j: (0, 0)),
                  _kv_wt_spec(),
                  pl.BlockSpec((1, HEAD_DIM), lambda i, j: (0, 0))],
        out_specs=[any_spec] * 4 + [pl.BlockSpec((tm, W_KV), lambda i, j: (i, j)), aux_spec, aux_spec],
        scratch_shapes=[pltpu.VMEM((2, tm * N_KV_HEADS, HEAD_DIM), F32),
                        pltpu.VMEM((4, N_META * N_KV_HEADS, HEAD_DIM), F32),
                        pltpu.SemaphoreType.DMA((2,)),
                        pltpu.SemaphoreType.DMA((4,))],
        compiler_params=_params("arbitrary", "arbitrary"),
        name="kv_state_proj",
    )(h, h_aux, wt, g_k)
    return outs[:4], outs[4], outs[5][0], outs[6][0]


def _lane_cumsum(x):
    n = x.shape[-1]
    lane = lax.broadcasted_iota(jnp.int32, x.shape, x.ndim - 1)
    k = 1
    while k < n:
        x = x + jnp.where(lane >= k, pltpu.roll(x, k, axis=x.ndim - 1), 0.0)
        k *= 2
    return x


def _cum_kernel(meta_ref, main_ref, cmeta_ref, cmain_ref):
    cm = _lane_cumsum(meta_ref[...])
    cmeta_ref[...] = cm * LOG2E
    cmain_ref[...] = (_lane_cumsum(main_ref[...]) + cm[:, LANES - 1:LANES]) * LOG2E


def _cum_forget(lf_meta_t, lf_main_t):
    b, _, s = lf_main_t.shape
    return pl.pallas_call(
        _cum_kernel,
        out_shape=[jax.ShapeDtypeStruct((b, N_Q_HEADS, LANES), F32),
                   jax.ShapeDtypeStruct((b, N_Q_HEADS, s), F32)],
        grid=(b,),
        in_specs=[pl.BlockSpec((N_Q_HEADS, LANES), lambda i: (0, 0)),
                  pl.BlockSpec((None, N_Q_HEADS, s), lambda i: (i, 0, 0))],
        out_specs=[pl.BlockSpec((None, N_Q_HEADS, LANES), lambda i: (i, 0, 0)),
                   pl.BlockSpec((None, N_Q_HEADS, s), lambda i: (i, 0, 0))],
        compiler_params=_params("parallel"),
        name="cum_forget",
    )(lf_meta_t, lf_main_t)


def _stack_heads(q):
    return jnp.concatenate([q[:, g * HEAD_DIM:(g + 1) * HEAD_DIM] for g in range(GROUP)], axis=0)


SB_CHUNK = 512
FOX_CHUNK = 512


def _chunk_visible(rel, row0, rows, t, tk):
    col = lax.broadcasted_iota(jnp.int32, (rows, tk), 1)
    row = (lax.broadcasted_iota(jnp.int32, (rows, tk), 0) + row0) % t
    return rel(col, row)


def _sb_block(qs, kb, vb, uo, rel, r_ref, acc_ref):
    r, tk = qs.shape[0], kb.shape[0]
    t = r // GROUP
    cr = min(r, SB_CHUNK)
    score = lambda c: lax.dot_general(qs[c * cr:(c + 1) * cr], kb, _NT, preferred_element_type=F32)
    z_next = score(0)
    for c in range(r // cr):
        rows = slice(c * cr, (c + 1) * cr)
        z = z_next
        if c + 1 < r // cr:
            z_next = score(c + 1)
        sp = _softplus2(z)
        if rel is not None:
            vis = _chunk_visible(rel, c * cr, cr, t, tk)
            sp = jnp.where(vis, sp, 0.0)
        ct = _split_dot(_split2(sp), uo)
        r_prev = r_ref[rows]
        a = jnp.exp2(z - ct[:, :tk] - r_prev)
        if rel is not None:
            a = jnp.where(vis, a, 0.0)
        acc_ref[rows] += _dot(a.astype(BF16), vb)
        r_ref[rows] = r_prev + ct[:, tk:]


def _sb_window(qs, kb, vb, uo, r_ref, acc_ref):
    r = qs.shape[0]
    t = r // GROUP
    nb = kb.shape[0] // t
    z = _dot_t(qs, kb)
    sp = _softplus2(z)
    vis = _chunk_visible(lambda c, row: c < row, 0, r, t, t)
    newer = None
    weights = [None] * nb
    for b in reversed(range(nb)):
        cols = slice(b * t, (b + 1) * t)
        diag = b == nb - 1
        spb = jnp.where(vis, sp[:, cols], 0.0) if diag else sp[:, cols]
        ct = _split_dot(_split2(spb), uo)
        arg = z[:, cols] - ct[:, :t]
        a = jnp.exp2(arg if newer is None else arg - newer)
        weights[b] = (jnp.where(vis, a, 0.0) if diag else a).astype(BF16)
        newer = ct[:, t:] if newer is None else newer + ct[:, t:]
    acc_ref[...] = _dot(jnp.concatenate(weights, axis=1), vb)
    r_ref[...] = newer


def _fox_block(qs, kb, vb, ckb, rel, m_ref, acc_ref, own_keys=False):
    r, tk = qs.shape[0], kb.shape[0]
    t = r // GROUP
    if own_keys:
        assert tk == t
        chains = [(g * t + part * (t // 2), t // 2, (part + 1) * (tk // 2))
                  for g in range(GROUP) for part in range(2)]
    else:
        cr = min(r, FOX_CHUNK)
        chains = [(c * cr, cr, tk) for c in range(r // cr)]
    v1 = jnp.concatenate([vb, jnp.ones((tk, LANES), BF16)], axis=1)
    score = lambda row0, n, kc: lax.dot_general(qs[row0:row0 + n], kb[:kc], _NT, preferred_element_type=F32)
    z_next = score(*chains[0])
    for c, (row0, n, kc) in enumerate(chains):
        rows = slice(row0, row0 + n)
        hc = max(n // t, 1)
        bias = ckb[row0 // t:row0 // t + hc, :kc]
        z = (z_next.reshape(hc, n // hc, kc) - bias[:, None, :]).reshape(n, kc)
        if c + 1 < len(chains):
            z_next = score(*chains[c + 1])
        if rel is not None:
            z = jnp.where(_chunk_visible(rel, row0, n, t, kc), z, NEG_BIG)
        m_prev = m_ref[rows]
        m_new = jnp.maximum(m_prev, jnp.max(z, axis=1, keepdims=True))
        alpha = jnp.exp2(m_prev - m_new)
        p = jnp.exp2(z - jnp.tile(m_new, (1, kc // LANES)))
        acc_ref[rows] = acc_ref[rows] * jnp.tile(alpha, (1, 2)) + _dot(p.astype(BF16), v1[:kc])
        m_ref[rows] = m_new


SB_T = 128
SB_WIN = 3
SB_SUB = 8


def _store_heads(o_ref, o, t, gate_ref, add_ref=None, row0=0):
    rows = slice(row0, row0 + t)
    for g in range(GROUP):
        cols = slice(g * HEAD_DIM, (g + 1) * HEAD_DIM)
        val = o[g * t:(g + 1) * t] * gate_ref[rows, cols].astype(F32)
        if add_ref is not None:
            val = val + add_ref[rows, cols].astype(F32)
        o_ref[rows, cols] = val.astype(o_ref.dtype)


def _sb_prompt_kernel(q_ref, k_ref, v_ref, mk_ref, mv_ref, uo_ref, gate_ref, o_ref, r_ref, acc_ref):
    i = pl.program_id(2)
    t = SB_T
    uo = uo_ref[...]
    qs = [_stack_heads(q_ref[sub * t:(sub + 1) * t]) for sub in range(SB_SUB)]
    tile = lambda sub: i * SB_SUB + sub

    def first_block_only(sub):
        r_ref[sub] = jnp.zeros(r_ref.shape[1:], F32)
        acc_ref[sub] = jnp.zeros(acc_ref.shape[1:], F32)
        s = pl.multiple_of(tile(sub) * t, t)
        _sb_block(qs[sub], k_ref[pl.ds(s, t), :], v_ref[pl.ds(s, t), :], uo, lambda c, r: c < r,
                  r_ref.at[sub], acc_ref.at[sub])

    assert SB_SUB >= SB_WIN - 1

    @pl.when(i > 0)
    def _():
        for sub in range(SB_SUB):
            s = pl.multiple_of((tile(sub) - (SB_WIN - 1)) * t, t)
            _sb_window(qs[sub], k_ref[pl.ds(s, SB_WIN * t), :], v_ref[pl.ds(s, SB_WIN * t), :], uo,
                       r_ref.at[sub], acc_ref.at[sub])

    @pl.when(i == 0)
    def _():
        for sub in range(SB_SUB):
            if sub >= SB_WIN - 1:
                _sb_window(qs[sub], k_ref[pl.ds((sub - (SB_WIN - 1)) * t, SB_WIN * t), :],
                           v_ref[pl.ds((sub - (SB_WIN - 1)) * t, SB_WIN * t), :], uo,
                           r_ref.at[sub], acc_ref.at[sub])
            else:
                first_block_only(sub)

    for sub in range(SB_SUB):
        def visit(kb, vb, rel, sub=sub):
            _sb_block(qs[sub], kb, vb, uo, rel, r_ref.at[sub], acc_ref.at[sub])
            return jnp.min(r_ref[sub])

        def cond(c):
            return jnp.logical_and(c[0] >= 0, c[1] < SB_DEAD)

        def body(c, visit=visit):
            s = pl.multiple_of(c[0] * t, t)
            return c[0] - 1, visit(k_ref[pl.ds(s, t), :], v_ref[pl.ds(s, t), :], None)

        covered = jnp.where(jnp.logical_or(i > 0, sub >= SB_WIN - 1), SB_WIN, 1)
        _, mn = lax.while_loop(cond, body, (tile(sub) - covered, jnp.min(r_ref[sub])))

        @pl.when(mn < SB_DEAD)
        def _(visit=visit):
            visit(mk_ref[...], mv_ref[...], lambda c, r: c < N_META)

        _store_heads(o_ref, acc_ref[sub], t, gate_ref, row0=sub * t)


def _kv_head_spec(seq, tensor):
    return pl.BlockSpec((seq, HEAD_DIM), lambda b, n, i: (b, tensor * N_KV_HEADS + n))


def _sb_prompt(q, kv, mk, mv, uo, gates, batch, seq):
    t = SB_T * SB_SUB
    nb = seq // t
    tile = pl.BlockSpec((t, GROUP * HEAD_DIM), lambda b, n, i: (b * nb + i, n))
    return pl.pallas_call(
        _sb_prompt_kernel,
        out_shape=jax.ShapeDtypeStruct((batch * seq, W_Q), BF16),
        grid=(batch, N_KV_HEADS, nb),
        in_specs=[tile,
                  _kv_head_spec(seq, 0),
                  _kv_head_spec(seq, 1),
                  pl.BlockSpec((LANES, HEAD_DIM), lambda b, n, i: (0, n)),
                  pl.BlockSpec((LANES, HEAD_DIM), lambda b, n, i: (0, n)),
                  pl.BlockSpec(uo.shape, lambda b, n, i: (0, 0)),
                  tile],
        out_specs=tile,
        scratch_shapes=[pltpu.VMEM((SB_SUB, GROUP * SB_T, SB_T), F32),
                        pltpu.VMEM((SB_SUB, GROUP * SB_T, HEAD_DIM), F32)],
        compiler_params=_params("parallel", "parallel", "arbitrary"),
        name="sb_prompt",
    )(q, kv, kv, mk, mv, uo, gates)


FOX_T = 512
FOX_TK = 1024


def _fox_prompt_kernel(q_ref, k_ref, v_ref, ck_ref, mk_ref, mv_ref, mck_ref, gate_ref, sb_ref,
                       o_ref, m_ref, acc_ref):
    i = pl.program_id(2)
    t = FOX_T
    qs = _stack_heads(q_ref[...])
    m_ref[...] = jnp.full_like(m_ref, NEG_BIG)
    acc_ref[...] = jnp.zeros_like(acc_ref)

    def visit(s, width, rel, own_keys=False):
        _fox_block(qs, k_ref[pl.ds(s, width), :], v_ref[pl.ds(s, width), :], ck_ref[:, pl.ds(s, width)],
                   rel, m_ref, acc_ref, own_keys)

    visit(pl.multiple_of(i * t, t), t, lambda c, r: c <= r, own_keys=True)

    per_visit = FOX_TK // t

    def body(j, carry):
        visit(pl.multiple_of(j * FOX_TK, FOX_TK), FOX_TK, None)
        return carry

    lax.fori_loop(0, i // per_visit, body, 0)

    for left in range(per_visit - 1):
        @pl.when(i % per_visit > left)
        def _(left=left):
            visit(pl.multiple_of((i - i % per_visit + left) * t, t), t, None)

    _fox_block(qs, mk_ref[...], mv_ref[...], mck_ref[...],
               lambda c, r: c < N_META, m_ref, acc_ref)

    acc = acc_ref[...]
    _store_heads(o_ref, acc[:, :HEAD_DIM] / acc[:, HEAD_DIM:], t, gate_ref, sb_ref)


def _fox_prompt(q, kv, ck, mk, mv, mck, gates, sb_part, batch, seq):
    t = FOX_T
    nb = seq // t
    tile = pl.BlockSpec((t, GROUP * HEAD_DIM), lambda b, n, i: (b * nb + i, n))
    return pl.pallas_call(
        _fox_prompt_kernel,
        out_shape=jax.ShapeDtypeStruct((batch * seq, W_Q), BF16),
        grid=(batch, N_KV_HEADS, nb),
        in_specs=[tile,
                  _kv_head_spec(seq, 2),
                  _kv_head_spec(seq, 3),
                  pl.BlockSpec((None, None, GROUP, seq), lambda b, n, i: (b, n, 0, 0)),
                  pl.BlockSpec((LANES, HEAD_DIM), lambda b, n, i: (0, n)),
                  pl.BlockSpec((LANES, HEAD_DIM), lambda b, n, i: (0, n)),
                  pl.BlockSpec((None, None, GROUP, LANES), lambda b, n, i: (b, n, 0, 0)),
                  pl.BlockSpec((t, GROUP * HEAD_DIM), lambda b, n, i: (b * nb + i, N_KV_HEADS + n)),
                  tile],
        out_specs=tile,
        scratch_shapes=[pltpu.VMEM((GROUP * t, LANES), F32), pltpu.VMEM((GROUP * t, 2 * LANES), F32)],
        compiler_params=_params("parallel", "parallel", "arbitrary"),
        name="fox_prompt",
    )(q, kv, kv, ck, mk, mv, mck, gates, sb_part)


def _meta_attn_kernel(sq_ref, sk_ref, sv_ref, fq_ref, fk_ref, fv_ref, ck_ref, uo_ref, gsb_ref, gfx_ref,
                      o_ref, m_ref, acc_ref, r_ref, sacc_ref):
    t = N_META
    valid = lambda c: c < N_META
    r_ref[...] = jnp.zeros_like(r_ref)
    sacc_ref[...] = jnp.zeros_like(sacc_ref)
    _sb_block(_stack_heads(sq_ref[...]), sk_ref[...], sv_ref[...], uo_ref[...],
              lambda c, r: jnp.logical_and(c < r, valid(c)), r_ref, sacc_ref)
    o = sacc_ref[...]
    m_ref[...] = jnp.full_like(m_ref, NEG_BIG)
    acc_ref[...] = jnp.zeros_like(acc_ref)
    _fox_block(_stack_heads(fq_ref[...]), fk_ref[...], fv_ref[...], ck_ref[...],
               lambda c, r: jnp.logical_and(c <= r, valid(c)), m_ref, acc_ref)
    acc = acc_ref[...]
    f = acc[:, :HEAD_DIM] / acc[:, HEAD_DIM:]
    for g in range(GROUP):
        cols = slice(g * HEAD_DIM, (g + 1) * HEAD_DIM)
        merged = (gsb_ref[:, cols].astype(F32) * o[g * t:(g + 1) * t]
                  + gfx_ref[:, cols].astype(F32) * f[g * t:(g + 1) * t])
        o_ref[:, cols] = merged.astype(o_ref.dtype)


def _meta_attn(sq, sk, sv, fq, fk, fv, ck, uo, gates):
    qspec = pl.BlockSpec((N_META, GROUP * HEAD_DIM), lambda n: (0, n))
    kspec = pl.BlockSpec((LANES, HEAD_DIM), lambda n: (0, n))
    return pl.pallas_call(
        _meta_attn_kernel,
        out_shape=jax.ShapeDtypeStruct((N_META, W_Q), BF16),
        grid=(N_KV_HEADS,),
        in_specs=[qspec, kspec, kspec, qspec, kspec, kspec,
                  pl.BlockSpec((None, GROUP, LANES), lambda n: (n, 0, 0)),
                  pl.BlockSpec(uo.shape, lambda n: (0, 0)),
                  qspec,
                  pl.BlockSpec((N_META, GROUP * HEAD_DIM), lambda n: (0, N_KV_HEADS + n))],
        out_specs=qspec,
        scratch_shapes=[pltpu.VMEM((GROUP * N_META, LANES), F32),
                        pltpu.VMEM((GROUP * N_META, 2 * LANES), F32),
                        pltpu.VMEM((GROUP * N_META, LANES), F32),
                        pltpu.VMEM((GROUP * N_META, HEAD_DIM), F32)],
        compiler_params=_params("parallel"),
        name="meta_attn",
    )(sq, sk, sv, fq, fk, fv, ck, uo, gates, gates)


PAGE_COLS = PAGE_SIZE * N_KV_HEADS
DEC_G = 16


def _own_head(shape):
    row = lax.broadcasted_iota(jnp.int32, shape, 0)
    col = lax.broadcasted_iota(jnp.int32, shape, 1)
    return (col % N_KV_HEADS) == (row // GROUP)


def _sb_decode_kernel(pt_ref, q_ref, gate_ref, ck_hbm, cv_hbm, uo_ref, o_ref, kbuf, vbuf, sem, r_ref, acc_ref):
    b = pl.program_id(0)
    n_pages = pt_ref.shape[1]
    q = q_ref[...]
    uo = uo_ref[...]
    own = _own_head((N_Q_HEADS, PAGE_COLS))

    def copies(p, slot):
        pid = pt_ref[b, p]
        return [pltpu.make_async_copy(ck_hbm.at[pid], kbuf.at[slot], sem.at[0, slot]),
                pltpu.make_async_copy(cv_hbm.at[pid], vbuf.at[slot], sem.at[1, slot])]

    r_ref[...] = jnp.zeros_like(r_ref)
    acc_ref[...] = jnp.zeros_like(acc_ref)
    for c in copies(n_pages - 1, (n_pages - 1) % 2):
        c.start()

    def cond(c):
        return jnp.logical_and(c[0] >= 0, c[1] < SB_DEAD)

    def body(c):
        p = c[0]
        slot = p % 2
        for cp in copies(p, slot):
            cp.wait()

        @pl.when(p > 0)
        def _():
            for cp in copies(p - 1, 1 - slot):
                cp.start()

        z = lax.dot_general(q, kbuf[slot].astype(BF16), _NT, preferred_element_type=F32)
        sp = jnp.where(own, _softplus2(z), 0.0)
        ct = _split_dot(_split2(sp), uo)
        r = r_ref[...]
        a = jnp.where(own, jnp.exp2(z - ct[:, :PAGE_COLS] - jnp.tile(r, (1, N_KV_HEADS))), 0.0)
        acc_ref[...] += _dot(a.astype(BF16), vbuf[slot].astype(BF16))
        r_new = r + ct[:, PAGE_COLS:]
        r_ref[...] = r_new
        return p - 1, jnp.min(r_new)

    p_end, _ = lax.while_loop(cond, body, (jnp.int32(n_pages - 1), jnp.float32(0.0)))

    @pl.when(p_end >= 0)
    def _():
        for cp in copies(p_end, p_end % 2):
            cp.wait()

    o_ref[...] = (acc_ref[...] * gate_ref[...].astype(F32)).astype(o_ref.dtype)


def _sb_decode(page_table, q, gate, cache_k, cache_v, uo):
    nb = q.shape[0]
    any_spec = pl.BlockSpec(memory_space=pl.ANY)
    head_spec = pl.BlockSpec((None, N_Q_HEADS, HEAD_DIM), lambda b, pt: (b, 0, 0))
    return pl.pallas_call(
        _sb_decode_kernel,
        out_shape=jax.ShapeDtypeStruct((nb, N_Q_HEADS, HEAD_DIM), BF16),
        grid_spec=pltpu.PrefetchScalarGridSpec(
            num_scalar_prefetch=1,
            grid=(nb,),
            in_specs=[head_spec, head_spec, any_spec, any_spec,
                      pl.BlockSpec(uo.shape, lambda b, pt: (0, 0))],
            out_specs=head_spec,
            scratch_shapes=[pltpu.VMEM((2, PAGE_COLS, HEAD_DIM), F32),
                            pltpu.VMEM((2, PAGE_COLS, HEAD_DIM), F32),
                            pltpu.SemaphoreType.DMA((2, 2)),
                            pltpu.VMEM((N_Q_HEADS, LANES), F32),
                            pltpu.VMEM((N_Q_HEADS, HEAD_DIM), F32)]),
        compiler_params=_params("arbitrary"),
        name="sb_decode",
    )(page_table, q, gate, cache_k, cache_v, uo)


def _fox_decode_kernel(pt_ref, q_ref, kn_ref, vn_ref, lfn_ref, gate_ref, sb_ref, ck_hbm, cv_hbm, lf_hbm,
                       us_ref, o_ref, kbuf, vbuf, lbuf, sem, m_ref, acc_ref, d_ref):
    n_seq, n_pages = pt_ref.shape
    n_grp = n_pages // DEC_G
    total = n_seq * n_grp
    us = us_ref[...]
    bias = jnp.where(_own_head((N_Q_HEADS, PAGE_COLS)), 0.0, NEG_BIG)

    def copies(t, slot):
        b = t // n_grp
        newest = n_pages - 1 - (t % n_grp) * DEC_G
        out = []
        for g in range(DEC_G):
            pid = pt_ref[b, newest - g]
            out += [pltpu.make_async_copy(ck_hbm.at[pid], kbuf.at[slot, g], sem.at[0, slot]),
                    pltpu.make_async_copy(cv_hbm.at[pid], vbuf.at[slot, g], sem.at[1, slot]),
                    pltpu.make_async_copy(lf_hbm.at[pid], lbuf.at[slot, g], sem.at[2, slot])]
        return out

    for cp in copies(0, 0):
        cp.start()

    def body(t, carry):
        slot = t % 2
        b = t // n_grp
        gi = t % n_grp

        @pl.when(t + 1 < total)
        def _():
            for cp in copies(t + 1, 1 - slot):
                cp.start()

        q = q_ref[b]

        @pl.when(gi == 0)
        def _():
            z_self = jnp.sum(q.astype(F32) * kn_ref[b].astype(F32), axis=1, keepdims=True)
            m_ref[...] = jnp.broadcast_to(z_self, m_ref.shape)
            acc_ref[...] = jnp.concatenate([vn_ref[b].astype(F32), jnp.ones((N_Q_HEADS, LANES), F32)], axis=1)
            d_ref[...] = jnp.broadcast_to(lfn_ref[b] * LOG2E, d_ref.shape)

        for cp in copies(t, slot):
            cp.wait()

        dt = _split_dot(_split3(lbuf[slot].reshape(DEC_G * N_Q_HEADS, PAGE_SIZE)), us) * LOG2E
        d = d_ref[...]
        zs = []
        for g in range(DEC_G):
            dg = dt[g * N_Q_HEADS:(g + 1) * N_Q_HEADS]
            z = lax.dot_general(q, kbuf[slot, g].astype(BF16), _NT, preferred_element_type=F32)
            zs.append(z + dg[:, :PAGE_COLS] + jnp.tile(d, (1, N_KV_HEADS)) + bias)
            d = d + dg[:, PAGE_COLS:]
        d_ref[...] = d
        z = jnp.concatenate(zs, axis=1)
        m_prev = m_ref[...]
        m_new = jnp.maximum(m_prev, jnp.max(z, axis=1, keepdims=True))
        alpha = jnp.exp2(m_prev - m_new)
        p = jnp.exp2(z - jnp.tile(m_new, (1, z.shape[1] // LANES)))
        lsum = jnp.sum(p, axis=1, keepdims=True)
        pb = p.astype(BF16)
        pv = _dot(pb[:, :PAGE_COLS], vbuf[slot, 0].astype(BF16))
        for g in range(1, DEC_G):
            pv += _dot(pb[:, g * PAGE_COLS:(g + 1) * PAGE_COLS], vbuf[slot, g].astype(BF16))
        acc = acc_ref[...] * jnp.tile(alpha, (1, 2)) + jnp.concatenate(
            [pv, jnp.broadcast_to(lsum, pv.shape)], axis=1)
        acc_ref[...] = acc
        m_ref[...] = m_new

        @pl.when(gi == n_grp - 1)
        def _():
            merged = acc[:, :HEAD_DIM] / acc[:, HEAD_DIM:] * gate_ref[b].astype(F32) + sb_ref[b].astype(F32)
            o_ref[b] = merged.astype(o_ref.dtype)

        return carry

    lax.fori_loop(0, total, body, 0)


def _fox_decode(page_table, q, k_new, v_new, lf_new, gate, sb_part, cache_k, cache_v, cache_lf_t, us):
    nb = q.shape[0]
    assert page_table.shape[1] % DEC_G == 0
    any_spec = pl.BlockSpec(memory_space=pl.ANY)
    head_spec = pl.BlockSpec((nb, N_Q_HEADS, HEAD_DIM), lambda i, pt: (0, 0, 0))
    return pl.pallas_call(
        _fox_decode_kernel,
        out_shape=jax.ShapeDtypeStruct((nb, N_Q_HEADS, HEAD_DIM), BF16),
        grid_spec=pltpu.PrefetchScalarGridSpec(
            num_scalar_prefetch=1,
            grid=(1,),
            in_specs=[head_spec, head_spec, head_spec,
                      pl.BlockSpec((nb, N_Q_HEADS, 1), lambda i, pt: (0, 0, 0)),
                      head_spec, head_spec,
                      any_spec, any_spec, any_spec,
                      pl.BlockSpec(us.shape, lambda i, pt: (0, 0))],
            out_specs=head_spec,
            scratch_shapes=[pltpu.VMEM((2, DEC_G, PAGE_COLS, HEAD_DIM), F32),
                            pltpu.VMEM((2, DEC_G, PAGE_COLS, HEAD_DIM), F32),
                            pltpu.VMEM((2, DEC_G, N_Q_HEADS, PAGE_SIZE), F32),
                            pltpu.SemaphoreType.DMA((3, 2)),
                            pltpu.VMEM((N_Q_HEADS, LANES), F32),
                            pltpu.VMEM((N_Q_HEADS, 2 * LANES), F32),
                            pltpu.VMEM((N_Q_HEADS, LANES), F32)]),
        compiler_params=_params("arbitrary"),
        name="fox_decode",
    )(page_table, q, k_new, v_new, lf_new, gate, sb_part, cache_k, cache_v, cache_lf_t, us)


def _out_proj_kernel(m_ref, ma_ref, w_ref, x_ref, xa_ref, y_ref, ya_ref):
    w = w_ref[...]
    y_ref[...] = x_ref[...] + _dot(m_ref[...], w)
    ya_ref[...] = xa_ref[...] + _dot(ma_ref[...], w)


def _out_proj(merged, merged_aux, w, x, x_aux, tm):
    rows = x.shape[0]
    y, y_aux = pl.pallas_call(
        _out_proj_kernel,
        out_shape=[jax.ShapeDtypeStruct((rows, D_MODEL), F32),
                   jax.ShapeDtypeStruct((rows // tm, AUX_ROWS, D_MODEL), F32)],
        grid=(rows // tm, D_MODEL // TN),
        in_specs=[pl.BlockSpec((tm, W_Q), lambda i, j: (i, 0)),
                  pl.BlockSpec((AUX_ROWS, W_Q), lambda i, j: (0, 0)),
                  pl.BlockSpec((W_Q, TN), lambda i, j: (0, j)),
                  pl.BlockSpec((tm, TN), lambda i, j: (i, j)),
                  pl.BlockSpec((AUX_ROWS, TN), lambda i, j: (0, j))],
        out_specs=[pl.BlockSpec((tm, TN), lambda i, j: (i, j)),
                   pl.BlockSpec((None, AUX_ROWS, TN), lambda i, j: (i, 0, j))],
        compiler_params=_params("parallel", "arbitrary"),
        name="out_proj",
    )(merged, merged_aux, w, x, x_aux)
    return y, y_aux[0]


def _ffn_kernel(y_ref, ya_ref, g_ref, wu_ref, wd_ref, o_ref, oa_ref, h_ref, ha_ref):
    i, k = pl.program_id(0), pl.program_id(1)

    def start(src, h_dst, o_dst):
        y = src[...]
        h_dst[...] = _rms(y, g_ref[...]).astype(BF16)
        o_dst[...] = y

    def step(h_src, o_dst):
        u = jnp.square(jnp.maximum(_dot(h_src[...], wu_ref[...]), 0.0))
        o_dst[...] += _dot(u.astype(BF16), wd_ref[...])

    @pl.when(k == 0)
    def _():
        start(y_ref, h_ref, o_ref)

    @pl.when(jnp.logical_and(i == 0, k == 0))
    def _():
        start(ya_ref, ha_ref, oa_ref)

    step(h_ref, o_ref)

    @pl.when(i == 0)
    def _():
        step(ha_ref, oa_ref)


def _ffn(y, y_aux, g, w_up, w_down, tm):
    rows = y.shape[0]
    aux_spec = pl.BlockSpec((AUX_ROWS, D_MODEL), lambda i, k: (0, 0))
    return pl.pallas_call(
        _ffn_kernel,
        out_shape=[jax.ShapeDtypeStruct((rows, D_MODEL), F32), jax.ShapeDtypeStruct((AUX_ROWS, D_MODEL), F32)],
        grid=(rows // tm, D_FF // TF),
        in_specs=[pl.BlockSpec((tm, D_MODEL), lambda i, k: (i, 0)),
                  aux_spec,
                  pl.BlockSpec((1, D_MODEL), lambda i, k: (0, 0)),
                  pl.BlockSpec((D_MODEL, TF), lambda i, k: (0, k)),
                  pl.BlockSpec((TF, D_MODEL), lambda i, k: (k, 0))],
        out_specs=[pl.BlockSpec((tm, D_MODEL), lambda i, k: (i, 0)), aux_spec],
        scratch_shapes=[pltpu.VMEM((tm, D_MODEL), BF16), pltpu.VMEM((AUX_ROWS, D_MODEL), BF16)],
        compiler_params=pltpu.CompilerParams(dimension_semantics=("arbitrary", "arbitrary"),
                                             vmem_limit_bytes=FFN_VMEM_LIMIT),
        name="ffn",
    )(y, y_aux, g, w_up, w_down)


def _project_rows(x, x_aux, wts, batch, seq):
    norm = lambda rows, tm: _rms_forget(rows, wts["g_mix"], wts["w_fl_hi"], wts["w_fl_lo"], wts["b_f"], tm)
    h, lf = norm(x, TM_MAIN // 2)
    h_aux, lf_aux = norm(x_aux, AUX_ROWS)
    wt = wts["w_in_t"]
    sq = _proj(h, h_aux, wt, 0, W_Q, wts["g_q"], "scale", TM_PROJ, "proj_sq")
    fq = _proj(h, h_aux, wt, W_Q + 2 * W_KV, W_Q, wts["g_q"], "norm_scale", TM_PROJ, "proj_fq")
    gates = _proj(h, h_aux, wt, FORGET_OFFSET + N_Q_HEADS, 2 * D_MODEL, wts["g_q"], "sigmoid", TM_PROJ,
                  "proj_gates")
    kv = _kv_state_proj(h, h_aux, wt, wts["g_k"], TM_PROJ, batch, seq)
    return sq, fq, gates, kv, (lf, lf_aux)


def _finish_rows(x, x_aux, merged, merged_aux, wts):
    y1, y1_aux = _out_proj(merged, merged_aux, wts["w_out"], x, x_aux, TM_PROJ)
    return _ffn(y1, y1_aux, wts["g_ffn"], wts["w_up"], wts["w_down"], TM_MAIN)


def kernel(x_prompt, x_sample, cache_sb_k, cache_sb_v, cache_fox_k, cache_fox_v, cache_fox_logf,
           page_table, meta_tokens, g_mix, w_in, b_forget, g_q, g_k, w_out, g_ffn, w_up, w_down):
    batch, seq, _ = x_prompt.shape
    dec = x_sample.shape[0]
    assert w_in.shape[0] == 1 and x_sample.shape[1] == 1 and N_META + dec <= AUX_ROWS
    assert seq % TM_PROJ == 0 and seq % (SB_T * SB_SUB) == 0 and seq % FOX_T == 0
    pool = cache_sb_k.shape[1]

    w_in_t = jnp.swapaxes(w_in[0], 0, 1)
    w_fl = jnp.pad(w_in_t[FORGET_OFFSET:FORGET_OFFSET + N_Q_HEADS], ((0, LANES - N_Q_HEADS), (0, 0)))
    w_fl_hi = w_fl.astype(BF16)
    wts = {
        "g_mix": g_mix, "g_q": g_q, "g_k": g_k, "g_ffn": g_ffn, "b_f": b_forget,
        "w_in_t": w_in_t,
        "w_fl_hi": w_fl_hi, "w_fl_lo": (w_fl - w_fl_hi.astype(F32)).astype(BF16),
        "w_out": w_out[0].astype(BF16), "w_up": w_up[0].astype(BF16), "w_down": w_down[0].astype(BF16),
    }

    x_main = x_prompt.reshape(batch * seq, D_MODEL)
    x_aux = jnp.concatenate([meta_tokens, x_sample.reshape(dec, D_MODEL),
                             jnp.zeros((AUX_ROWS - N_META - dec, D_MODEL), F32)], axis=0)

    (sq_m, sq_a), (fq_m, fq_a), (gates_m, gates_a), kv, (lf_m, lf_a) = _project_rows(
        x_main, x_aux, wts, batch, seq)
    (sk_st, sv_st, fk_st, fv_st), kvb_m, kv_a, kvb_a = kv
    four = lambda a: [a[:, t * W_KV:(t + 1) * W_KV] for t in range(4)]
    sk_a, sv_a, fk_a, fv_a = four(kv_a)
    skb_a, svb_a, fkb_a, fvb_a = four(kvb_a)

    lf_meta_t = jnp.pad(lf_a[:N_META].T, ((0, 0), (0, LANES - N_META)))
    lf_main_t = jnp.swapaxes(lf_m.reshape(batch, seq, N_Q_HEADS), 1, 2)
    ck_meta, ck_main = _cum_forget(lf_meta_t, lf_main_t)
    ck_meta = ck_meta.reshape(batch, N_KV_HEADS, GROUP, LANES)
    ck_main = ck_main.reshape(batch, N_KV_HEADS, GROUP, seq)

    pad_keys = lambda a: jnp.pad(a[:N_META], ((0, LANES - N_META), (0, 0)))
    mk_sb, mv_sb, mk_fx, mv_fx = pad_keys(skb_a), pad_keys(svb_a), pad_keys(fkb_a), pad_keys(fvb_a)

    j_idx = lax.broadcasted_iota(jnp.int32, (LANES, LANES), 0)
    s_idx = lax.broadcasted_iota(jnp.int32, (LANES, LANES), 1)
    ones = jnp.ones((LANES, LANES), BF16)
    uo = jnp.concatenate([(j_idx >= s_idx).astype(BF16), ones], axis=1)
    uo = jnp.concatenate([uo] * 2, axis=0)
    c_src = lax.broadcasted_iota(jnp.int32, (PAGE_COLS, PAGE_COLS), 0) // N_KV_HEADS
    c_dst = lax.broadcasted_iota(jnp.int32, (PAGE_COLS, PAGE_COLS), 1) // N_KV_HEADS
    uo_page = jnp.concatenate([(c_src >= c_dst).astype(BF16), jnp.ones((PAGE_COLS, LANES), BF16)], axis=1)
    uo_page = jnp.concatenate([uo_page] * 2, axis=0)
    j_key = lax.broadcasted_iota(jnp.int32, (PAGE_SIZE, PAGE_COLS), 0)
    c_key = lax.broadcasted_iota(jnp.int32, (PAGE_SIZE, PAGE_COLS), 1) // N_KV_HEADS
    us_page = jnp.concatenate([(j_key > c_key).astype(BF16), ones], axis=1)
    us_page = jnp.concatenate([us_page] * 3, axis=0)

    sb_m = _sb_prompt(sq_m, kvb_m, mk_sb, mv_sb, uo, gates_m, batch, seq)
    merged_m = _fox_prompt(fq_m, kvb_m, ck_main, mk_fx, mv_fx, ck_meta, gates_m, sb_m, batch, seq)
    merged_meta = _meta_attn(sq_a[:N_META], mk_sb, mv_sb, fq_a[:N_META], mk_fx, mv_fx,
                             ck_meta[0], uo, gates_a[:N_META])

    heads = lambda a: a[N_META:N_META + dec].reshape(dec, N_Q_HEADS, HEAD_DIM)
    kv_heads = lambda a: jnp.repeat(a[N_META:N_META + dec].reshape(dec, N_KV_HEADS, HEAD_DIM), GROUP, axis=1)
    pages = lambda c: c[0].reshape(pool, PAGE_COLS, HEAD_DIM)
    sb_dec = _sb_decode(page_table, heads(sq_a), heads(gates_a[:, :W_Q]),
                        pages(cache_sb_k), pages(cache_sb_v), uo_page)
    merged_dec = _fox_decode(page_table, heads(fq_a), kv_heads(fkb_a), kv_heads(fvb_a),
                             lf_a[N_META:N_META + dec].reshape(dec, N_Q_HEADS, 1),
                             heads(gates_a[:, W_Q:]), sb_dec,
                             pages(cache_fox_k), pages(cache_fox_v),
                             jnp.swapaxes(cache_fox_logf[0], 1, 2), us_page)

    tail = jnp.zeros((AUX_ROWS - N_META - dec, W_Q), BF16)
    merged_a = jnp.concatenate([merged_meta, merged_dec.reshape(dec, W_Q), tail], axis=0)

    y_main, y_aux = _finish_rows(x_main, x_aux, merged_m, merged_a, wts)

    def prompt_state(a_aux, a_main, tail_shape):
        meta = jnp.broadcast_to(a_aux[None, :N_META], (batch, N_META, a_aux.shape[1]))
        full = jnp.concatenate([meta, a_main.reshape(batch, seq, a_aux.shape[1])], axis=1)
        return full.reshape((1, batch, seq + N_META) + tail_shape)

    kv_shape = (N_KV_HEADS, HEAD_DIM)
    kv_state = lambda st: st.reshape((1, batch, seq + N_META) + kv_shape)
    sample_state = lambda a, tail_shape: a[N_META:N_META + dec].reshape((1, dec, 1) + tail_shape)
    return (y_main.reshape(batch, seq, D_MODEL), y_aux[N_META:N_META + dec].reshape(dec, 1, D_MODEL),
            kv_state(sk_st), kv_state(sv_st), kv_state(fk_st), kv_state(fv_st),
            prompt_state(lf_a, lf_m, (N_Q_HEADS,)),
            sample_state(sk_a, kv_shape), sample_state(sv_a, kv_shape),
            sample_state(fk_a, kv_shape), sample_state(fv_a, kv_shape),
            sample_state(lf_a, (N_Q_HEADS,)))
```

```python
import functools

import jax
import jax.numpy as jnp
from jax import lax
from jax.experimental import pallas as pl
from jax.experimental.pallas import tpu as pltpu

D_MODEL = 2048
HEAD_DIM = 128
N_Q_HEADS = 16
N_KV_HEADS = 4
GROUP = 4
W_Q = N_Q_HEADS * HEAD_DIM
W_KV = N_KV_HEADS * HEAD_DIM
D_FF = 4 * D_MODEL
N_META = 16
PAGE_SIZE = 128
EPS = 1e-6
LOG2E = 1.4426950408889634
Q_SCALE = HEAD_DIM ** -0.5 * LOG2E
FORGET_OFFSET = 2 * W_Q + 4 * W_KV
AUX_ROWS = 32

BF16 = jnp.bfloat16
F32 = jnp.float32

LANES = 128
SUBLANES = 8
NEG_BIG = -1e30
SB_DEAD = 104.0 * LOG2E
VMEM_LIMIT = 48 * 1024 * 1024
FFN_VMEM_LIMIT = 58 * 1024 * 1024
TM_MAIN = 1024
TM_PROJ = 2048
TN = 512
TF = 1024
PROJ_CHUNK = 256

_NT = (((1,), (1,)), ((), ()))


def _params(*sem):
    return pltpu.CompilerParams(dimension_semantics=sem, vmem_limit_bytes=VMEM_LIMIT)


def _rms(x, g):
    ms = jnp.mean(x * x, axis=-1, keepdims=True)
    return x * lax.rsqrt(ms + EPS) * g


def _softplus(z):
    return jnp.maximum(z, 0.0) + jnp.log1p(jnp.exp(-jnp.abs(z)))


def _softplus2(z2):
    return jnp.maximum(z2, 0.0) + jnp.log2(1.0 + jnp.exp2(-jnp.abs(z2)))


def _split2(x):
    hi = x.astype(BF16)
    lo = (x - hi.astype(F32)).astype(BF16)
    return hi, lo


def _split3(x):
    hi = x.astype(BF16)
    r = x - hi.astype(F32)
    mid = r.astype(BF16)
    lo = (r - mid.astype(F32)).astype(BF16)
    return hi, mid, lo


def _dot(a, b):
    return jnp.dot(a, b, preferred_element_type=F32)


def _dot_t(a, bt):
    return lax.dot_general(a, bt, _NT, preferred_element_type=F32)


def _split_dot(parts, m_stacked):
    return _dot(jnp.concatenate(parts, axis=1), m_stacked)


def _rms_forget_kernel(x_ref, g_ref, whi_ref, wlo_ref, b_ref, h_ref, lf_ref):
    h = _rms(x_ref[...], g_ref[...])
    h_ref[...] = h.astype(h_ref.dtype)
    hh, hl = _split2(h)
    z = _dot_t(hh, whi_ref[...]) + _dot_t(hl, whi_ref[...]) + _dot_t(hh, wlo_ref[...])
    z = z[:, :N_Q_HEADS] + b_ref[...]
    lf_ref[...] = -_softplus(-z)


def _rms_forget(x, g, w_hi, w_lo, b, tm):
    rows = x.shape[0]
    const = lambda shape: pl.BlockSpec(shape, lambda i: (0, 0))
    return pl.pallas_call(
        _rms_forget_kernel,
        out_shape=[jax.ShapeDtypeStruct((rows, D_MODEL), BF16),
                   jax.ShapeDtypeStruct((rows, N_Q_HEADS), F32)],
        grid=(rows // tm,),
        in_specs=[pl.BlockSpec((tm, D_MODEL), lambda i: (i, 0)),
                  const((1, D_MODEL)), const((LANES, D_MODEL)), const((LANES, D_MODEL)),
                  const((1, N_Q_HEADS))],
        out_specs=[pl.BlockSpec((tm, D_MODEL), lambda i: (i, 0)),
                   pl.BlockSpec((tm, N_Q_HEADS), lambda i: (i, 0))],
        compiler_params=_params("parallel"),
        name="rms_forget",
    )(x, g, w_hi, w_lo, b)


def _head_norm(acc, g):
    outs = []
    for hh in range(acc.shape[1] // HEAD_DIM):
        xh = acc[:, hh * HEAD_DIM:(hh + 1) * HEAD_DIM]
        outs.append(_rms(xh, g))
    return jnp.concatenate(outs, axis=1)


def _proj_kernel(h_ref, ha_ref, wt_ref, g_ref, o_ref, oa_ref, *, mode):
    wt = wt_ref[...].astype(BF16)
    tm = h_ref.shape[0]
    chunk = min(tm, PROJ_CHUNK)
    pieces = [(h_ref, o_ref, slice(c * chunk, (c + 1) * chunk)) for c in range(tm // chunk)]
    pieces.append((ha_ref, oa_ref, slice(None)))
    acc_next = _dot_t(h_ref[pieces[0][2]], wt)
    for n, (_, dst, rows) in enumerate(pieces):
        acc = acc_next
        if n + 1 < len(pieces):
            src, _, nxt = pieces[n + 1]
            acc_next = _dot_t(src[nxt], wt)
        if mode == "scale":
            acc = acc * Q_SCALE
        elif mode == "norm_scale":
            acc = _head_norm(acc, g_ref[...]) * Q_SCALE
        elif mode == "sigmoid":
            acc = jax.nn.sigmoid(acc)
        dst[rows] = acc.astype(dst.dtype)


def _wt_rows_spec(row0):
    if row0 % TN == 0:
        return pl.BlockSpec((TN, D_MODEL), lambda i, j: (row0 // TN + j, 0))
    assert row0 % SUBLANES == 0
    return pl.BlockSpec((pl.Element(TN), pl.Element(D_MODEL)),
                        lambda i, j: (pl.multiple_of(row0 + j * TN, SUBLANES), 0))


def _proj(h, h_aux, wt, row0, n, g, mode, tm, name):
    rows = h.shape[0]
    out, out_aux = pl.pallas_call(
        functools.partial(_proj_kernel, mode=mode),
        out_shape=[jax.ShapeDtypeStruct((rows, n), BF16),
                   jax.ShapeDtypeStruct((rows // tm, AUX_ROWS, n), BF16)],
        grid=(rows // tm, n // TN),
        in_specs=[pl.BlockSpec((tm, D_MODEL), lambda i, j: (i, 0)),
                  pl.BlockSpec((AUX_ROWS, D_MODEL), lambda i, j: (0, 0)),
                  _wt_rows_spec(row0),
                  pl.BlockSpec((1, HEAD_DIM), lambda i, j: (0, 0))],
        out_specs=[pl.BlockSpec((tm, TN), lambda i, j: (i, j)),
                   pl.BlockSpec((None, AUX_ROWS, TN), lambda i, j: (i, 0, j))],
        compiler_params=_params("parallel", "arbitrary"),
        name=name,
    )(h, h_aux, wt, g)
    return out, out_aux[0]


def _kv_wt_spec():
    sb_kv, fox_kv = W_Q // W_KV, (2 * W_Q + 2 * W_KV) // W_KV
    return pl.BlockSpec((W_KV, D_MODEL), lambda i, j: (sb_kv + j + (fox_kv - sb_kv - 2) * (j // 2), 0))


def _kv_state_kernel(h_ref, ha_ref, wt_ref, gk_ref, *refs, tiles_per_batch):
    state = refs[:4]
    bf_ref, aux_f32_ref, aux_bf_ref, stage, meta_stage, sem, meta_sem = refs[4:]
    i, j = pl.program_id(0), pl.program_id(1)
    n_i = pl.num_programs(0)
    tm = h_ref.shape[0]
    b, it = i // tiles_per_batch, i % tiles_per_batch
    row0 = pl.multiple_of(N_META * N_KV_HEADS + it * tm * N_KV_HEADS, SUBLANES)
    slot = j % 2
    wt = wt_ref[...].astype(BF16)
    finish = lambda a: jnp.where(j == 2, _head_norm(a, gk_ref[...]), a)
    val = finish(_dot_t(h_ref[...], wt))
    val_aux = finish(_dot_t(ha_ref[...], wt))
    bf_ref[...] = val.astype(BF16)
    aux_f32_ref[...] = val_aux
    aux_bf_ref[...] = val_aux.astype(BF16)

    def tile_copy(jj):
        return pltpu.make_async_copy(stage.at[slot], state[jj].at[b, pl.ds(row0, tm * N_KV_HEADS), :],
                                     sem.at[slot])

    def meta_copy(jj):
        return pltpu.make_async_copy(meta_stage.at[jj], state[jj].at[b, pl.ds(0, N_META * N_KV_HEADS), :],
                                     meta_sem.at[jj])

    def per_tensor(fn):
        for jj in range(4):
            pl.when(j == jj)(functools.partial(fn, jj))

    def interleave(dst, val, rows):
        for n in range(N_KV_HEADS):
            dst[pl.ds(n, rows, stride=N_KV_HEADS), :] = val[:rows, n * HEAD_DIM:(n + 1) * HEAD_DIM]

    @pl.when(jnp.logical_or(i > 0, j >= 2))
    def _():
        per_tensor(lambda jj: tile_copy((jj + 2) % 4).wait())

    interleave(stage.at[slot], val, tm)
    per_tensor(lambda jj: tile_copy(jj).start())

    @pl.when(it == 0)
    def _():
        @pl.when(b > 0)
        def _():
            per_tensor(lambda jj: meta_copy(jj).wait())

        interleave(meta_stage.at[j], val_aux, N_META)
        per_tensor(lambda jj: meta_copy(jj).start())

    @pl.when(jnp.logical_and(i == n_i - 1, j == 3))
    def _():
        pltpu.make_async_copy(stage.at[0], state[2].at[b, pl.ds(row0, tm * N_KV_HEADS), :], sem.at[0]).wait()
        tile_copy(3).wait()
        for jj in range(4):
            meta_copy(jj).wait()


def _kv_state_proj(h, h_aux, wt, g_k, tm, batch, seq):
    rows = h.shape[0]
    tiles_per_batch = seq // tm
    state_rows = (N_META + seq) * N_KV_HEADS
    any_spec = pl.BlockSpec(memory_space=pl.ANY)
    aux_spec = pl.BlockSpec((None, AUX_ROWS, W_KV), lambda i, j: (i, 0, j))
    outs = pl.pallas_call(
        functools.partial(_kv_state_kernel, tiles_per_batch=tiles_per_batch),
        out_shape=[jax.ShapeDtypeStruct((batch, state_rows, HEAD_DIM), F32)] * 4
        + [jax.ShapeDtypeStruct((rows, 4 * W_KV), BF16),
           jax.ShapeDtypeStruct((rows // tm, AUX_ROWS, 4 * W_KV), F32),
           jax.ShapeDtypeStruct((rows // tm, AUX_ROWS, 4 * W_KV), BF16)],
        grid=(rows // tm, 4),
        in_specs=[pl.BlockSpec((tm, D_MODEL), lambda i, j: (i, 0)),
                  pl.BlockSpec((AUX_ROWS, D_MODEL), lambda i, j: (0, 0)),
                  _kv_wt_spec(),
                  pl.BlockSpec((1, HEAD_DIM), lambda i, j: (0, 0))],
        out_specs=[any_spec] * 4 + [pl.BlockSpec((tm, W_KV), lambda i, j: (i, j)), aux_spec, aux_spec],
        scratch_shapes=[pltpu.VMEM((2, tm * N_KV_HEADS, HEAD_DIM), F32),
                        pltpu.VMEM((4, N_META * N_KV_HEADS, HEAD_DIM), F32),
                        pltpu.SemaphoreType.DMA((2,)),
                        pltpu.SemaphoreType.DMA((4,))],
        compiler_params=_params("arbitrary", "arbitrary"),
        name="kv_state_proj",
    )(h, h_aux, wt, g_k)
    return outs[:4], outs[4], outs[5][0], outs[6][0]


def _lane_cumsum(x):
    n = x.shape[-1]
    lane = lax.broadcasted_iota(jnp.int32, x.shape, x.ndim - 1)
    k = 1
    while k < n:
        x = x + jnp.where(lane >= k, pltpu.roll(x, k, axis=x.ndim - 1), 0.0)
        k *= 2
    return x


def _cum_kernel(meta_ref, main_ref, cmeta_ref, cmain_ref):
    cm = _lane_cumsum(meta_ref[...])
    cmeta_ref[...] = cm * LOG2E
    cmain_ref[...] = (_lane_cumsum(main_ref[...]) + cm[:, LANES - 1:LANES]) * LOG2E


def _cum_forget(lf_meta_t, lf_main_t):
    b, _, s = lf_main_t.shape
    return pl.pallas_call(
        _cum_kernel,
        out_shape=[jax.ShapeDtypeStruct((b, N_Q_HEADS, LANES), F32),
                   jax.ShapeDtypeStruct((b, N_Q_HEADS, s), F32)],
        grid=(b,),
        in_specs=[pl.BlockSpec((N_Q_HEADS, LANES), lambda i: (0, 0)),
                  pl.BlockSpec((None, N_Q_HEADS, s), lambda i: (i, 0, 0))],
        out_specs=[pl.BlockSpec((None, N_Q_HEADS, LANES), lambda i: (i, 0, 0)),
                   pl.BlockSpec((None, N_Q_HEADS, s), lambda i: (i, 0, 0))],
        compiler_params=_params("parallel"),
        name="cum_forget",
    )(lf_meta_t, lf_main_t)


def _stack_heads(q):
    return jnp.concatenate([q[:, g * HEAD_DIM:(g + 1) * HEAD_DIM] for g in range(GROUP)], axis=0)


SB_CHUNK = 512
FOX_CHUNK = 512


def _chunk_visible(rel, row0, rows, t, tk):
    col = lax.broadcasted_iota(jnp.int32, (rows, tk), 1)
    row = (lax.broadcasted_iota(jnp.int32, (rows, tk), 0) + row0) % t
    return rel(col, row)


def _sb_block(qs, kb, vb, uo, rel, r_ref, acc_ref):
    r, tk = qs.shape[0], kb.shape[0]
    t = r // GROUP
    cr = min(r, SB_CHUNK)
    score = lambda c: lax.dot_general(qs[c * cr:(c + 1) * cr], kb, _NT, preferred_element_type=F32)
    z_next = score(0)
    for c in range(r // cr):
        rows = slice(c * cr, (c + 1) * cr)
        z = z_next
        if c + 1 < r // cr:
            z_next = score(c + 1)
        sp = _softplus2(z)
        if rel is not None:
            vis = _chunk_visible(rel, c * cr, cr, t, tk)
            sp = jnp.where(vis, sp, 0.0)
        ct = _split_dot(_split2(sp), uo)
        r_prev = r_ref[rows]
        a = jnp.exp2(z - ct[:, :tk] - r_prev)
        if rel is not None:
            a = jnp.where(vis, a, 0.0)
        acc_ref[rows] += _dot(a.astype(BF16), vb)
        r_ref[rows] = r_prev + ct[:, tk:]


def _sb_window(qs, kb, vb, uo, r_ref, acc_ref):
    r = qs.shape[0]
    t = r // GROUP
    nb = kb.shape[0] // t
    z = _dot_t(qs, kb)
    sp = _softplus2(z)
    vis = _chunk_visible(lambda c, row: c < row, 0, r, t, t)
    newer = None
    weights = [None] * nb
    for b in reversed(range(nb)):
        cols = slice(b * t, (b + 1) * t)
        diag = b == nb - 1
        spb = jnp.where(vis, sp[:, cols], 0.0) if diag else sp[:, cols]
        ct = _split_dot(_split2(spb), uo)
        arg = z[:, cols] - ct[:, :t]
        a = jnp.exp2(arg if newer is None else arg - newer)
        weights[b] = (jnp.where(vis, a, 0.0) if diag else a).astype(BF16)
        newer = ct[:, t:] if newer is None else newer + ct[:, t:]
    acc_ref[...] = _dot(jnp.concatenate(weights, axis=1), vb)
    r_ref[...] = newer


def _fox_block(qs, kb, vb, ckb, rel, m_ref, acc_ref, own_keys=False):
    r, tk = qs.shape[0], kb.shape[0]
    t = r // GROUP
    if own_keys:
        assert tk == t
        chains = [(g * t + part * (t // 2), t // 2, (part + 1) * (tk // 2))
                  for g in range(GROUP) for part in range(2)]
    else:
        cr = min(r, FOX_CHUNK)
        chains = [(c * cr, cr, tk) for c in range(r // cr)]
    v1 = jnp.concatenate([vb, jnp.ones((tk, LANES), BF16)], axis=1)
    score = lambda row0, n, kc: lax.dot_general(qs[row0:row0 + n], kb[:kc], _NT, preferred_element_type=F32)
    z_next = score(*chains[0])
    for c, (row0, n, kc) in enumerate(chains):
        rows = slice(row0, row0 + n)
        hc = max(n // t, 1)
        bias = ckb[row0 // t:row0 // t + hc, :kc]
        z = (z_next.reshape(hc, n // hc, kc) - bias[:, None, :]).reshape(n, kc)
        if c + 1 < len(chains):
            z_next = score(*chains[c + 1])
        if rel is not None:
            z = jnp.where(_chunk_visible(rel, row0, n, t, kc), z, NEG_BIG)
        m_prev = m_ref[rows]
        m_new = jnp.maximum(m_prev, jnp.max(z, axis=1, keepdims=True))
        alpha = jnp.exp2(m_prev - m_new)
        p = jnp.exp2(z - jnp.tile(m_new, (1, kc // LANES)))
        acc_ref[rows] = acc_ref[rows] * jnp.tile(alpha, (1, 2)) + _dot(p.astype(BF16), v1[:kc])
        m_ref[rows] = m_new


SB_T = 128
SB_WIN = 3
SB_SUB = 8


def _store_heads(o_ref, o, t, gate_ref, add_ref=None, row0=0):
    rows = slice(row0, row0 + t)
    for g in range(GROUP):
        cols = slice(g * HEAD_DIM, (g + 1) * HEAD_DIM)
        val = o[g * t:(g + 1) * t] * gate_ref[rows, cols].astype(F32)
        if add_ref is not None:
            val = val + add_ref[rows, cols].astype(F32)
        o_ref[rows, cols] = val.astype(o_ref.dtype)


def _sb_prompt_kernel(q_ref, k_ref, v_ref, mk_ref, mv_ref, uo_ref, gate_ref, o_ref, r_ref, acc_ref):
    i = pl.program_id(2)
    t = SB_T
    uo = uo_ref[...]
    qs = [_stack_heads(q_ref[sub * t:(sub + 1) * t]) for sub in range(SB_SUB)]
    tile = lambda sub: i * SB_SUB + sub

    def first_block_only(sub):
        r_ref[sub] = jnp.zeros(r_ref.shape[1:], F32)
        acc_ref[sub] = jnp.zeros(acc_ref.shape[1:], F32)
        s = pl.multiple_of(tile(sub) * t, t)
        _sb_block(qs[sub], k_ref[pl.ds(s, t), :], v_ref[pl.ds(s, t), :], uo, lambda c, r: c < r,
                  r_ref.at[sub], acc_ref.at[sub])

    assert SB_SUB >= SB_WIN - 1

    @pl.when(i > 0)
    def _():
        for sub in range(SB_SUB):
            s = pl.multiple_of((tile(sub) - (SB_WIN - 1)) * t, t)
            _sb_window(qs[sub], k_ref[pl.ds(s, SB_WIN * t), :], v_ref[pl.ds(s, SB_WIN * t), :], uo,
                       r_ref.at[sub], acc_ref.at[sub])

    @pl.when(i == 0)
    def _():
        for sub in range(SB_SUB):
            if sub >= SB_WIN - 1:
                _sb_window(qs[sub], k_ref[pl.ds((sub - (SB_WIN - 1)) * t, SB_WIN * t), :],
                           v_ref[pl.ds((sub - (SB_WIN - 1)) * t, SB_WIN * t), :], uo,
                           r_ref.at[sub], acc_ref.at[sub])
            else:
                first_block_only(sub)

    for sub in range(SB_SUB):
        def visit(kb, vb, rel, sub=sub):
            _sb_block(qs[sub], kb, vb, uo, rel, r_ref.at[sub], acc_ref.at[sub])
            return jnp.min(r_ref[sub])

        def cond(c):
            return jnp.logical_and(c[0] >= 0, c[1] < SB_DEAD)

        def body(c, visit=visit):
            s = pl.multiple_of(c[0] * t, t)
            return c[0] - 1, visit(k_ref[pl.ds(s, t), :], v_ref[pl.ds(s, t), :], None)

        covered = jnp.where(jnp.logical_or(i > 0, sub >= SB_WIN - 1), SB_WIN, 1)
        _, mn = lax.while_loop(cond, body, (tile(sub) - covered, jnp.min(r_ref[sub])))

        @pl.when(mn < SB_DEAD)
        def _(visit=visit):
            visit(mk_ref[...], mv_ref[...], lambda c, r: c < N_META)

        _store_heads(o_ref, acc_ref[sub], t, gate_ref, row0=sub * t)


def _kv_head_spec(seq, tensor):
    return pl.BlockSpec((seq, HEAD_DIM), lambda b, n, i: (b, tensor * N_KV_HEADS + n))


def _sb_prompt(q, kv, mk, mv, uo, gates, batch, seq):
    t = SB_T * SB_SUB
    nb = seq // t
    tile = pl.BlockSpec((t, GROUP * HEAD_DIM), lambda b, n, i: (b * nb + i, n))
    return pl.pallas_call(
        _sb_prompt_kernel,
        out_shape=jax.ShapeDtypeStruct((batch * seq, W_Q), BF16),
        grid=(batch, N_KV_HEADS, nb),
        in_specs=[tile,
                  _kv_head_spec(seq, 0),
                  _kv_head_spec(seq, 1),
                  pl.BlockSpec((LANES, HEAD_DIM), lambda b, n, i: (0, n)),
                  pl.BlockSpec((LANES, HEAD_DIM), lambda b, n, i: (0, n)),
                  pl.BlockSpec(uo.shape, lambda b, n, i: (0, 0)),
                  tile],
        out_specs=tile,
        scratch_shapes=[pltpu.VMEM((SB_SUB, GROUP * SB_T, SB_T), F32),
                        pltpu.VMEM((SB_SUB, GROUP * SB_T, HEAD_DIM), F32)],
        compiler_params=_params("parallel", "parallel", "arbitrary"),
        name="sb_prompt",
    )(q, kv, kv, mk, mv, uo, gates)


FOX_T = 512
FOX_TK = 1024


def _fox_prompt_kernel(q_ref, k_ref, v_ref, ck_ref, mk_ref, mv_ref, mck_ref, gate_ref, sb_ref,
                       o_ref, m_ref, acc_ref):
    i = pl.program_id(2)
    t = FOX_T
    qs = _stack_heads(q_ref[...])
    m_ref[...] = jnp.full_like(m_ref, NEG_BIG)
    acc_ref[...] = jnp.zeros_like(acc_ref)

    def visit(s, width, rel, own_keys=False):
        _fox_block(qs, k_ref[pl.ds(s, width), :], v_ref[pl.ds(s, width), :], ck_ref[:, pl.ds(s, width)],
                   rel, m_ref, acc_ref, own_keys)

    visit(pl.multiple_of(i * t, t), t, lambda c, r: c <= r, own_keys=True)

    per_visit = FOX_TK // t

    def body(j, carry):
        visit(pl.multiple_of(j * FOX_TK, FOX_TK), FOX_TK, None)
        return carry

    lax.fori_loop(0, i // per_visit, body, 0)

    for left in range(per_visit - 1):
        @pl.when(i % per_visit > left)
        def _(left=left):
            visit(pl.multiple_of((i - i % per_visit + left) * t, t), t, None)

    _fox_block(qs, mk_ref[...], mv_ref[...], mck_ref[...],
               lambda c, r: c < N_META, m_ref, acc_ref)

    acc = acc_ref[...]
    _store_heads(o_ref, acc[:, :HEAD_DIM] / acc[:, HEAD_DIM:], t, gate_ref, sb_ref)


def _fox_prompt(q, kv, ck, mk, mv, mck, gates, sb_part, batch, seq):
    t = FOX_T
    nb = seq // t
    tile = pl.BlockSpec((t, GROUP * HEAD_DIM), lambda b, n, i: (b * nb + i, n))
    return pl.pallas_call(
        _fox_prompt_kernel,
        out_shape=jax.ShapeDtypeStruct((batch * seq, W_Q), BF16),
        grid=(batch, N_KV_HEADS, nb),
        in_specs=[tile,
                  _kv_head_spec(seq, 2),
                  _kv_head_spec(seq, 3),
                  pl.BlockSpec((None, None, GROUP, seq), lambda b, n, i: (b, n, 0, 0)),
                  pl.BlockSpec((LANES, HEAD_DIM), lambda b, n, i: (0, n)),
                  pl.BlockSpec((LANES, HEAD_DIM), lambda b, n, i: (0, n)),
                  pl.BlockSpec((None, None, GROUP, LANES), lambda b, n, i: (b, n, 0, 0)),
                  pl.BlockSpec((t, GROUP * HEAD_DIM), lambda b, n, i: (b * nb + i, N_KV_HEADS + n)),
                  tile],
        out_specs=tile,
        scratch_shapes=[pltpu.VMEM((GROUP * t, LANES), F32), pltpu.VMEM((GROUP * t, 2 * LANES), F32)],
        compiler_params=_params("parallel", "parallel", "arbitrary"),
        name="fox_prompt",
    )(q, kv, kv, ck, mk, mv, mck, gates, sb_part)


def _meta_attn_kernel(sq_ref, sk_ref, sv_ref, fq_ref, fk_ref, fv_ref, ck_ref, uo_ref, gsb_ref, gfx_ref,
                      o_ref, m_ref, acc_ref, r_ref, sacc_ref):
    t = N_META
    valid = lambda c: c < N_META
    r_ref[...] = jnp.zeros_like(r_ref)
    sacc_ref[...] = jnp.zeros_like(sacc_ref)
    _sb_block(_stack_heads(sq_ref[...]), sk_ref[...], sv_ref[...], uo_ref[...],
              lambda c, r: jnp.logical_and(c < r, valid(c)), r_ref, sacc_ref)
    o = sacc_ref[...]
    m_ref[...] = jnp.full_like(m_ref, NEG_BIG)
    acc_ref[...] = jnp.zeros_like(acc_ref)
    _fox_block(_stack_heads(fq_ref[...]), fk_ref[...], fv_ref[...], ck_ref[...],
               lambda c, r: jnp.logical_and(c <= r, valid(c)), m_ref, acc_ref)
    acc = acc_ref[...]
    f = acc[:, :HEAD_DIM] / acc[:, HEAD_DIM:]
    for g in range(GROUP):
        cols = slice(g * HEAD_DIM, (g + 1) * HEAD_DIM)
        merged = (gsb_ref[:, cols].astype(F32) * o[g * t:(g + 1) * t]
                  + gfx_ref[:, cols].astype(F32) * f[g * t:(g + 1) * t])
        o_ref[:, cols] = merged.astype(o_ref.dtype)


def _meta_attn(sq, sk, sv, fq, fk, fv, ck, uo, gates):
    qspec = pl.BlockSpec((N_META, GROUP * HEAD_DIM), lambda n: (0, n))
    kspec = pl.BlockSpec((LANES, HEAD_DIM), lambda n: (0, n))
    return pl.pallas_call(
        _meta_attn_kernel,
        out_shape=jax.ShapeDtypeStruct((N_META, W_Q), BF16),
        grid=(N_KV_HEADS,),
        in_specs=[qspec, kspec, kspec, qspec, kspec, kspec,
                  pl.BlockSpec((None, GROUP, LANES), lambda n: (n, 0, 0)),
                  pl.BlockSpec(uo.shape, lambda n: (0, 0)),
                  qspec,
                  pl.BlockSpec((N_META, GROUP * HEAD_DIM), lambda n: (0, N_KV_HEADS + n))],
        out_specs=qspec,
        scratch_shapes=[pltpu.VMEM((GROUP * N_META, LANES), F32),
                        pltpu.VMEM((GROUP * N_META, 2 * LANES), F32),
                        pltpu.VMEM((GROUP * N_META, LANES), F32),
                        pltpu.VMEM((GROUP * N_META, HEAD_DIM), F32)],
        compiler_params=_params("parallel"),
        name="meta_attn",
    )(sq, sk, sv, fq, fk, fv, ck, uo, gates, gates)


PAGE_COLS = PAGE_SIZE * N_KV_HEADS
DEC_G = 16


def _own_head(shape):
    row = lax.broadcasted_iota(jnp.int32, shape, 0)
    col = lax.broadcasted_iota(jnp.int32, shape, 1)
    return (col % N_KV_HEADS) == (row // GROUP)


def _sb_decode_kernel(pt_ref, q_ref, gate_ref, ck_hbm, cv_hbm, uo_ref, o_ref, kbuf, vbuf, sem, r_ref, acc_ref):
    b = pl.program_id(0)
    n_pages = pt_ref.shape[1]
    q = q_ref[...]
    uo = uo_ref[...]
    own = _own_head((N_Q_HEADS, PAGE_COLS))

    def copies(p, slot):
        pid = pt_ref[b, p]
        return [pltpu.make_async_copy(ck_hbm.at[pid], kbuf.at[slot], sem.at[0, slot]),
                pltpu.make_async_copy(cv_hbm.at[pid], vbuf.at[slot], sem.at[1, slot])]

    r_ref[...] = jnp.zeros_like(r_ref)
    acc_ref[...] = jnp.zeros_like(acc_ref)
    for c in copies(n_pages - 1, (n_pages - 1) % 2):
        c.start()

    def cond(c):
        return jnp.logical_and(c[0] >= 0, c[1] < SB_DEAD)

    def body(c):
        p = c[0]
        slot = p % 2
        for cp in copies(p, slot):
            cp.wait()

        @pl.when(p > 0)
        def _():
            for cp in copies(p - 1, 1 - slot):
                cp.start()

        z = lax.dot_general(q, kbuf[slot].astype(BF16), _NT, preferred_element_type=F32)
        sp = jnp.where(own, _softplus2(z), 0.0)
        ct = _split_dot(_split2(sp), uo)
        r = r_ref[...]
        a = jnp.where(own, jnp.exp2(z - ct[:, :PAGE_COLS] - jnp.tile(r, (1, N_KV_HEADS))), 0.0)
        acc_ref[...] += _dot(a.astype(BF16), vbuf[slot].astype(BF16))
        r_new = r + ct[:, PAGE_COLS:]
        r_ref[...] = r_new
        return p - 1, jnp.min(r_new)

    p_end, _ = lax.while_loop(cond, body, (jnp.int32(n_pages - 1), jnp.float32(0.0)))

    @pl.when(p_end >= 0)
    def _():
        for cp in copies(p_end, p_end % 2):
            cp.wait()

    o_ref[...] = (acc_ref[...] * gate_ref[...].astype(F32)).astype(o_ref.dtype)


def _sb_decode(page_table, q, gate, cache_k, cache_v, uo):
    nb = q.shape[0]
    any_spec = pl.BlockSpec(memory_space=pl.ANY)
    head_spec = pl.BlockSpec((None, N_Q_HEADS, HEAD_DIM), lambda b, pt: (b, 0, 0))
    return pl.pallas_call(
        _sb_decode_kernel,
        out_shape=jax.ShapeDtypeStruct((nb, N_Q_HEADS, HEAD_DIM), BF16),
        grid_spec=pltpu.PrefetchScalarGridSpec(
            num_scalar_prefetch=1,
            grid=(nb,),
            in_specs=[head_spec, head_spec, any_spec, any_spec,
                      pl.BlockSpec(uo.shape, lambda b, pt: (0, 0))],
            out_specs=head_spec,
            scratch_shapes=[pltpu.VMEM((2, PAGE_COLS, HEAD_DIM), F32),
                            pltpu.VMEM((2, PAGE_COLS, HEAD_DIM), F32),
                            pltpu.SemaphoreType.DMA((2, 2)),
                            pltpu.VMEM((N_Q_HEADS, LANES), F32),
                            pltpu.VMEM((N_Q_HEADS, HEAD_DIM), F32)]),
        compiler_params=_params("arbitrary"),
        name="sb_decode",
    )(page_table, q, gate, cache_k, cache_v, uo)


def _fox_decode_kernel(pt_ref, q_ref, kn_ref, vn_ref, lfn_ref, gate_ref, sb_ref, ck_hbm, cv_hbm, lf_hbm,
                       us_ref, o_ref, kbuf, vbuf, lbuf, sem, m_ref, acc_ref, d_ref):
    n_seq, n_pages = pt_ref.shape
    n_grp = n_pages // DEC_G
    total = n_seq * n_grp
    us = us_ref[...]
    bias = jnp.where(_own_head((N_Q_HEADS, PAGE_COLS)), 0.0, NEG_BIG)

    def copies(t, slot):
        b = t // n_grp
        newest = n_pages - 1 - (t % n_grp) * DEC_G
        out = []
        for g in range(DEC_G):
            pid = pt_ref[b, newest - g]
            out += [pltpu.make_async_copy(ck_hbm.at[pid], kbuf.at[slot, g], sem.at[0, slot]),
                    pltpu.make_async_copy(cv_hbm.at[pid], vbuf.at[slot, g], sem.at[1, slot]),
                    pltpu.make_async_copy(lf_hbm.at[pid], lbuf.at[slot, g], sem.at[2, slot])]
        return out

    for cp in copies(0, 0):
        cp.start()

    def body(t, carry):
        slot = t % 2
        b = t // n_grp
        gi = t % n_grp

        @pl.when(t + 1 < total)
        def _():
            for cp in copies(t + 1, 1 - slot):
                cp.start()

        q = q_ref[b]

        @pl.when(gi == 0)
        def _():
            z_self = jnp.sum(q.astype(F32) * kn_ref[b].astype(F32), axis=1, keepdims=True)
            m_ref[...] = jnp.broadcast_to(z_self, m_ref.shape)
            acc_ref[...] = jnp.concatenate([vn_ref[b].astype(F32), jnp.ones((N_Q_HEADS, LANES), F32)], axis=1)
            d_ref[...] = jnp.broadcast_to(lfn_ref[b] * LOG2E, d_ref.shape)

        for cp in copies(t, slot):
            cp.wait()

        dt = _split_dot(_split3(lbuf[slot].reshape(DEC_G * N_Q_HEADS, PAGE_SIZE)), us) * LOG2E
        d = d_ref[...]
        zs = []
        for g in range(DEC_G):
            dg = dt[g * N_Q_HEADS:(g + 1) * N_Q_HEADS]
            z = lax.dot_general(q, kbuf[slot, g].astype(BF16), _NT, preferred_element_type=F32)
            zs.append(z + dg[:, :PAGE_COLS] + jnp.tile(d, (1, N_KV_HEADS)) + bias)
            d = d + dg[:, PAGE_COLS:]
        d_ref[...] = d
        z = jnp.concatenate(zs, axis=1)
        m_prev = m_ref[...]
        m_new = jnp.maximum(m_prev, jnp.max(z, axis=1, keepdims=True))
        alpha = jnp.exp2(m_prev - m_new)
        p = jnp.exp2(z - jnp.tile(m_new, (1, z.shape[1] // LANES)))
        lsum = jnp.sum(p, axis=1, keepdims=True)
        pb = p.astype(BF16)
        pv = _dot(pb[:, :PAGE_COLS], vbuf[slot, 0].astype(BF16))
        for g in range(1, DEC_G):
            pv += _dot(pb[:, g * PAGE_COLS:(g + 1) * PAGE_COLS], vbuf[slot, g].astype(BF16))
        acc = acc_ref[...] * jnp.tile(alpha, (1, 2)) + jnp.concatenate(
            [pv, jnp.broadcast_to(lsum, pv.shape)], axis=1)
        acc_ref[...] = acc
        m_ref[...] = m_new

        @pl.when(gi == n_grp - 1)
        def _():
            merged = acc[:, :HEAD_DIM] / acc[:, HEAD_DIM:] * gate_ref[b].astype(F32) + sb_ref[b].astype(F32)
            o_ref[b] = merged.astype(o_ref.dtype)

        return carry

    lax.fori_loop(0, total, body, 0)


def _fox_decode(page_table, q, k_new, v_new, lf_new, gate, sb_part, cache_k, cache_v, cache_lf_t, us):
    nb = q.shape[0]
    assert page_table.shape[1] % DEC_G == 0
    any_spec = pl.BlockSpec(memory_space=pl.ANY)
    head_spec = pl.BlockSpec((nb, N_Q_HEADS, HEAD_DIM), lambda i, pt: (0, 0, 0))
    return pl.pallas_call(
        _fox_decode_kernel,
        out_shape=jax.ShapeDtypeStruct((nb, N_Q_HEADS, HEAD_DIM), BF16),
        grid_spec=pltpu.PrefetchScalarGridSpec(
            num_scalar_prefetch=1,
            grid=(1,),
            in_specs=[head_spec, head_spec, head_spec,
                      pl.BlockSpec((nb, N_Q_HEADS, 1), lambda i, pt: (0, 0, 0)),
                      head_spec, head_spec,
                      any_spec, any_spec, any_spec,
                      pl.BlockSpec(us.shape, lambda i, pt: (0, 0))],
            out_specs=head_spec,
            scratch_shapes=[pltpu.VMEM((2, DEC_G, PAGE_COLS, HEAD_DIM), F32),
                            pltpu.VMEM((2, DEC_G, PAGE_COLS, HEAD_DIM), F32),
                            pltpu.VMEM((2, DEC_G, N_Q_HEADS, PAGE_SIZE), F32),
                            pltpu.SemaphoreType.DMA((3, 2)),
                            pltpu.VMEM((N_Q_HEADS, LANES), F32),
                            pltpu.VMEM((N_Q_HEADS, 2 * LANES), F32),
                            pltpu.VMEM((N_Q_HEADS, LANES), F32)]),
        compiler_params=_params("arbitrary"),
        name="fox_decode",
    )(page_table, q, k_new, v_new, lf_new, gate, sb_part, cache_k, cache_v, cache_lf_t, us)


def _out_proj_kernel(m_ref, ma_ref, w_ref, x_ref, xa_ref, y_ref, ya_ref):
    w = w_ref[...]
    y_ref[...] = x_ref[...] + _dot(m_ref[...], w)
    ya_ref[...] = xa_ref[...] + _dot(ma_ref[...], w)


def _out_proj(merged, merged_aux, w, x, x_aux, tm):
    rows = x.shape[0]
    y, y_aux = pl.pallas_call(
        _out_proj_kernel,
        out_shape=[jax.ShapeDtypeStruct((rows, D_MODEL), F32),
                   jax.ShapeDtypeStruct((rows // tm, AUX_ROWS, D_MODEL), F32)],
        grid=(rows // tm, D_MODEL // TN),
        in_specs=[pl.BlockSpec((tm, W_Q), lambda i, j: (i, 0)),
                  pl.BlockSpec((AUX_ROWS, W_Q), lambda i, j: (0, 0)),
                  pl.BlockSpec((W_Q, TN), lambda i, j: (0, j)),
                  pl.BlockSpec((tm, TN), lambda i, j: (i, j)),
                  pl.BlockSpec((AUX_ROWS, TN), lambda i, j: (0, j))],
        out_specs=[pl.BlockSpec((tm, TN), lambda i, j: (i, j)),
                   pl.BlockSpec((None, AUX_ROWS, TN), lambda i, j: (i, 0, j))],
        compiler_params=_params("parallel", "arbitrary"),
        name="out_proj",
    )(merged, merged_aux, w, x, x_aux)
    return y, y_aux[0]


def _ffn_kernel(y_ref, ya_ref, g_ref, wu_ref, wd_ref, o_ref, oa_ref, h_ref, ha_ref):
    i, k = pl.program_id(0), pl.program_id(1)

    def start(src, h_dst, o_dst):
        y = src[...]
        h_dst[...] = _rms(y, g_ref[...]).astype(BF16)
        o_dst[...] = y

    def step(h_src, o_dst):
        u = jnp.square(jnp.maximum(_dot(h_src[...], wu_ref[...]), 0.0))
        o_dst[...] += _dot(u.astype(BF16), wd_ref[...])

    @pl.when(k == 0)
    def _():
        start(y_ref, h_ref, o_ref)

    @pl.when(jnp.logical_and(i == 0, k == 0))
    def _():
        start(ya_ref, ha_ref, oa_ref)

    step(h_ref, o_ref)

    @pl.when(i == 0)
    def _():
        step(ha_ref, oa_ref)


def _ffn(y, y_aux, g, w_up, w_down, tm):
    rows = y.shape[0]
    aux_spec = pl.BlockSpec((AUX_ROWS, D_MODEL), lambda i, k: (0, 0))
    return pl.pallas_call(
        _ffn_kernel,
        out_shape=[jax.ShapeDtypeStruct((rows, D_MODEL), F32), jax.ShapeDtypeStruct((AUX_ROWS, D_MODEL), F32)],
        grid=(rows // tm, D_FF // TF),
        in_specs=[pl.BlockSpec((tm, D_MODEL), lambda i, k: (i, 0)),
                  aux_spec,
                  pl.BlockSpec((1, D_MODEL), lambda i, k: (0, 0)),
                  pl.BlockSpec((D_MODEL, TF), lambda i, k: (0, k)),
                  pl.BlockSpec((TF, D_MODEL), lambda i, k: (k, 0))],
        out_specs=[pl.BlockSpec((tm, D_MODEL), lambda i, k: (i, 0)), aux_spec],
        scratch_shapes=[pltpu.VMEM((tm, D_MODEL), BF16), pltpu.VMEM((AUX_ROWS, D_MODEL), BF16)],
        compiler_params=pltpu.CompilerParams(dimension_semantics=("arbitrary", "arbitrary"),
                                             vmem_limit_bytes=FFN_VMEM_LIMIT),
        name="ffn",
    )(y, y_aux, g, w_up, w_down)


def _project_rows(x, x_aux, wts, batch, seq):
    norm = lambda rows, tm: _rms_forget(rows, wts["g_mix"], wts["w_fl_hi"], wts["w_fl_lo"], wts["b_f"], tm)
    h, lf = norm(x, TM_MAIN // 2)
    h_aux, lf_aux = norm(x_aux, AUX_ROWS)
    wt = wts["w_in_t"]
    sq = _proj(h, h_aux, wt, 0, W_Q, wts["g_q"], "scale", TM_PROJ, "proj_sq")
    fq = _proj(h, h_aux, wt, W_Q + 2 * W_KV, W_Q, wts["g_q"], "norm_scale", TM_PROJ, "proj_fq")
    gates = _proj(h, h_aux, wt, FORGET_OFFSET + N_Q_HEADS, 2 * D_MODEL, wts["g_q"], "sigmoid", TM_PROJ,
                  "proj_gates")
    kv = _kv_state_proj(h, h_aux, wt, wts["g_k"], TM_PROJ, batch, seq)
    return sq, fq, gates, kv, (lf, lf_aux)


def _finish_rows(x, x_aux, merged, merged_aux, wts):
    y1, y1_aux = _out_proj(merged, merged_aux, wts["w_out"], x, x_aux, TM_PROJ)
    return _ffn(y1, y1_aux, wts["g_ffn"], wts["w_up"], wts["w_down"], TM_MAIN)


def kernel(x_prompt, x_sample, cache_sb_k, cache_sb_v, cache_fox_k, cache_fox_v, cache_fox_logf,
           page_table, meta_tokens, g_mix, w_in, b_forget, g_q, g_k, w_out, g_ffn, w_up, w_down):
    batch, seq, _ = x_prompt.shape
    dec = x_sample.shape[0]
    assert w_in.shape[0] == 1 and x_sample.shape[1] == 1 and N_META + dec <= AUX_ROWS
    assert seq % TM_PROJ == 0 and seq % (SB_T * SB_SUB) == 0 and seq % FOX_T == 0
    pool = cache_sb_k.shape[1]

    w_in_t = jnp.swapaxes(w_in[0], 0, 1)
    w_fl = jnp.pad(w_in_t[FORGET_OFFSET:FORGET_OFFSET + N_Q_HEADS], ((0, LANES - N_Q_HEADS), (0, 0)))
    w_fl_hi = w_fl.astype(BF16)
    wts = {
        "g_mix": g_mix, "g_q": g_q, "g_k": g_k, "g_ffn": g_ffn, "b_f": b_forget,
        "w_in_t": w_in_t,
        "w_fl_hi": w_fl_hi, "w_fl_lo": (w_fl - w_fl_hi.astype(F32)).astype(BF16),
        "w_out": w_out[0].astype(BF16), "w_up": w_up[0].astype(BF16), "w_down": w_down[0].astype(BF16),
    }

    x_main = x_prompt.reshape(batch * seq, D_MODEL)
    x_aux = jnp.concatenate([meta_tokens, x_sample.reshape(dec, D_MODEL),
                             jnp.zeros((AUX_ROWS - N_META - dec, D_MODEL), F32)], axis=0)

    (sq_m, sq_a), (fq_m, fq_a), (gates_m, gates_a), kv, (lf_m, lf_a) = _project_rows(
        x_main, x_aux, wts, batch, seq)
    (sk_st, sv_st, fk_st, fv_st), kvb_m, kv_a, kvb_a = kv
    four = lambda a: [a[:, t * W_KV:(t + 1) * W_KV] for t in range(4)]
    sk_a, sv_a, fk_a, fv_a = four(kv_a)
    skb_a, svb_a, fkb_a, fvb_a = four(kvb_a)

    lf_meta_t = jnp.pad(lf_a[:N_META].T, ((0, 0), (0, LANES - N_META)))
    lf_main_t = jnp.swapaxes(lf_m.reshape(batch, seq, N_Q_HEADS), 1, 2)
    ck_meta, ck_main = _cum_forget(lf_meta_t, lf_main_t)
    ck_meta = ck_meta.reshape(batch, N_KV_HEADS, GROUP, LANES)
    ck_main = ck_main.reshape(batch, N_KV_HEADS, GROUP, seq)

    pad_keys = lambda a: jnp.pad(a[:N_META], ((0, LANES - N_META), (0, 0)))
    mk_sb, mv_sb, mk_fx, mv_fx = pad_keys(skb_a), pad_keys(svb_a), pad_keys(fkb_a), pad_keys(fvb_a)

    j_idx = lax.broadcasted_iota(jnp.int32, (LANES, LANES), 0)
    s_idx = lax.broadcasted_iota(jnp.int32, (LANES, LANES), 1)
    ones = jnp.ones((LANES, LANES), BF16)
    uo = jnp.concatenate([(j_idx >= s_idx).astype(BF16), ones], axis=1)
    uo = jnp.concatenate([uo] * 2, axis=0)
    c_src = lax.broadcasted_iota(jnp.int32, (PAGE_COLS, PAGE_COLS), 0) // N_KV_HEADS
    c_dst = lax.broadcasted_iota(jnp.int32, (PAGE_COLS, PAGE_COLS), 1) // N_KV_HEADS
    uo_page = jnp.concatenate([(c_src >= c_dst).astype(BF16), jnp.ones((PAGE_COLS, LANES), BF16)], axis=1)
    uo_page = jnp.concatenate([uo_page] * 2, axis=0)
    j_key = lax.broadcasted_iota(jnp.int32, (PAGE_SIZE, PAGE_COLS), 0)
    c_key = lax.broadcasted_iota(jnp.int32, (PAGE_SIZE, PAGE_COLS), 1) // N_KV_HEADS
    us_page = jnp.concatenate([(j_key > c_key).astype(BF16), ones], axis=1)
    us_page = jnp.concatenate([us_page] * 3, axis=0)

    sb_m = _sb_prompt(sq_m, kvb_m, mk_sb, mv_sb, uo, gates_m, batch, seq)
    merged_m = _fox_prompt(fq_m, kvb_m, ck_main, mk_fx, mv_fx, ck_meta, gates_m, sb_m, batch, seq)
    merged_meta = _meta_attn(sq_a[:N_META], mk_sb, mv_sb, fq_a[:N_META], mk_fx, mv_fx,
                             ck_meta[0], uo, gates_a[:N_META])

    heads = lambda a: a[N_META:N_META + dec].reshape(dec, N_Q_HEADS, HEAD_DIM)
    kv_heads = lambda a: jnp.repeat(a[N_META:N_META + dec].reshape(dec, N_KV_HEADS, HEAD_DIM), GROUP, axis=1)
    pages = lambda c: c[0].reshape(pool, PAGE_COLS, HEAD_DIM)
    sb_dec = _sb_decode(page_table, heads(sq_a), heads(gates_a[:, :W_Q]),
                        pages(cache_sb_k), pages(cache_sb_v), uo_page)
    merged_dec = _fox_decode(page_table, heads(fq_a), kv_heads(fkb_a), kv_heads(fvb_a),
                             lf_a[N_META:N_META + dec].reshape(dec, N_Q_HEADS, 1),
                             heads(gates_a[:, W_Q:]), sb_dec,
                             pages(cache_fox_k), pages(cache_fox_v),
                             jnp.swapaxes(cache_fox_logf[0], 1, 2), us_page)

    tail = jnp.zeros((AUX_ROWS - N_META - dec, W_Q), BF16)
    merged_a = jnp.concatenate([merged_meta, merged_dec.reshape(dec, W_Q), tail], axis=0)

    y_main, y_aux = _finish_rows(x_main, x_aux, merged_m, merged_a, wts)

    def prompt_state(a_aux, a_main, tail_shape):
        meta = jnp.broadcast_to(a_aux[None, :N_META], (batch, N_META, a_aux.shape[1]))
        full = jnp.concatenate([meta, a_main.reshape(batch, seq, a_aux.shape[1])], axis=1)
        return full.reshape((1, batch, seq + N_META) + tail_shape)

    kv_shape = (N_KV_HEADS, HEAD_DIM)
    kv_state = lambda st: st.reshape((1, batch, seq + N_META) + kv_shape)
    sample_state = lambda a, tail_shape: a[N_META:N_META + dec].reshape((1, dec, 1) + tail_shape)
    return (y_main.reshape(batch, seq, D_MODEL), y_aux[N_META:N_META + dec].reshape(dec, 1, D_MODEL),
            kv_state(sk_st), kv_state(sv_st), kv_state(fk_st), kv_state(fv_st),
            prompt_state(lf_a, lf_m, (N_Q_HEADS,)),
            sample_state(sk_a, kv_shape), sample_state(sv_a, kv_shape),
            sample_state(fk_a, kv_shape), sample_state(fv_a, kv_shape),
            sample_state(lf_a, (N_Q_HEADS,)))
```
